```python
import jax, jax.numpy as jnp
from jax import lax
import numpy as np

D_MODEL = 1024
BATCH = 32
SEQ = 256
DEPTH = 2
DEC_BATCH = 8
DEC_SEQ = 1024
PAST_LEN = 256

GRID_W = 64
N_DIR = 2
H_GLA = 4
DK_GLA = 128
DV_GLA = 256
GLA_RANK = 16
GLA_TAU = 16.0
GLA_CHUNK = 16
H_MLSTM = 4
DH_MLSTM = 256
MLSTM_CHUNK = 64
CONV_W = 3
D_FF = 2816
N_MOD = 9
LN_EPS = 1e-5
NORM_EPS = 1e-6
QK_GLA = H_GLA * DK_GLA
V_GLA = H_GLA * DV_GLA
W_MLSTM = H_MLSTM * DH_MLSTM
IN_SIZES = (QK_GLA, QK_GLA, V_GLA, V_GLA, GLA_RANK, GLA_RANK,
            2 * W_MLSTM, W_MLSTM, W_MLSTM, H_MLSTM, H_MLSTM, H_MLSTM, H_MLSTM,
            D_MODEL, D_MODEL)
D_IN = sum(IN_SIZES)

kernel_name = 'hybrid_gla_mlstm_diffusion_step'


def _layer_norm(x, g, b):
    xf = x.astype(jnp.float32)
    mu = xf.mean(-1, keepdims=True)
    var = jnp.square(xf - mu).mean(-1, keepdims=True)
    return ((xf - mu) * lax.rsqrt(var + LN_EPS) * g + b).astype(x.dtype)


def _head_rms(x, g):
    xf = x.astype(jnp.float32)
    return xf * lax.rsqrt(jnp.mean(jnp.square(xf), -1, keepdims=True) + NORM_EPS) * g


def _heads(t, n_heads):
    b, t_len, ch = t.shape
    return t.reshape(b, t_len, n_heads, ch // n_heads).transpose(0, 2, 1, 3)


def _merge_heads(t):
    b, h, t_len, d = t.shape
    return t.transpose(0, 2, 1, 3).reshape(b, t_len, h * d)


def _to_chunks(t, chunk):
    b, h, t_len = t.shape[:3]
    t = t.reshape((b, h, t_len // chunk, chunk) + t.shape[3:])
    return jnp.moveaxis(t, 2, 0)


def _from_chunks(t):
    t = jnp.moveaxis(t, 0, 2)
    b, h, n, c = t.shape[:4]
    return t.reshape((b, h, n * c) + t.shape[4:])


def _gla_scan(q, k, v, log_a, s0):
    f32 = jnp.float32
    mask = jnp.tril(jnp.ones((GLA_CHUNK, GLA_CHUNK), bool))

    def step(s, inp):
        qc, kc, vc, ac = inp
        cum = jnp.cumsum(ac, axis=-2)
        qd = qc * jnp.exp(cum)
        kd = kc * jnp.exp(-cum)
        att = jnp.where(mask, jnp.einsum('bhtd,bhsd->bhts', qd, kd), 0.0)
        o = jnp.einsum('bhts,bhsv->bhtv', att, vc) + jnp.einsum('bhtd,bhdv->bhtv', qd, s)
        cum_end = cum[..., -1:, :]
        s_new = jnp.exp(cum_end[..., 0, :])[..., None] * s + jnp.einsum('bhsd,bhsv->bhdv', kc * jnp.exp(cum_end - cum), vc)
        return s_new, o

    xs = tuple(_to_chunks(t.astype(f32), GLA_CHUNK) for t in (q, k, v, log_a))
    s_fin, o = lax.scan(step, s0.astype(f32), xs)
    return _from_chunks(o), s_fin


def _mlstm_scan(q, k, v, i_pre, log_f, c0, n0, m0):
    f32 = jnp.float32
    mask = jnp.tril(jnp.ones((MLSTM_CHUNK, MLSTM_CHUNK), bool))

    def step(carry, inp):
        c, n, m = carry
        qc, kc, vc, ic, fc = inp
        cum = jnp.cumsum(fc, axis=-1)
        log_d = jnp.where(mask, cum[..., :, None] - cum[..., None, :] + ic[..., None, :], -jnp.inf)
        log_inter = cum + m[..., None]
        m_t = jnp.maximum(log_inter, log_d.max(-1))
        d = jnp.exp(log_d - m_t[..., None])
        inter = jnp.exp(log_inter - m_t)
        s = jnp.einsum('bhtd,bhsd->bhts', qc, kc) * d
        num = jnp.einsum('bhts,bhsv->bhtv', s, vc) + inter[..., None] * jnp.einsum('bhtd,bhdv->bhtv', qc, c)
        den = s.sum(-1) + inter * jnp.einsum('bhtd,bhd->bht', qc, n)
        h = num / jnp.maximum(jnp.abs(den), jnp.exp(-m_t))[..., None]
        cum_end = cum[..., -1]
        log_w = cum_end[..., None] - cum + ic
        m_new = jnp.maximum(cum_end + m, log_w.max(-1))
        w = jnp.exp(log_w - m_new[..., None])
        decay = jnp.exp(cum_end + m - m_new)
        c_new = decay[..., None, None] * c + jnp.einsum('bhsd,bhsv->bhdv', kc * w[..., None], vc)
        n_new = decay[..., None] * n + jnp.einsum('bhs,bhsd->bhd', w, kc)
        return (c_new, n_new, m_new), h

    xs = tuple(_to_chunks(t.astype(f32), MLSTM_CHUNK) for t in (q, k, v, i_pre, log_f))
    carry, h = lax.scan(step, (c0.astype(f32), n0.astype(f32), m0.astype(f32)), xs)
    return _from_chunks(h), carry


def _centred_conv(x, w, b):
    t_len = x.shape[1]
    pad = CONV_W // 2
    xp = jnp.pad(x, ((0, 0), (pad, pad), (0, 0)))
    return sum(xp[:, j:j + t_len] * w[j] for j in range(CONV_W)) + b


def _swiglu(h, w_gate, w_up, w_down):
    return (jax.nn.silu(h @ w_gate) * (h @ w_up)) @ w_down


def _grid_pos_embed(t_len, dtype):
    rows = t_len // GRID_W
    r = jnp.repeat(jnp.arange(rows), GRID_W).astype(jnp.float32)
    col = jnp.tile(jnp.arange(GRID_W), rows).astype(jnp.float32)
    nf = D_MODEL // 4
    omega = 1.0 / (10000.0 ** (jnp.arange(nf, dtype=jnp.float32) / nf))
    er = r[:, None] * omega
    ec = col[:, None] * omega
    return jnp.concatenate([jnp.sin(er), jnp.cos(er), jnp.sin(ec), jnp.cos(ec)], axis=-1).astype(dtype)


def _mixer(h, st, l, p):
    s_gla, c_m, n_m, m_m = st
    z = h @ p['w_in'][l]
    points = [int(v) for v in np.cumsum(IN_SIZES)[:-1]]
    (q_g, k_g, v_g, r_g, a_f, a_b, qk_m, v_m, o_m,
     i_f, f_f, i_b, f_b, g_gla, g_mlstm) = jnp.split(z, points, axis=-1)
    flip = lambda t: jnp.flip(t, axis=2)
    to_bht = lambda t: jnp.moveaxis(t, -1, 1)

    q_g = _heads(q_g, H_GLA) * DK_GLA ** -0.5
    k_g = _heads(k_g, H_GLA)
    v_g = _heads(v_g, H_GLA)
    la_f = _heads(jax.nn.log_sigmoid(a_f @ p['w_decay'][l, 0] + p['b_decay'][l, 0]) / GLA_TAU, H_GLA)
    la_b = _heads(jax.nn.log_sigmoid(a_b @ p['w_decay'][l, 1] + p['b_decay'][l, 1]) / GLA_TAU, H_GLA)
    o_f, s_f = _gla_scan(q_g, k_g, v_g, la_f, s_gla[:, 0])
    o_b, s_b = _gla_scan(flip(q_g), flip(k_g), flip(v_g), flip(la_b), s_gla[:, 1])
    o_g = _merge_heads(_head_rms(o_f + flip(o_b), p['gla_norm_g'][l])).astype(h.dtype)
    y_g = (o_g * jax.nn.silu(r_g)) @ p['w_br_gla'][l]

    qk_m = jax.nn.silu(_centred_conv(qk_m, p['w_conv'][l], p['b_conv'][l]))
    q_m, k_m = jnp.split(qk_m, 2, axis=-1)
    q_m = _heads(q_m, H_MLSTM) * DH_MLSTM ** -0.5
    k_m = _heads(k_m, H_MLSTM)
    v_m = _heads(v_m, H_MLSTM)
    lf_f = to_bht(jax.nn.log_sigmoid(f_f + p['f_bias'][l, 0]))
    lf_b = to_bht(jax.nn.log_sigmoid(f_b + p['f_bias'][l, 1]))
    h_f, (c_f, n_f, m_f) = _mlstm_scan(q_m, k_m, v_m, to_bht(i_f), lf_f, c_m[:, 0], n_m[:, 0], m_m[:, 0])
    h_b, (c_b, n_b, m_b) = _mlstm_scan(flip(q_m), flip(k_m), flip(v_m), flip(to_bht(i_b)), flip(lf_b),
                                       c_m[:, 1], n_m[:, 1], m_m[:, 1])
    h_m = _merge_heads(_head_rms(h_f + flip(h_b), p['mlstm_norm_g'][l])).astype(h.dtype)
    y_m = (jax.nn.sigmoid(o_m) * h_m) @ p['w_br_mlstm'][l]

    y = (jax.nn.sigmoid(g_gla) * y_g + jax.nn.sigmoid(g_mlstm) * y_m) @ p['w_out'][l]
    new_st = (jnp.stack([s_f, s_b], axis=1), jnp.stack([c_f, c_b], axis=1),
              jnp.stack([n_f, n_b], axis=1), jnp.stack([m_f, m_b], axis=1))
    return y, new_st


def _layer(x, cond, st, l, p, alpha):
    ada = (jax.nn.silu(cond) @ p['w_ada'][l] + p['b_ada'][l]).reshape(cond.shape[0], N_MOD, 1, D_MODEL)
    sh1, sc1, g1, sh2, sc2, g2, sh3, sc3, g3 = [ada[:, j] for j in range(N_MOD)]
    f1 = _swiglu(x * (1.0 + sc1) + sh1, p['ffn1_w_gate'][l], p['ffn1_w_up'][l], p['ffn1_w_down'][l])
    x = _layer_norm(alpha * x + 0.5 * g1 * f1, p['ln_g'][l, 0], p['ln_b'][l, 0])
    y, new_st = _mixer(x * (1.0 + sc2) + sh2, st, l, p)
    x = _layer_norm(alpha * x + g2 * y, p['ln_g'][l, 1], p['ln_b'][l, 1])
    f2 = _swiglu(x * (1.0 + sc3) + sh3, p['ffn2_w_gate'][l], p['ffn2_w_up'][l], p['ffn2_w_down'][l])
    x = _layer_norm(alpha * x + 0.5 * g3 * f2, p['ln_g'][l, 2], p['ln_b'][l, 2])
    return x, new_st


def setup_inputs(seed: int = 0) -> dict:
    keys = iter(jax.random.split(jax.random.key(seed), 40))

    def nrm(shape, scale):
        return jax.random.normal(next(keys), shape, jnp.float32) * scale

    beta = (8.0 * DEPTH) ** -0.25
    d_in_scale = D_MODEL ** -0.5
    return {
        'x_prompt': nrm((BATCH, SEQ, D_MODEL), 1.0),
        'x_sample': nrm((DEC_BATCH, DEC_SEQ, D_MODEL), 1.0),
        'c': nrm((DEC_BATCH, D_MODEL), 1.0),
        'state_gla_s': nrm((DEC_BATCH, DEPTH, N_DIR, H_GLA, DK_GLA, DV_GLA), 0.5),
        'state_mlstm_c': nrm((DEC_BATCH, DEPTH, N_DIR, H_MLSTM, DH_MLSTM, DH_MLSTM), 0.1),
        'state_mlstm_n': nrm((DEC_BATCH, DEPTH, N_DIR, H_MLSTM, DH_MLSTM), 0.1),
        'state_mlstm_m': nrm((DEC_BATCH, DEPTH, N_DIR, H_MLSTM), 1.0),
        'c_ctx': nrm((D_MODEL,), 1.0),
        'w_ada': nrm((DEPTH, D_MODEL, N_MOD * D_MODEL), 0.5 * d_in_scale),
        'b_ada': nrm((DEPTH, N_MOD * D_MODEL), 0.02),
        'ffn1_w_gate': nrm((DEPTH, D_MODEL, D_FF), d_in_scale),
        'ffn1_w_up': nrm((DEPTH, D_MODEL, D_FF), d_in_scale),
        'ffn1_w_down': nrm((DEPTH, D_FF, D_MODEL), beta * D_FF ** -0.5),
        'w_in': nrm((DEPTH, D_MODEL, D_IN), d_in_scale),
        'w_decay': nrm((DEPTH, N_DIR, GLA_RANK, QK_GLA), GLA_RANK ** -0.5),
        'b_decay': nrm((DEPTH, N_DIR, QK_GLA), 0.1),
        'w_conv': nrm((DEPTH, CONV_W, 2 * W_MLSTM), CONV_W ** -0.5),
        'b_conv': nrm((DEPTH, 2 * W_MLSTM), 0.02),
        'f_bias': 3.0 + nrm((DEPTH, N_DIR, H_MLSTM), 0.1),
        'gla_norm_g': 1.0 + nrm((DEPTH, DV_GLA), 0.05),
        'mlstm_norm_g': 1.0 + nrm((DEPTH, DH_MLSTM), 0.05),
        'w_br_gla': nrm((DEPTH, V_GLA, D_MODEL), V_GLA ** -0.5),
        'w_br_mlstm': nrm((DEPTH, W_MLSTM, D_MODEL), W_MLSTM ** -0.5),
        'w_out': nrm((DEPTH, D_MODEL, D_MODEL), beta * d_in_scale),
        'ffn2_w_gate': nrm((DEPTH, D_MODEL, D_FF), d_in_scale),
        'ffn2_w_up': nrm((DEPTH, D_MODEL, D_FF), d_in_scale),
        'ffn2_w_down': nrm((DEPTH, D_FF, D_MODEL), beta * D_FF ** -0.5),
        'ln_g': 1.0 + nrm((DEPTH, 3, D_MODEL), 0.05),
        'ln_b': nrm((DEPTH, 3, D_MODEL), 0.02),
    }


def reference(x_prompt, x_sample, c, state_gla_s, state_mlstm_c, state_mlstm_n, state_mlstm_m, c_ctx,
              w_ada, b_ada, ffn1_w_gate, ffn1_w_up, ffn1_w_down, w_in, w_decay, b_decay, w_conv, b_conv,
              f_bias, gla_norm_g, mlstm_norm_g, w_br_gla, w_br_mlstm, w_out,
              ffn2_w_gate, ffn2_w_up, ffn2_w_down, ln_g, ln_b):
    p = dict(w_ada=w_ada, b_ada=b_ada, ffn1_w_gate=ffn1_w_gate, ffn1_w_up=ffn1_w_up,
             ffn1_w_down=ffn1_w_down, w_in=w_in, w_decay=w_decay, b_decay=b_decay,
             w_conv=w_conv, b_conv=b_conv, f_bias=f_bias, gla_norm_g=gla_norm_g,
             mlstm_norm_g=mlstm_norm_g, w_br_gla=w_br_gla, w_br_mlstm=w_br_mlstm, w_out=w_out,
             ffn2_w_gate=ffn2_w_gate, ffn2_w_up=ffn2_w_up, ffn2_w_down=ffn2_w_down,
             ln_g=ln_g, ln_b=ln_b)
    alpha = (2.0 * DEPTH) ** 0.25
    f32 = jnp.float32

    bp = x_prompt.shape[0]
    zero_st = (jnp.zeros((bp, N_DIR, H_GLA, DK_GLA, DV_GLA), f32),
               jnp.zeros((bp, N_DIR, H_MLSTM, DH_MLSTM, DH_MLSTM), f32),
               jnp.zeros((bp, N_DIR, H_MLSTM, DH_MLSTM), f32),
               jnp.zeros((bp, N_DIR, H_MLSTM), f32))
    h = x_prompt
    gla_s, mc, mn, mm = [], [], [], []
    for l in range(DEPTH):
        h, (s_l, c_l, n_l, m_l) = _layer(h, c_ctx[None, :], zero_st, l, p, alpha)
        gla_s.append(s_l)
        mc.append(c_l)
        mn.append(n_l)
        mm.append(m_l)
    y_prompt = h

    h = x_sample + _grid_pos_embed(x_sample.shape[1], x_sample.dtype)[None]
    for l in range(DEPTH):
        st = (state_gla_s[:, l], state_mlstm_c[:, l], state_mlstm_n[:, l], state_mlstm_m[:, l])
        h, _ = _layer(h, c, st, l, p, alpha)
    y_sample = h

    new_gla_s = jnp.stack(gla_s, axis=1)
    new_mlstm_c = jnp.stack(mc, axis=1)
    new_mlstm_n = jnp.stack(mn, axis=1)
    new_mlstm_m = jnp.stack(mm, axis=1)
    return (y_prompt, y_sample, new_gla_s, new_mlstm_c, new_mlstm_n, new_mlstm_m)
```

```python
import functools

import jax
import jax.numpy as jnp
import numpy as np
from jax import lax
from jax.experimental import pallas as pl
from jax.experimental.pallas import tpu as pltpu

F32 = jnp.float32
BF16 = jnp.bfloat16

D_MODEL = 1024
DEPTH = 2
GRID_W = 64
N_DIR = 2
H_GLA = 4
DK_GLA = 128
DV_GLA = 256
GLA_RANK = 16
GLA_TAU = 16.0
H_MLSTM = 4
DH_MLSTM = 256
D_FF = 2816
N_MOD = 9
LN_EPS = 1e-5
NORM_EPS = 1e-6
QK_GLA = H_GLA * DK_GLA
V_GLA = H_GLA * DV_GLA
W_MLSTM = H_MLSTM * DH_MLSTM
ALPHA = (2.0 * DEPTH) ** 0.25

Z_BIG = 9 * 1024
Z_SMALL = 128
COL_I = (32, 40)
COL_F = (36, 44)
ROW_I = (0, 8)
ROW_F = (4, 12)

LANE = 128
VMEM_LIMIT = 56 * 1024 * 1024

FFN_TM = 512
FFN_TF = 1408
PROJ_TM = 1024
PROJ_TN = 1024
MIX_TM = 512
GLA_CHUNK = 64
GLA_SUB = 16
MLSTM_CHUNK = 128
FIN_ROWS = 256


def _dot(a, b):
    return jnp.dot(a, b, preferred_element_type=F32)


def _dot_nt(a, b):
    return lax.dot_general(a, b, (((1,), (1,)), ((), ())), preferred_element_type=F32)


def _dot_tn(a, b):
    return lax.dot_general(a, b, (((0,), (0,)), ((), ())), preferred_element_type=F32)


def _split3(x):
    hi = x.astype(BF16)
    r = x - hi.astype(F32)
    mid = r.astype(BF16)
    lo = (r - mid.astype(F32)).astype(BF16)
    return hi, mid, lo


def _split2(x):
    hi = x.astype(BF16)
    return hi, (x - hi.astype(F32)).astype(BF16)


def _dot_acc(x, w):
    xh, xm, _ = _split3(x)
    wh, wl = _split2(w)
    return _dot(xh, wh) + _dot(xh, wl) + _dot(xm, wh)


def _silu(x):
    return x * jax.nn.sigmoid(x)


def _log_sigmoid(x):
    return jnp.minimum(x, 0.0) - jnp.log1p(jnp.exp(-jnp.abs(x)))


def _layer_norm(y, g, b):
    mu = jnp.mean(y, axis=-1, keepdims=True)
    d = y - mu
    var = jnp.mean(d * d, axis=-1, keepdims=True)
    return d * lax.rsqrt(var + LN_EPS) * g + b


def _cparams(n_axes):
    return pltpu.CompilerParams(
        dimension_semantics=("arbitrary",) * n_axes, vmem_limit_bytes=VMEM_LIMIT)


def _ada_kernel(c_ref, w_ref, b_ref, o_ref):
    o_ref[...] = _dot_acc(_silu(c_ref[...]), w_ref[...]) + b_ref[...]


def _ada_call(cond16, w_ada, b_ada):
    return pl.pallas_call(
        _ada_kernel,
        grid=(DEPTH, N_MOD),
        in_specs=[
            pl.BlockSpec((16, D_MODEL), lambda l, j: (0, 0)),
            pl.BlockSpec((None, D_MODEL, D_MODEL), lambda l, j: (l, 0, j)),
            pl.BlockSpec((None, None, 1, D_MODEL), lambda l, j: (l, j, 0, 0)),
        ],
        out_specs=pl.BlockSpec((None, None, 16, D_MODEL), lambda l, j: (l, j, 0, 0)),
        out_shape=jax.ShapeDtypeStruct((DEPTH, N_MOD, 16, D_MODEL), F32),
        compiler_params=_cparams(2),
        name="ada_mod",
    )(cond16, w_ada, b_ada.reshape(DEPTH, N_MOD, 1, D_MODEL))


def _cond_row(i, cond_row0, tiles_per_cond):
    return cond_row0 + lax.div(i, jnp.int32(tiles_per_cond))


def _ffn_kernel(*refs, has_pos, cond_row0, tiles_per_cond, n_f):
    if has_pos:
        x_ref, pos_ref, mod_ref, wg_ref, wu_ref, wd_ref, lg_ref, lb_ref, o_ref, hm_s, acc_s = refs
    else:
        x_ref, mod_ref, wg_ref, wu_ref, wd_ref, lg_ref, lb_ref, o_ref, hm_s, acc_s = refs
        pos_ref = None
    i = pl.program_id(0)
    f = pl.program_id(1)
    r = _cond_row(i, cond_row0, tiles_per_cond)

    def load_x():
        x = x_ref[...]
        return x + pos_ref[...] if has_pos else x

    @pl.when(f == 0)
    def _():
        sh = mod_ref[0, pl.ds(r, 1), :]
        sc = mod_ref[1, pl.ds(r, 1), :]
        hm_s[...] = (load_x() * (1.0 + sc) + sh).astype(BF16)
        acc_s[...] = jnp.zeros_like(acc_s)

    hm = hm_s[...]
    g = _dot(hm, wg_ref[...])
    u = _dot(hm, wu_ref[...])
    acc_s[...] += _dot((_silu(g) * u).astype(BF16), wd_ref[...])

    @pl.when(f == n_f - 1)
    def _():
        gate = mod_ref[2, pl.ds(r, 1), :]
        y = ALPHA * load_x() + (0.5 * gate) * acc_s[...]
        o_ref[...] = _layer_norm(y, lg_ref[...], lb_ref[...])


def _ffn_call(x, pos, ada4, wg, wu, wd, lng, lnb, *, layer, sub, cond_row0, rows_per_cond):
    m = x.shape[0]
    tm, tf = FFN_TM, FFN_TF
    n_f = D_FF // tf
    has_pos = pos is not None
    kern = functools.partial(_ffn_kernel, has_pos=has_pos, cond_row0=cond_row0,
                             tiles_per_cond=rows_per_cond // tm, n_f=n_f)
    in_specs = [pl.BlockSpec((tm, D_MODEL), lambda i, f: (i, 0))]
    args = [x]
    if has_pos:
        n_pos = pos.shape[0] // tm
        in_specs.append(pl.BlockSpec((tm, D_MODEL), lambda i, f: (lax.rem(i, jnp.int32(n_pos)), 0)))
        args.append(pos)
    in_specs += [
        pl.BlockSpec((None, 3, 16, D_MODEL), lambda i, f: (layer, sub, 0, 0)),
        pl.BlockSpec((None, None, D_MODEL, tf), lambda i, f: (layer, f, 0, 0)),
        pl.BlockSpec((None, None, D_MODEL, tf), lambda i, f: (layer, f, 0, 0)),
        pl.BlockSpec((None, None, tf, D_MODEL), lambda i, f: (layer, f, 0, 0)),
        pl.BlockSpec((None, 1, D_MODEL), lambda i, f: (3 * layer + sub, 0, 0)),
        pl.BlockSpec((None, 1, D_MODEL), lambda i, f: (3 * layer + sub, 0, 0)),
    ]
    args += [ada4, wg, wu, wd, lng, lnb]
    return pl.pallas_call(
        kern,
        grid=(m // tm, n_f),
        in_specs=in_specs,
        out_specs=pl.BlockSpec((tm, D_MODEL), lambda i, f: (i, 0)),
        out_shape=jax.ShapeDtypeStruct((m, D_MODEL), F32),
        scratch_shapes=[pltpu.VMEM((tm, D_MODEL), BF16), pltpu.VMEM((tm, D_MODEL), F32)],
        compiler_params=_cparams(2),
        name="ffn_ln",
    )(*args)


def _proj_kernel(x_ref, mod_ref, wb_ref, ws_ref, wst_ref, zb_ref, zs_ref, zst_ref, hm_s,
                 *, cond_row0, tiles_per_cond):
    i = pl.program_id(0)
    j = pl.program_id(1)

    @pl.when(j == 0)
    def _():
        r = _cond_row(i, cond_row0, tiles_per_cond)
        sh = mod_ref[0, pl.ds(r, 1), :]
        sc = mod_ref[1, pl.ds(r, 1), :]
        hm = x_ref[...] * (1.0 + sc) + sh
        hh, hmid, _ = _split3(hm)
        hm_s[...] = hh
        wh, wl = _split2(ws_ref[...])
        zs_ref[...] = _dot(hh, wh) + _dot(hh, wl) + _dot(hmid, wh)
        wth, wtl = _split2(wst_ref[...])
        zst_ref[...] = _dot_nt(wth, hh) + _dot_nt(wtl, hh) + _dot_nt(wth, hmid)

    zb_ref[...] = _dot(hm_s[...], wb_ref[...]).astype(BF16)


def _proj_call(x, ada4, w_big, w_small, w_small_t, *, layer, cond_row0, rows_per_cond):
    m = x.shape[0]
    tm, tn = PROJ_TM, PROJ_TN
    kern = functools.partial(_proj_kernel, cond_row0=cond_row0, tiles_per_cond=rows_per_cond // tm)
    return pl.pallas_call(
        kern,
        grid=(m // tm, Z_BIG // tn),
        in_specs=[
            pl.BlockSpec((tm, D_MODEL), lambda i, j: (i, 0)),
            pl.BlockSpec((None, 3, 16, D_MODEL), lambda i, j: (layer, 1, 0, 0)),
            pl.BlockSpec((None, D_MODEL, tn), lambda i, j: (layer, 0, j)),
            pl.BlockSpec((None, D_MODEL, Z_SMALL), lambda i, j: (layer, 0, 0)),
            pl.BlockSpec((None, 16, D_MODEL), lambda i, j: (layer, 0, 0)),
        ],
        out_specs=[
            pl.BlockSpec((tm, tn), lambda i, j: (i, j)),
            pl.BlockSpec((tm, Z_SMALL), lambda i, j: (i, 0)),
            pl.BlockSpec((16, tm), lambda i, j: (0, i)),
        ],
        out_shape=[
            jax.ShapeDtypeStruct((m, Z_BIG), BF16),
            jax.ShapeDtypeStruct((m, Z_SMALL), F32),
            jax.ShapeDtypeStruct((16, m), F32),
        ],
        scratch_shapes=[pltpu.VMEM((tm, D_MODEL), BF16)],
        compiler_params=_cparams(2),
        name="mixer_in_proj",
    )(x, ada4, w_big, w_small, w_small_t)


def _gla_kernel(*refs, seq_len, has_state, emit_state):
    it = iter(refs)
    q_ref, k_ref, v_ref, r_ref, zs_ref, wdec_ref, bdec_ref, gn_ref = (next(it) for _ in range(8))
    s0_ref = next(it) if has_state else None
    og_ref = next(it)
    sout_ref = next(it) if emit_state else None
    la_s, st_s, oacc_s, cum_s = (next(it) for _ in range(4))
    t_len = seq_len
    c_len = GLA_CHUNK
    n_chunks = t_len // c_len

    for rb in range(t_len // FIN_ROWS):
        rows = pl.ds(rb * FIN_ROWS, FIN_ROWS)
        zh, zm, _ = _split3(zs_ref[rows, :])
        for d in range(N_DIR):
            wh, wl = _split2(wdec_ref[d])
            x = _dot(zh, wh) + _dot(zh, wl) + _dot(zm, wh) + bdec_ref[d]
            la_s[d, rows, :] = _log_sigmoid(x) * (1.0 / GLA_TAU)

    for d in range(N_DIR):
        for h in range(H_GLA):
            if has_state:
                st_s[d, h] = s0_ref[d, h].T
            else:
                st_s[d, h] = jnp.zeros((DV_GLA, DK_GLA), F32)
    oacc_s[...] = jnp.zeros_like(oacc_s)

    ri = lax.broadcasted_iota(jnp.int32, (c_len, c_len), 0)
    ci = lax.broadcasted_iota(jnp.int32, (c_len, c_len), 1)
    rr = lax.broadcasted_iota(jnp.int32, (c_len, DK_GLA), 0)
    odd_rows = (lax.shift_right_logical(rr, 4) & 1) == 1
    upper_rows = rr >= 2 * GLA_SUB
    same_sub = lax.shift_right_logical(ri, 4) == lax.shift_right_logical(ci, 4)
    same_half = lax.shift_right_logical(ri, 5) == lax.shift_right_logical(ci, 5)
    causal = (ri >= ci, ri <= ci)
    tri = tuple(jnp.where(c, 1.0, 0.0).astype(BF16) for c in causal)
    pm0 = tuple(same_sub & c for c in causal)
    q1rows = (odd_rows, jnp.logical_not(odd_rows))
    q2rows = (upper_rows, jnp.logical_not(upper_rows))

    def bcast(x, n):
        return jnp.broadcast_to(x, (n, DK_GLA))

    def chunk(d, row0):
        rows = pl.ds(row0, c_len)
        l3 = _split3(la_s[d, rows, :])
        cum_s[d] = _dot(tri[d], l3[0]) + _dot(tri[d], l3[1]) + _dot(tri[d], l3[2])
        for h in range(H_GLA):
            ls = slice(DK_GLA * h, DK_GLA * (h + 1))
            vs = slice(DV_GLA * h, DV_GLA * (h + 1))

            def row(i):
                return cum_s[d, pl.ds(i, 1), ls]

            c = cum_s[d, :, ls]
            zero = jnp.zeros((GLA_SUB, DK_GLA), F32)
            if d == 0:
                ref0 = jnp.concatenate([zero, bcast(row(15), 16), bcast(row(31), 16), bcast(row(47), 16)], axis=0)
                ref1 = jnp.concatenate([bcast(row(15), 32), bcast(row(47), 32)], axis=0)
                ref2, cend = row(31), row(63)
            else:
                ref0 = jnp.concatenate([bcast(row(16), 16), bcast(row(32), 16), bcast(row(48), 16), zero], axis=0)
                ref1 = jnp.concatenate([bcast(row(16), 32), bcast(row(48), 32)], axis=0)
                ref2, cend = row(32), row(0)
            q = q_ref[rows, ls].astype(F32)
            k = k_ref[rows, ls].astype(F32)
            v = v_ref[rows, vs]
            e0 = c - ref0
            p0 = _dot_nt((q * jnp.exp(e0)).astype(BF16), (k * jnp.exp(-e0)).astype(BF16))
            x1 = jnp.exp(-jnp.abs(c - ref1))
            p1 = _dot_nt(jnp.where(q1rows[d], q * x1, 0.0).astype(BF16),
                         jnp.where(q1rows[d], 0.0, k * x1).astype(BF16))
            x2 = jnp.exp(-jnp.abs(c - ref2))
            p2 = _dot_nt(jnp.where(q2rows[d], q * x2, 0.0).astype(BF16),
                         jnp.where(q2rows[d], 0.0, k * x2).astype(BF16))
            p = jnp.where(pm0[d], p0, jnp.where(same_half, p1, p2))
            st = st_s[d, h]
            o = _dot(p.astype(BF16), v) + _dot_nt((q * jnp.exp(c)).astype(BF16), st.astype(BF16))
            oacc_s[rows, vs] += o
            kst = (k * jnp.exp(cend - c)).astype(BF16)
            st_s[d, h] = st * jnp.exp(cend) + _dot_tn(v, kst)

    def body(j, carry):
        chunk(0, pl.multiple_of(j * c_len, c_len))
        chunk(1, pl.multiple_of((n_chunks - 1 - j) * c_len, c_len))
        return carry

    lax.fori_loop(0, n_chunks, body, 0)

    g = gn_ref[...]
    for rb in range(t_len // FIN_ROWS):
        rows = pl.ds(rb * FIN_ROWS, FIN_ROWS)
        for h in range(H_GLA):
            vs = slice(DV_GLA * h, DV_GLA * (h + 1))
            o = oacc_s[rows, vs] * (DK_GLA ** -0.5)
            on = o * lax.rsqrt(jnp.mean(o * o, axis=-1, keepdims=True) + NORM_EPS) * g
            og_ref[rows, vs] = (on * _silu(r_ref[rows, vs].astype(F32))).astype(BF16)

    if emit_state:
        for d in range(N_DIR):
            for h in range(H_GLA):
                sout_ref[d, h] = st_s[d, h].T


def _gla_call(zb, zs, wdec, bdec, gnorm, state, *, layer, n_seq, seq_len, emit_state):
    has_state = state is not None
    t = seq_len
    kern = functools.partial(_gla_kernel, seq_len=t, has_state=has_state, emit_state=emit_state)
    in_specs = [
        pl.BlockSpec((t, QK_GLA), lambda b: (b, 0)),
        pl.BlockSpec((t, QK_GLA), lambda b: (b, 1)),
        pl.BlockSpec((t, V_GLA), lambda b: (b, 1)),
        pl.BlockSpec((t, V_GLA), lambda b: (b, 2)),
        pl.BlockSpec((t, Z_SMALL), lambda b: (b, 0)),
        pl.BlockSpec((None, N_DIR, Z_SMALL, QK_GLA), lambda b: (layer, 0, 0, 0)),
        pl.BlockSpec((None, N_DIR, 1, QK_GLA), lambda b: (layer, 0, 0, 0)),
        pl.BlockSpec((None, 1, DV_GLA), lambda b: (layer, 0, 0)),
    ]
    args = [zb, zb, zb, zb, zs, wdec, bdec, gnorm]
    if has_state:
        in_specs.append(pl.BlockSpec((None, None, N_DIR, H_GLA, DK_GLA, DV_GLA),
                                     lambda b: (b, layer, 0, 0, 0, 0)))
        args.append(state)
    out_specs = [pl.BlockSpec((t, V_GLA), lambda b: (b, 0))]
    out_shape = [jax.ShapeDtypeStruct((n_seq * t, V_GLA), BF16)]
    if emit_state:
        out_specs.append(pl.BlockSpec((None, N_DIR, H_GLA, DK_GLA, DV_GLA), lambda b: (b, 0, 0, 0, 0)))
        out_shape.append(jax.ShapeDtypeStruct((n_seq, N_DIR, H_GLA, DK_GLA, DV_GLA), F32))
    return pl.pallas_call(
        kern,
        grid=(n_seq,),
        in_specs=in_specs,
        out_specs=out_specs,
        out_shape=out_shape,
        scratch_shapes=[
            pltpu.VMEM((N_DIR, t, QK_GLA), F32),
            pltpu.VMEM((N_DIR, H_GLA, DV_GLA, DK_GLA), F32),
            pltpu.VMEM((t, V_GLA), F32),
            pltpu.VMEM((N_DIR, GLA_CHUNK, QK_GLA), F32),
        ],
        compiler_params=_cparams(1),
        name="gla_scan",
    )(*args)


def _mlstm_kernel(*refs, seq_len, has_state, emit_state):
    it = iter(refs)
    (q_ref, k_ref, v_ref, om_ref, zs_ref, zst_ref, wconv_ref, bconv_ref,
     bcol_ref, brow_ref, gn_ref) = (next(it) for _ in range(11))
    if has_state:
        c0_ref, n0_ref, m0_ref = (next(it) for _ in range(3))
    hm_ref = next(it)
    if emit_state:
        cout_ref, nout_ref, mout_ref = (next(it) for _ in range(3))
    (qc_s, kc_s, xp_s, lf_s, lft_s, c_s, n_s, m_s, hacc_s, cumc_s, cumr_s) = (next(it) for _ in range(11))
    t_len = seq_len
    c_len = MLSTM_CHUNK
    n_chunks = t_len // c_len
    dh = DH_MLSTM

    xp_s[pl.ds(0, 8), :] = jnp.zeros((8, dh), F32)
    xp_s[pl.ds(8 + t_len, 8), :] = jnp.zeros((8, dh), F32)
    for which, src in enumerate((q_ref, k_ref)):
        for h in range(H_MLSTM):
            hs = slice(dh * h, dh * (h + 1))
            ws = slice(which * W_MLSTM + dh * h, which * W_MLSTM + dh * (h + 1))
            xp_s[pl.ds(8, t_len), :] = src[:, hs].astype(F32)
            w0 = wconv_ref[pl.ds(0, 1), ws]
            w1 = wconv_ref[pl.ds(1, 1), ws]
            w2 = wconv_ref[pl.ds(2, 1), ws]
            bias = bconv_ref[:, ws]
            for rb in range(t_len // FIN_ROWS):
                r0 = rb * FIN_ROWS
                y = (w0 * xp_s[pl.ds(r0 + 7, FIN_ROWS), :] + w1 * xp_s[pl.ds(r0 + 8, FIN_ROWS), :]
                     + w2 * xp_s[pl.ds(r0 + 9, FIN_ROWS), :] + bias)
                a = _silu(y)
                if which == 0:
                    qc_s[pl.ds(r0, FIN_ROWS), hs] = (a * (dh ** -0.5)).astype(BF16)
                else:
                    kc_s[pl.ds(r0, FIN_ROWS), hs] = a

    lf_s[...] = _log_sigmoid(zs_ref[...] + bcol_ref[...])
    zt = zst_ref[...]
    trow = lax.broadcasted_iota(jnp.int32, zt.shape, 0)
    is_f = (lax.shift_right_logical(trow, 2) & 1) == 1
    lft_s[...] = jnp.where(is_f, _log_sigmoid(zt + brow_ref[...]), zt)

    for d in range(N_DIR):
        for h in range(H_MLSTM):
            if has_state:
                c_s[d, h] = c0_ref[d, h]
                n_s[d, h] = n0_ref[d, pl.ds(h, 1), :]
                m_s[d, h] = m0_ref[pl.ds(d, 1), pl.ds(h, 1)]
            else:
                c_s[d, h] = jnp.zeros((dh, dh), F32)
                n_s[d, h] = jnp.zeros((1, dh), F32)
                m_s[d, h] = jnp.zeros((1, 1), F32)
    hacc_s[...] = jnp.zeros_like(hacc_s)

    ri = lax.broadcasted_iota(jnp.int32, (c_len, c_len), 0)
    ci = lax.broadcasted_iota(jnp.int32, (c_len, c_len), 1)
    causal = (ri >= ci, ri <= ci)
    tri_c = tuple(jnp.where(c, 1.0, 0.0).astype(BF16) for c in causal)
    tri_r = (tri_c[1], tri_c[0])

    def chunk(d, row0):
        rows = pl.ds(row0, c_len)
        l3 = _split3(lf_s[rows, :])
        cumc_s[d] = _dot(tri_c[d], l3[0]) + _dot(tri_c[d], l3[1]) + _dot(tri_c[d], l3[2])
        r3 = _split3(lft_s[:, rows])
        cumr_s[d] = _dot(r3[0], tri_r[d]) + _dot(r3[1], tri_r[d]) + _dot(r3[2], tri_r[d])
        end = c_len - 1 if d == 0 else 0
        for h in range(H_MLSTM):
            hs = slice(dh * h, dh * (h + 1))
            cf = COL_F[d] + h
            cum_c = cumc_s[d, :, pl.ds(cf, 1)]
            cum_end = cumc_s[d, pl.ds(end, 1), pl.ds(cf, 1)]
            i_c = zs_ref[rows, pl.ds(COL_I[d] + h, 1)]
            cum_r = cumr_s[d, pl.ds(ROW_F[d] + h, 1), :]
            i_r = lft_s[pl.ds(ROW_I[d] + h, 1), rows]
            m_prev = m_s[d, h]
            log_d = jnp.where(causal[d], cum_c - cum_r + i_r, -jnp.inf)
            log_int = cum_c + m_prev
            m_t = jnp.maximum(log_int, jnp.max(log_d, axis=1, keepdims=True))
            dmat = jnp.exp(log_d - m_t)
            inter = jnp.exp(log_int - m_t)
            q = qc_s[rows, hs]
            kf = kc_s[rows, hs]
            v = v_ref[rows, hs]
            s = _dot_nt(q, kf.astype(BF16)) * dmat
            cst = c_s[d, h]
            nst = n_s[d, h]
            num = _dot(s.astype(BF16), v) + inter * _dot(q, cst.astype(BF16))
            den = (jnp.sum(s, axis=1, keepdims=True)
                   + inter * jnp.sum(q.astype(F32) * nst, axis=1, keepdims=True))
            hacc_s[rows, hs] += num / jnp.maximum(jnp.abs(den), jnp.exp(-m_t))
            log_w = cum_end - cum_c + i_c
            m_new = jnp.maximum(cum_end + m_prev, jnp.max(log_w, axis=0, keepdims=True))
            kw = kf * jnp.exp(log_w - m_new)
            decay = jnp.exp(cum_end + m_prev - m_new)
            c_s[d, h] = decay * cst + _dot_tn(kw.astype(BF16), v)
            n_s[d, h] = decay * nst + jnp.sum(kw, axis=0, keepdims=True)
            m_s[d, h] = m_new

    def body(j, carry):
        chunk(0, pl.multiple_of(j * c_len, c_len))
        chunk(1, pl.multiple_of((n_chunks - 1 - j) * c_len, c_len))
        return carry

    lax.fori_loop(0, n_chunks, body, 0)

    g = gn_ref[...]
    for rb in range(t_len // FIN_ROWS):
        rows = pl.ds(rb * FIN_ROWS, FIN_ROWS)
        for h in range(H_MLSTM):
            hs = slice(dh * h, dh * (h + 1))
            o = hacc_s[rows, hs]
            on = o * lax.rsqrt(jnp.mean(o * o, axis=-1, keepdims=True) + NORM_EPS) * g
            hm_ref[rows, hs] = (jax.nn.sigmoid(om_ref[rows, hs].astype(F32)) * on).astype(BF16)

    if emit_state:
        for d in range(N_DIR):
            for h in range(H_MLSTM):
                cout_ref[d, h] = c_s[d, h]
                nout_ref[d, pl.ds(h, 1), :] = n_s[d, h]
                mout_ref[pl.ds(d, 1), pl.ds(h, 1)] = m_s[d, h]


def _mlstm_call(zb, zs, zst, wconv, bconv, bcol, brow, gnorm, states, *, layer, n_seq, seq_len, emit_state):
    has_state = states is not None
    t = seq_len
    dh = DH_MLSTM
    kern = functools.partial(_mlstm_kernel, seq_len=t, has_state=has_state, emit_state=emit_state)
    in_specs = [
        pl.BlockSpec((t, W_MLSTM), lambda b: (b, 3)),
        pl.BlockSpec((t, W_MLSTM), lambda b: (b, 4)),
        pl.BlockSpec((t, W_MLSTM), lambda b: (b, 5)),
        pl.BlockSpec((t, W_MLSTM), lambda b: (b, 6)),
        pl.BlockSpec((t, Z_SMALL), lambda b: (b, 0)),
        pl.BlockSpec((16, t), lambda b: (0, b)),
        pl.BlockSpec((None, 3, 2 * W_MLSTM), lambda b: (layer, 0, 0)),
        pl.BlockSpec((None, 1, 2 * W_MLSTM), lambda b: (layer, 0, 0)),
        pl.BlockSpec((None, 1, Z_SMALL), lambda b: (layer, 0, 0)),
        pl.BlockSpec((None, 16, 1), lambda b: (layer, 0, 0)),
        pl.BlockSpec((None, 1, dh), lambda b: (layer, 0, 0)),
    ]
    args = [zb, zb, zb, zb, zs, zst, wconv, bconv, bcol, brow, gnorm]
    if has_state:
        c0, n0, m0 = states
        in_specs += [
            pl.BlockSpec((None, None, N_DIR, H_MLSTM, dh, dh), lambda b: (b, layer, 0, 0, 0, 0)),
            pl.BlockSpec((None, None, N_DIR, H_MLSTM, dh), lambda b: (b, layer, 0, 0, 0)),
            pl.BlockSpec((None, None, N_DIR, H_MLSTM), lambda b: (b, layer, 0, 0)),
        ]
        args += [c0, n0, m0]
    out_specs = [pl.BlockSpec((t, W_MLSTM), lambda b: (b, 0))]
    out_shape = [jax.ShapeDtypeStruct((n_seq * t, W_MLSTM), BF16)]
    if emit_state:
        out_specs += [
            pl.BlockSpec((None, N_DIR, H_MLSTM, dh, dh), lambda b: (b, 0, 0, 0, 0)),
            pl.BlockSpec((None, N_DIR, H_MLSTM, dh), lambda b: (b, 0, 0, 0)),
            pl.BlockSpec((None, N_DIR, H_MLSTM), lambda b: (b, 0, 0)),
        ]
        out_shape += [
            jax.ShapeDtypeStruct((n_seq, N_DIR, H_MLSTM, dh, dh), F32),
            jax.ShapeDtypeStruct((n_seq, N_DIR, H_MLSTM, dh), F32),
            jax.ShapeDtypeStruct((n_seq, N_DIR, H_MLSTM), F32),
        ]
    return pl.pallas_call(
        kern,
        grid=(n_seq,),
        in_specs=in_specs,
        out_specs=out_specs,
        out_shape=out_shape,
        scratch_shapes=[
            pltpu.VMEM((t, W_MLSTM), BF16),
            pltpu.VMEM((t, W_MLSTM), F32),
            pltpu.VMEM((t + 16, dh), F32),
            pltpu.VMEM((t, Z_SMALL), F32),
            pltpu.VMEM((16, t), F32),
            pltpu.VMEM((N_DIR, H_MLSTM, dh, dh), F32),
            pltpu.VMEM((N_DIR, H_MLSTM, 1, dh), F32),
            pltpu.VMEM((N_DIR, H_MLSTM, 1, 1), F32),
            pltpu.VMEM((t, W_MLSTM), F32),
            pltpu.VMEM((N_DIR, MLSTM_CHUNK, Z_SMALL), F32),
            pltpu.VMEM((N_DIR, 16, MLSTM_CHUNK), F32),
        ],
        compiler_params=_cparams(1),
        name="mlstm_scan",
    )(*args)


def _mix_kernel(x_ref, og_ref, hm_ref, gg_ref, gm_ref, mod_ref, wbg_ref, wbm_ref, wo_ref,
                lg_ref, lb_ref, o_ref, *, cond_row0, tiles_per_cond):
    r = _cond_row(pl.program_id(0), cond_row0, tiles_per_cond)
    y_g = _dot(og_ref[...], wbg_ref[...])
    y_m = _dot(hm_ref[...], wbm_ref[...])
    mix = (jax.nn.sigmoid(gg_ref[...].astype(F32)) * y_g
           + jax.nn.sigmoid(gm_ref[...].astype(F32)) * y_m)
    y = _dot(mix.astype(BF16), wo_ref[...])
    gate = mod_ref[2, pl.ds(r, 1), :]
    o_ref[...] = _layer_norm(ALPHA * x_ref[...] + gate * y, lg_ref[...], lb_ref[...])


def _mix_call(x, og, hm, zb, ada4, wbg, wbm, wo, lng, lnb, *, layer, cond_row0, rows_per_cond):
    m = x.shape[0]
    tm = MIX_TM
    kern = functools.partial(_mix_kernel, cond_row0=cond_row0, tiles_per_cond=rows_per_cond // tm)
    wspec = pl.BlockSpec((None, D_MODEL, D_MODEL), lambda i: (layer, 0, 0))
    return pl.pallas_call(
        kern,
        grid=(m // tm,),
        in_specs=[
            pl.BlockSpec((tm, D_MODEL), lambda i: (i, 0)),
            pl.BlockSpec((tm, V_GLA), lambda i: (i, 0)),
            pl.BlockSpec((tm, W_MLSTM), lambda i: (i, 0)),
            pl.BlockSpec((tm, D_MODEL), lambda i: (i, 7)),
            pl.BlockSpec((tm, D_MODEL), lambda i: (i, 8)),
            pl.BlockSpec((None, 3, 16, D_MODEL), lambda i: (layer, 1, 0, 0)),
            wspec, wspec, wspec,
            pl.BlockSpec((None, 1, D_MODEL), lambda i: (3 * layer + 1, 0, 0)),
            pl.BlockSpec((None, 1, D_MODEL), lambda i: (3 * layer + 1, 0, 0)),
        ],
        out_specs=pl.BlockSpec((tm, D_MODEL), lambda i: (i, 0)),
        out_shape=jax.ShapeDtypeStruct((m, D_MODEL), F32),
        compiler_params=_cparams(1),
        name="mixer_out_ln",
    )(x, og, hm, zb, zb, ada4, wbg, wbm, wo, lng, lnb)


def _grid_pos_embed(t_len):
    rows = t_len // GRID_W
    r = jnp.repeat(jnp.arange(rows), GRID_W).astype(F32)
    col = jnp.tile(jnp.arange(GRID_W), rows).astype(F32)
    nf = D_MODEL // 4
    omega = 1.0 / (10000.0 ** (jnp.arange(nf, dtype=F32) / nf))
    er = r[:, None] * omega
    ec = col[:, None] * omega
    return jnp.concatenate([jnp.sin(er), jnp.cos(er), jnp.sin(ec), jnp.cos(ec)], axis=-1)


def _pack_ffn(w_gate, w_up, w_down):
    n_f = D_FF // FFN_TF
    wg = w_gate.astype(BF16).reshape(DEPTH, D_MODEL, n_f, FFN_TF).transpose(0, 2, 1, 3)
    wu = w_up.astype(BF16).reshape(DEPTH, D_MODEL, n_f, FFN_TF).transpose(0, 2, 1, 3)
    wd = w_down.astype(BF16).reshape(DEPTH, n_f, FFN_TF, D_MODEL)
    return wg, wu, wd


def _pack_w_in(w_in):
    o = np.cumsum((0, QK_GLA, QK_GLA, V_GLA, V_GLA, GLA_RANK, GLA_RANK, 2 * W_MLSTM, W_MLSTM, W_MLSTM,
                   H_MLSTM, H_MLSTM, H_MLSTM, H_MLSTM, D_MODEL, D_MODEL))
    o = [int(v) for v in o]
    big = jnp.concatenate([w_in[:, :, o[0]:o[4]], w_in[:, :, o[6]:o[9]], w_in[:, :, o[13]:o[15]]],
                          axis=-1).astype(BF16)
    small = jnp.concatenate([w_in[:, :, o[4]:o[6]], w_in[:, :, o[9]:o[13]]], axis=-1)
    small_p = jnp.pad(small, ((0, 0), (0, 0), (0, Z_SMALL - small.shape[-1])))
    small_t = jnp.swapaxes(w_in[:, :, o[9]:o[13]], 1, 2)
    return big, small_p, small_t


def kernel(x_prompt, x_sample, c, state_gla_s, state_mlstm_c, state_mlstm_n, state_mlstm_m, c_ctx,
           w_ada, b_ada, ffn1_w_gate, ffn1_w_up, ffn1_w_down, w_in, w_decay, b_decay, w_conv, b_conv,
           f_bias, gla_norm_g, mlstm_norm_g, w_br_gla, w_br_mlstm, w_out,
           ffn2_w_gate, ffn2_w_up, ffn2_w_down, ln_g, ln_b):
    bp, tp, _ = x_prompt.shape
    bs, ts, _ = x_sample.shape
    assert bs + 1 <= 16 and tp % FIN_ROWS == 0 and ts % FIN_ROWS == 0

    cond16 = jnp.zeros((16, D_MODEL), F32).at[0].set(c_ctx).at[1:1 + bs].set(c)
    ada4 = _ada_call(cond16, w_ada, b_ada)

    ffn1 = _pack_ffn(ffn1_w_gate, ffn1_w_up, ffn1_w_down)
    ffn2 = _pack_ffn(ffn2_w_gate, ffn2_w_up, ffn2_w_down)
    w_big, w_small, w_small_t = _pack_w_in(w_in)
    wdec = jnp.zeros((DEPTH, N_DIR, Z_SMALL, QK_GLA), F32)
    wdec = wdec.at[:, 0, 0:GLA_RANK].set(w_decay[:, 0]).at[:, 1, GLA_RANK:2 * GLA_RANK].set(w_decay[:, 1])
    bdec = b_decay.reshape(DEPTH, N_DIR, 1, QK_GLA)
    bcol = jnp.zeros((DEPTH, 1, Z_SMALL), F32)
    brow = jnp.zeros((DEPTH, 16, 1), F32)
    for d in range(N_DIR):
        bcol = bcol.at[:, 0, COL_F[d]:COL_F[d] + H_MLSTM].set(f_bias[:, d])
        brow = brow.at[:, ROW_F[d]:ROW_F[d] + H_MLSTM, 0].set(f_bias[:, d])
    bconv = b_conv.reshape(DEPTH, 1, 2 * W_MLSTM)
    gn_gla = gla_norm_g.reshape(DEPTH, 1, DV_GLA)
    gn_ml = mlstm_norm_g.reshape(DEPTH, 1, DH_MLSTM)
    wbg, wbm, wo = w_br_gla.astype(BF16), w_br_mlstm.astype(BF16), w_out.astype(BF16)
    lng = ln_g.reshape(DEPTH * 3, 1, D_MODEL)
    lnb = ln_b.reshape(DEPTH * 3, 1, D_MODEL)
    pos = _grid_pos_embed(ts)

    def run_pass(x, n_seq, seq_len, cond_row0, rows_per_cond, pos_embed, states):
        emit = states is None
        outs = []
        kw = dict(cond_row0=cond_row0, rows_per_cond=rows_per_cond)
        for l in range(DEPTH):
            x = _ffn_call(x, pos_embed if l == 0 else None, ada4, *ffn1, lng, lnb, layer=l, sub=0, **kw)
            zb, zs, zst = _proj_call(x, ada4, w_big, w_small, w_small_t, layer=l, **kw)
            gla = _gla_call(zb, zs, wdec, bdec, gn_gla, None if emit else states[0],
                            layer=l, n_seq=n_seq, seq_len=seq_len, emit_state=emit)
            ml = _mlstm_call(zb, zs, zst, w_conv, bconv, bcol, brow, gn_ml, None if emit else states[1:],
                             layer=l, n_seq=n_seq, seq_len=seq_len, emit_state=emit)
            x = _mix_call(x, gla[0], ml[0], zb, ada4, wbg, wbm, wo, lng, lnb, layer=l, **kw)
            x = _ffn_call(x, None, ada4, *ffn2, lng, lnb, layer=l, sub=2, **kw)
            if emit:
                outs.append((gla[1], ml[1], ml[2], ml[3]))
        return x, outs

    m_ctx = bp * tp
    y_p, st = run_pass(x_prompt.reshape(m_ctx, D_MODEL), bp, tp, 0, m_ctx, None, None)
    y_s, _ = run_pass(x_sample.reshape(bs * ts, D_MODEL), bs, ts, 1, ts, pos,
                      (state_gla_s, state_mlstm_c, state_mlstm_n, state_mlstm_m))
    new_states = tuple(jnp.stack([st[l][k] for l in range(DEPTH)], axis=1) for k in range(4))
    return (y_p.reshape(bp, tp, D_MODEL), y_s.reshape(bs, ts, D_MODEL)) + new_states
```

```python
import functools

import jax
import jax.numpy as jnp
import numpy as np
from jax import lax
from jax.experimental import pallas as pl
from jax.experimental.pallas import tpu as pltpu

F32 = jnp.float32
BF16 = jnp.bfloat16

D_MODEL = 1024
DEPTH = 2
GRID_W = 64
N_DIR = 2
H_GLA = 4
DK_GLA = 128
DV_GLA = 256
GLA_RANK = 16
GLA_TAU = 16.0
H_MLSTM = 4
DH_MLSTM = 256
D_FF = 2816
N_MOD = 9
LN_EPS = 1e-5
NORM_EPS = 1e-6
QK_GLA = H_GLA * DK_GLA
V_GLA = H_GLA * DV_GLA
W_MLSTM = H_MLSTM * DH_MLSTM
ALPHA = (2.0 * DEPTH) ** 0.25

Z_BIG = 9 * 1024
Z_SMALL = 128
COL_I = (32, 40)
COL_F = (36, 44)
ROW_I = (0, 8)
ROW_F = (4, 12)

LANE = 128
VMEM_LIMIT = 56 * 1024 * 1024

FFN_TM = 1024
FFN_SUB = 512
FFN_CHUNKS = ((0, 1536), (1536, 1280))
PROJ_TM = 1024
PROJ_TN = 1024
MIX_TM = 512
GLA_CHUNK = 64
GLA_SUB = 16
MLSTM_CHUNK = 128
FIN_ROWS = 256


def _dot(a, b):
    return jnp.dot(a, b, preferred_element_type=F32)


def _dot_nt(a, b):
    return lax.dot_general(a, b, (((1,), (1,)), ((), ())), preferred_element_type=F32)


def _dot_tn(a, b):
    return lax.dot_general(a, b, (((0,), (0,)), ((), ())), preferred_element_type=F32)


def _split3(x):
    hi = x.astype(BF16)
    r = x - hi.astype(F32)
    mid = r.astype(BF16)
    lo = (r - mid.astype(F32)).astype(BF16)
    return hi, mid, lo


def _split2(x):
    hi = x.astype(BF16)
    return hi, (x - hi.astype(F32)).astype(BF16)


def _dot_acc(x, w):
    xh, xm, _ = _split3(x)
    wh, wl = _split2(w)
    return _dot(xh, wh) + _dot(xh, wl) + _dot(xm, wh)


def _silu(x):
    return x * jax.nn.sigmoid(x)


def _log_sigmoid(x):
    return jnp.minimum(x, 0.0) - jnp.log1p(jnp.exp(-jnp.abs(x)))


def _layer_norm(y, g, b):
    mu = jnp.mean(y, axis=-1, keepdims=True)
    d = y - mu
    var = jnp.mean(d * d, axis=-1, keepdims=True)
    return d * lax.rsqrt(var + LN_EPS) * g + b


def _cparams(n_axes):
    return pltpu.CompilerParams(
        dimension_semantics=("arbitrary",) * n_axes, vmem_limit_bytes=VMEM_LIMIT)


def _ada_kernel(c_ref, w_ref, b_ref, o_ref):
    o_ref[...] = _dot_acc(_silu(c_ref[...]), w_ref[...]) + b_ref[...]


def _ada_call(cond16, w_ada, b_ada):
    return pl.pallas_call(
        _ada_kernel,
        grid=(DEPTH, N_MOD),
        in_specs=[
            pl.BlockSpec((16, D_MODEL), lambda l, j: (0, 0)),
            pl.BlockSpec((None, D_MODEL, D_MODEL), lambda l, j: (l, 0, j)),
            pl.BlockSpec((None, None, 1, D_MODEL), lambda l, j: (l, j, 0, 0)),
        ],
        out_specs=pl.BlockSpec((None, None, 16, D_MODEL), lambda l, j: (l, j, 0, 0)),
        out_shape=jax.ShapeDtypeStruct((DEPTH, N_MOD, 16, D_MODEL), F32),
        compiler_params=_cparams(2),
        name="ada_mod",
    )(cond16, w_ada, b_ada.reshape(DEPTH, N_MOD, 1, D_MODEL))


def _cond_row(i, cond_row0, tiles_per_cond):
    return cond_row0 + lax.div(i, jnp.int32(tiles_per_cond))


def _ffn_kernel(*refs, has_pos, cond_row0, tiles_per_cond):
    if has_pos:
        x_ref, pos_ref, mod_ref, wg_ref, wu_ref, wd_ref, lg_ref, lb_ref, o_ref = refs
    else:
        x_ref, mod_ref, wg_ref, wu_ref, wd_ref, lg_ref, lb_ref, o_ref = refs
        pos_ref = None
    r = _cond_row(pl.program_id(0), cond_row0, tiles_per_cond)
    sh = mod_ref[0, pl.ds(r, 1), :]
    sc = mod_ref[1, pl.ds(r, 1), :]
    gate = mod_ref[2, pl.ds(r, 1), :]
    for s in range(FFN_TM // FFN_SUB):
        rows = pl.ds(s * FFN_SUB, FFN_SUB)
        x = x_ref[rows, :]
        if has_pos:
            x = x + pos_ref[rows, :]
        hm = (x * (1.0 + sc) + sh).astype(BF16)
        acc = None
        for c0, cw in FFN_CHUNKS:
            g = _dot(hm, wg_ref[:, c0:c0 + cw])
            u = _dot(hm, wu_ref[:, c0:c0 + cw])
            part = _dot((_silu(g) * u).astype(BF16), wd_ref[c0:c0 + cw, :])
            acc = part if acc is None else acc + part
        y = ALPHA * x + (0.5 * gate) * acc
        o_ref[rows, :] = _layer_norm(y, lg_ref[...], lb_ref[...])


def _ffn_call(x, pos, ada4, wg, wu, wd, lng, lnb, *, layer, sub, cond_row0, rows_per_cond):
    m = x.shape[0]
    tm = FFN_TM
    has_pos = pos is not None
    kern = functools.partial(_ffn_kernel, has_pos=has_pos, cond_row0=cond_row0,
                             tiles_per_cond=rows_per_cond // tm)
    once = pl.Buffered(1)
    in_specs = [pl.BlockSpec((tm, D_MODEL), lambda i: (i, 0))]
    args = [x]
    if has_pos:
        assert pos.shape[0] == tm
        in_specs.append(pl.BlockSpec((tm, D_MODEL), lambda i: (0, 0), pipeline_mode=once))
        args.append(pos)
    in_specs += [
        pl.BlockSpec((None, 3, 16, D_MODEL), lambda i: (layer, sub, 0, 0)),
        pl.BlockSpec((None, D_MODEL, D_FF), lambda i: (layer, 0, 0), pipeline_mode=once),
        pl.BlockSpec((None, D_MODEL, D_FF), lambda i: (layer, 0, 0), pipeline_mode=once),
        pl.BlockSpec((None, D_FF, D_MODEL), lambda i: (layer, 0, 0), pipeline_mode=once),
        pl.BlockSpec((None, 1, D_MODEL), lambda i: (3 * layer + sub, 0, 0)),
        pl.BlockSpec((None, 1, D_MODEL), lambda i: (3 * layer + sub, 0, 0)),
    ]
    args += [ada4, wg, wu, wd, lng, lnb]
    return pl.pallas_call(
        kern,
        grid=(m // tm,),
        in_specs=in_specs,
        out_specs=pl.BlockSpec((tm, D_MODEL), lambda i: (i, 0)),
        out_shape=jax.ShapeDtypeStruct((m, D_MODEL), F32),
        compiler_params=_cparams(1),
        name="ffn_ln",
    )(*args)


def _proj_kernel(x_ref, mod_ref, wb_ref, ws_ref, zb_ref, zs_ref, zst_ref, hm_s,
                 *, cond_row0, tiles_per_cond):
    i = pl.program_id(0)
    j = pl.program_id(1)

    @pl.when(j == 0)
    def _():
        r = _cond_row(i, cond_row0, tiles_per_cond)
        sh = mod_ref[0, pl.ds(r, 1), :]
        sc = mod_ref[1, pl.ds(r, 1), :]
        hm = (x_ref[...] * (1.0 + sc) + sh).astype(BF16)
        hm_s[...] = hm
        zs = _dot(hm, ws_ref[...])
        zs_ref[...] = zs
        zst_ref[...] = zs.T[COL_I[0]:COL_I[0] + 16, :]

    zb_ref[...] = _dot(hm_s[...], wb_ref[...]).astype(BF16)


def _proj_call(x, ada4, w_big, w_small, *, layer, cond_row0, rows_per_cond):
    m = x.shape[0]
    tm, tn = PROJ_TM, PROJ_TN
    kern = functools.partial(_proj_kernel, cond_row0=cond_row0, tiles_per_cond=rows_per_cond // tm)
    return pl.pallas_call(
        kern,
        grid=(m // tm, Z_BIG // tn),
        in_specs=[
            pl.BlockSpec((tm, D_MODEL), lambda i, j: (i, 0)),
            pl.BlockSpec((None, 3, 16, D_MODEL), lambda i, j: (layer, 1, 0, 0)),
            pl.BlockSpec((None, D_MODEL, tn), lambda i, j: (layer, 0, j)),
            pl.BlockSpec((None, D_MODEL, Z_SMALL), lambda i, j: (layer, 0, 0)),
        ],
        out_specs=[
            pl.BlockSpec((tm, tn), lambda i, j: (i, j)),
            pl.BlockSpec((tm, Z_SMALL), lambda i, j: (i, 0)),
            pl.BlockSpec((16, tm), lambda i, j: (0, i)),
        ],
        out_shape=[
            jax.ShapeDtypeStruct((m, Z_BIG), BF16),
            jax.ShapeDtypeStruct((m, Z_SMALL), F32),
            jax.ShapeDtypeStruct((16, m), F32),
        ],
        scratch_shapes=[pltpu.VMEM((tm, D_MODEL), BF16)],
        compiler_params=_cparams(2),
        name="mixer_in_proj",
    )(x, ada4, w_big, w_small)


def _gla_kernel(*refs, seq_len, has_state, emit_state, n_carried):
    it = iter(refs)
    q_ref, k_ref, v_ref, r_ref, zs_ref, wdec_ref, bdec_ref, gn_ref = (next(it) for _ in range(8))
    s0_ref = next(it) if has_state else None
    if n_carried:
        next(it)
    og_ref = next(it)
    sout_ref = next(it) if emit_state else None
    la_s, st_s, oacc_s, cum_s = (next(it) for _ in range(4))
    t_len = seq_len
    c_len = GLA_CHUNK
    n_chunks = t_len // c_len

    for rb in range(t_len // FIN_ROWS):
        rows = pl.ds(rb * FIN_ROWS, FIN_ROWS)
        zh, zm, _ = _split3(zs_ref[rows, :])
        for d in range(N_DIR):
            wh, wl = _split2(wdec_ref[d])
            x = _dot(zh, wh) + _dot(zh, wl) + _dot(zm, wh) + bdec_ref[d]
            la_s[d, rows, :] = _log_sigmoid(x) * (1.0 / GLA_TAU)

    for d in range(N_DIR):
        for h in range(H_GLA):
            if has_state:
                st_s[d, h] = s0_ref[d, h].T
            else:
                st_s[d, h] = jnp.zeros((DV_GLA, DK_GLA), F32)
    oacc_s[...] = jnp.zeros_like(oacc_s)

    ri = lax.broadcasted_iota(jnp.int32, (c_len, c_len), 0)
    ci = lax.broadcasted_iota(jnp.int32, (c_len, c_len), 1)
    rr = lax.broadcasted_iota(jnp.int32, (c_len, DK_GLA), 0)
    odd_rows = (lax.shift_right_logical(rr, 4) & 1) == 1
    upper_rows = rr >= 2 * GLA_SUB
    same_sub = lax.shift_right_logical(ri, 4) == lax.shift_right_logical(ci, 4)
    same_half = lax.shift_right_logical(ri, 5) == lax.shift_right_logical(ci, 5)
    causal = (ri >= ci, ri <= ci)
    tri = tuple(jnp.where(c, 1.0, 0.0).astype(BF16) for c in causal)
    pm0 = tuple(same_sub & c for c in causal)
    q1rows = (odd_rows, jnp.logical_not(odd_rows))
    q2rows = (upper_rows, jnp.logical_not(upper_rows))

    def bcast(x, n):
        return jnp.broadcast_to(x, (n, DK_GLA))

    def chunk(d, row0):
        rows = pl.ds(row0, c_len)
        l3 = _split3(la_s[d, rows, :])
        cum_s[d] = _dot(tri[d], l3[0]) + _dot(tri[d], l3[1]) + _dot(tri[d], l3[2])
        for h in range(H_GLA):
            ls = slice(DK_GLA * h, DK_GLA * (h + 1))
            vs = slice(DV_GLA * h, DV_GLA * (h + 1))

            def row(i):
                return cum_s[d, pl.ds(i, 1), ls]

            c = cum_s[d, :, ls]
            zero = jnp.zeros((GLA_SUB, DK_GLA), F32)
            if d == 0:
                ref0 = jnp.concatenate([zero, bcast(row(15), 16), bcast(row(31), 16), bcast(row(47), 16)], axis=0)
                ref1 = jnp.concatenate([bcast(row(15), 32), bcast(row(47), 32)], axis=0)
                ref2, cend = row(31), row(63)
            else:
                ref0 = jnp.concatenate([bcast(row(16), 16), bcast(row(32), 16), bcast(row(48), 16), zero], axis=0)
                ref1 = jnp.concatenate([bcast(row(16), 32), bcast(row(48), 32)], axis=0)
                ref2, cend = row(32), row(0)
            q = q_ref[rows, ls].astype(F32)
            k = k_ref[rows, ls].astype(F32)
            v = v_ref[rows, vs]
            e0 = c - ref0
            p0 = _dot_nt((q * jnp.exp(e0)).astype(BF16), (k * jnp.exp(-e0)).astype(BF16))
            x1 = jnp.exp(-jnp.abs(c - ref1))
            p1 = _dot_nt(jnp.where(q1rows[d], q * x1, 0.0).astype(BF16),
                         jnp.where(q1rows[d], 0.0, k * x1).astype(BF16))
            x2 = jnp.exp(-jnp.abs(c - ref2))
            p2 = _dot_nt(jnp.where(q2rows[d], q * x2, 0.0).astype(BF16),
                         jnp.where(q2rows[d], 0.0, k * x2).astype(BF16))
            p = jnp.where(pm0[d], p0, jnp.where(same_half, p1, p2))
            st = st_s[d, h]
            o = _dot(p.astype(BF16), v) + _dot_nt((q * jnp.exp(c)).astype(BF16), st.astype(BF16))
            oacc_s[rows, vs] += o
            kst = (k * jnp.exp(cend - c)).astype(BF16)
            st_s[d, h] = st * jnp.exp(cend) + _dot_tn(v, kst)

    def body(j, carry):
        chunk(0, pl.multiple_of(j * c_len, c_len))
        chunk(1, pl.multiple_of((n_chunks - 1 - j) * c_len, c_len))
        return carry

    lax.fori_loop(0, n_chunks, body, 0)

    g = gn_ref[...]
    for rb in range(t_len // FIN_ROWS):
        rows = pl.ds(rb * FIN_ROWS, FIN_ROWS)
        for h in range(H_GLA):
            vs = slice(DV_GLA * h, DV_GLA * (h + 1))
            o = oacc_s[rows, vs] * (DK_GLA ** -0.5)
            on = o * lax.rsqrt(jnp.mean(o * o, axis=-1, keepdims=True) + NORM_EPS) * g
            og_ref[rows, vs] = (on * _silu(r_ref[rows, vs].astype(F32))).astype(BF16)

    if emit_state:
        for d in range(N_DIR):
            for h in range(H_GLA):
                sout_ref[d, h] = st_s[d, h].T


def _gla_call(zb, zs, wdec, bdec, gnorm, state, carried, *, layer, n_seq, seq_len, emit_state):
    has_state = state is not None
    t = seq_len
    n_carried = 0 if carried is None else 1
    kern = functools.partial(_gla_kernel, seq_len=t, has_state=has_state, emit_state=emit_state,
                             n_carried=n_carried)
    in_specs = [
        pl.BlockSpec((t, QK_GLA), lambda b: (b, 0)),
        pl.BlockSpec((t, QK_GLA), lambda b: (b, 1)),
        pl.BlockSpec((t, V_GLA), lambda b: (b, 1)),
        pl.BlockSpec((t, V_GLA), lambda b: (b, 2)),
        pl.BlockSpec((t, Z_SMALL), lambda b: (b, 0)),
        pl.BlockSpec((None, N_DIR, Z_SMALL, QK_GLA), lambda b: (layer, 0, 0, 0)),
        pl.BlockSpec((None, N_DIR, 1, QK_GLA), lambda b: (layer, 0, 0, 0)),
        pl.BlockSpec((None, 1, DV_GLA), lambda b: (layer, 0, 0)),
    ]
    args = [zb, zb, zb, zb, zs, wdec, bdec, gnorm]
    if has_state:
        in_specs.append(pl.BlockSpec((None, None, N_DIR, H_GLA, DK_GLA, DV_GLA),
                                     lambda b: (b, layer, 0, 0, 0, 0)))
        args.append(state)
    aliases = {}
    if n_carried:
        aliases[len(args)] = 1
        in_specs.append(pl.BlockSpec(memory_space=pl.ANY))
        args.append(carried)
    out_specs = [pl.BlockSpec((t, V_GLA), lambda b: (b, 0))]
    out_shape = [jax.ShapeDtypeStruct((n_seq * t, V_GLA), BF16)]
    if emit_state:
        out_specs.append(pl.BlockSpec((None, None, N_DIR, H_GLA, DK_GLA, DV_GLA),
                                      lambda b: (b, layer, 0, 0, 0, 0)))
        out_shape.append(jax.ShapeDtypeStruct((n_seq, DEPTH, N_DIR, H_GLA, DK_GLA, DV_GLA), F32))
    return pl.pallas_call(
        kern,
        grid=(n_seq,),
        in_specs=in_specs,
        out_specs=out_specs,
        out_shape=out_shape,
        input_output_aliases=aliases,
        scratch_shapes=[
            pltpu.VMEM((N_DIR, t, QK_GLA), F32),
            pltpu.VMEM((N_DIR, H_GLA, DV_GLA, DK_GLA), F32),
            pltpu.VMEM((t, V_GLA), F32),
            pltpu.VMEM((N_DIR, GLA_CHUNK, QK_GLA), F32),
        ],
        compiler_params=_cparams(1),
        name="gla_scan",
    )(*args)


def _mlstm_kernel(*refs, seq_len, has_state, emit_state, n_carried):
    it = iter(refs)
    (q_ref, k_ref, v_ref, om_ref, zs_ref, zst_ref, wconv_ref, bconv_ref,
     bcol_ref, brow_ref, gn_ref) = (next(it) for _ in range(11))
    if has_state:
        c0_ref, n0_ref, m0_ref = (next(it) for _ in range(3))
    for _ in range(n_carried):
        next(it)
    hm_ref = next(it)
    if emit_state:
        cout_ref, nout_ref, mout_ref = (next(it) for _ in range(3))
    (qc_s, kc_s, xp_s, lf_s, lft_s, c_s, n_s, m_s, hacc_s, cumc_s, cumr_s) = (next(it) for _ in range(11))
    t_len = seq_len
    c_len = MLSTM_CHUNK
    n_chunks = t_len // c_len
    dh = DH_MLSTM

    xp_s[pl.ds(0, 8), :] = jnp.zeros((8, dh), F32)
    xp_s[pl.ds(8 + t_len, 8), :] = jnp.zeros((8, dh), F32)
    for which, src in enumerate((q_ref, k_ref)):
        for h in range(H_MLSTM):
            hs = slice(dh * h, dh * (h + 1))
            ws = slice(which * W_MLSTM + dh * h, which * W_MLSTM + dh * (h + 1))
            xp_s[pl.ds(8, t_len), :] = src[:, hs].astype(F32)
            w0 = wconv_ref[pl.ds(0, 1), ws]
            w1 = wconv_ref[pl.ds(1, 1), ws]
            w2 = wconv_ref[pl.ds(2, 1), ws]
            bias = bconv_ref[:, ws]
            for rb in range(t_len // FIN_ROWS):
                r0 = rb * FIN_ROWS
                y = (w0 * xp_s[pl.ds(r0 + 7, FIN_ROWS), :] + w1 * xp_s[pl.ds(r0 + 8, FIN_ROWS), :]
                     + w2 * xp_s[pl.ds(r0 + 9, FIN_ROWS), :] + bias)
                a = _silu(y)
                if which == 0:
                    qc_s[pl.ds(r0, FIN_ROWS), hs] = (a * (dh ** -0.5)).astype(BF16)
                else:
                    kc_s[pl.ds(r0, FIN_ROWS), hs] = a

    lf_s[...] = _log_sigmoid(zs_ref[...] + bcol_ref[...])
    zt = zst_ref[...]
    trow = lax.broadcasted_iota(jnp.int32, zt.shape, 0)
    is_f = (lax.shift_right_logical(trow, 2) & 1) == 1
    lft_s[...] = jnp.where(is_f, _log_sigmoid(zt + brow_ref[...]), zt)

    for d in range(N_DIR):
        for h in range(H_MLSTM):
            if has_state:
                c_s[d, h] = c0_ref[d, h]
                n_s[d, h] = n0_ref[d, pl.ds(h, 1), :]
                m_s[d, h] = m0_ref[pl.ds(d, 1), pl.ds(h, 1)]
            else:
                c_s[d, h] = jnp.zeros((dh, dh), F32)
                n_s[d, h] = jnp.zeros((1, dh), F32)
                m_s[d, h] = jnp.zeros((1, 1), F32)
    hacc_s[...] = jnp.zeros_like(hacc_s)

    ri = lax.broadcasted_iota(jnp.int32, (c_len, c_len), 0)
    ci = lax.broadcasted_iota(jnp.int32, (c_len, c_len), 1)
    causal = (ri >= ci, ri <= ci)
    tri_c = tuple(jnp.where(c, 1.0, 0.0).astype(BF16) for c in causal)
    tri_r = (tri_c[1], tri_c[0])

    def chunk(d, row0):
        rows = pl.ds(row0, c_len)
        l3 = _split3(lf_s[rows, :])
        cumc_s[d] = _dot(tri_c[d], l3[0]) + _dot(tri_c[d], l3[1]) + _dot(tri_c[d], l3[2])
        r3 = _split3(lft_s[:, rows])
        cumr_s[d] = _dot(r3[0], tri_r[d]) + _dot(r3[1], tri_r[d]) + _dot(r3[2], tri_r[d])
        end = c_len - 1 if d == 0 else 0
        for h in range(H_MLSTM):
            hs = slice(dh * h, dh * (h + 1))
            cf = COL_F[d] + h
            cum_c = cumc_s[d, :, pl.ds(cf, 1)]
            cum_end = cumc_s[d, pl.ds(end, 1), pl.ds(cf, 1)]
            i_c = zs_ref[rows, pl.ds(COL_I[d] + h, 1)]
            cum_r = cumr_s[d, pl.ds(ROW_F[d] + h, 1), :]
            i_r = lft_s[pl.ds(ROW_I[d] + h, 1), rows]
            m_prev = m_s[d, h]
            log_d = jnp.where(causal[d], cum_c - cum_r + i_r, -jnp.inf)
            log_int = cum_c + m_prev
            m_t = jnp.maximum(log_int, jnp.max(log_d, axis=1, keepdims=True))
            dmat = jnp.exp(log_d - m_t)
            inter = jnp.exp(log_int - m_t)
            q = qc_s[rows, hs]
            kf = kc_s[rows, hs]
            v = v_ref[rows, hs]
            s = _dot_nt(q, kf.astype(BF16)) * dmat
            cst = c_s[d, h]
            nst = n_s[d, h]
            num = _dot(s.astype(BF16), v) + inter * _dot(q, cst.astype(BF16))
            den = (jnp.sum(s, axis=1, keepdims=True)
                   + inter * jnp.sum(q.astype(F32) * nst, axis=1, keepdims=True))
            hacc_s[rows, hs] += num / jnp.maximum(jnp.abs(den), jnp.exp(-m_t))
            log_w = cum_end - cum_c + i_c
            m_new = jnp.maximum(cum_end + m_prev, jnp.max(log_w, axis=0, keepdims=True))
            kw = kf * jnp.exp(log_w - m_new)
            decay = jnp.exp(cum_end + m_prev - m_new)
            c_s[d, h] = decay * cst + _dot_tn(kw.astype(BF16), v)
            n_s[d, h] = decay * nst + jnp.sum(kw, axis=0, keepdims=True)
            m_s[d, h] = m_new

    def body(j, carry):
        chunk(0, pl.multiple_of(j * c_len, c_len))
        chunk(1, pl.multiple_of((n_chunks - 1 - j) * c_len, c_len))
        return carry

    lax.fori_loop(0, n_chunks, body, 0)

    g = gn_ref[...]
    for rb in range(t_len // FIN_ROWS):
        rows = pl.ds(rb * FIN_ROWS, FIN_ROWS)
        for h in range(H_MLSTM):
            hs = slice(dh * h, dh * (h + 1))
            o = hacc_s[rows, hs]
            on = o * lax.rsqrt(jnp.mean(o * o, axis=-1, keepdims=True) + NORM_EPS) * g
            hm_ref[rows, hs] = (jax.nn.sigmoid(om_ref[rows, hs].astype(F32)) * on).astype(BF16)

    if emit_state:
        for d in range(N_DIR):
            for h in range(H_MLSTM):
                cout_ref[d, h] = c_s[d, h]
                nout_ref[d, pl.ds(h, 1), :] = n_s[d, h]
                mout_ref[pl.ds(d, 1), pl.ds(h, 1)] = m_s[d, h]


def _mlstm_call(zb, zs, zst, wconv, bconv, bcol, brow, gnorm, states, carried, *, layer, n_seq, seq_len,
                emit_state):
    has_state = states is not None
    t = seq_len
    dh = DH_MLSTM
    n_carried = 0 if carried is None else 3
    kern = functools.partial(_mlstm_kernel, seq_len=t, has_state=has_state, emit_state=emit_state,
                             n_carried=n_carried)
    in_specs = [
        pl.BlockSpec((t, W_MLSTM), lambda b: (b, 3)),
        pl.BlockSpec((t, W_MLSTM), lambda b: (b, 4)),
        pl.BlockSpec((t, W_MLSTM), lambda b: (b, 5)),
        pl.BlockSpec((t, W_MLSTM), lambda b: (b, 6)),
        pl.BlockSpec((t, Z_SMALL), lambda b: (b, 0)),
        pl.BlockSpec((16, t), lambda b: (0, b)),
        pl.BlockSpec((None, 3, 2 * W_MLSTM), lambda b: (layer, 0, 0)),
        pl.BlockSpec((None, 1, 2 * W_MLSTM), lambda b: (layer, 0, 0)),
        pl.BlockSpec((None, 1, Z_SMALL), lambda b: (layer, 0, 0)),
        pl.BlockSpec((None, 16, 1), lambda b: (layer, 0, 0)),
        pl.BlockSpec((None, 1, dh), lambda b: (layer, 0, 0)),
    ]
    args = [zb, zb, zb, zb, zs, zst, wconv, bconv, bcol, brow, gnorm]
    if has_state:
        c0, n0, m0 = states
        in_specs += [
            pl.BlockSpec((None, None, N_DIR, H_MLSTM, dh, dh), lambda b: (b, layer, 0, 0, 0, 0)),
            pl.BlockSpec((None, None, N_DIR, H_MLSTM, dh), lambda b: (b, layer, 0, 0, 0)),
            pl.BlockSpec((None, None, N_DIR, H_MLSTM), lambda b: (b, layer, 0, 0)),
        ]
        args += [c0, n0, m0]
    aliases = {}
    for k in range(n_carried):
        aliases[len(args)] = 1 + k
        in_specs.append(pl.BlockSpec(memory_space=pl.ANY))
        args.append(carried[k])
    out_specs = [pl.BlockSpec((t, W_MLSTM), lambda b: (b, 0))]
    out_shape = [jax.ShapeDtypeStruct((n_seq * t, W_MLSTM), BF16)]
    if emit_state:
        out_specs += [
            pl.BlockSpec((None, None, N_DIR, H_MLSTM, dh, dh), lambda b: (b, layer, 0, 0, 0, 0)),
            pl.BlockSpec((None, None, N_DIR, H_MLSTM, dh), lambda b: (b, layer, 0, 0, 0)),
            pl.BlockSpec((None, None, N_DIR, H_MLSTM), lambda b: (b, layer, 0, 0)),
        ]
        out_shape += [
            jax.ShapeDtypeStruct((n_seq, DEPTH, N_DIR, H_MLSTM, dh, dh), F32),
            jax.ShapeDtypeStruct((n_seq, DEPTH, N_DIR, H_MLSTM, dh), F32),
            jax.ShapeDtypeStruct((n_seq, DEPTH, N_DIR, H_MLSTM), F32),
        ]
    return pl.pallas_call(
        kern,
        grid=(n_seq,),
        in_specs=in_specs,
        out_specs=out_specs,
        out_shape=out_shape,
        input_output_aliases=aliases,
        scratch_shapes=[
            pltpu.VMEM((t, W_MLSTM), BF16),
            pltpu.VMEM((t, W_MLSTM), F32),
            pltpu.VMEM((t + 16, dh), F32),
            pltpu.VMEM((t, Z_SMALL), F32),
            pltpu.VMEM((16, t), F32),
            pltpu.VMEM((N_DIR, H_MLSTM, dh, dh), F32),
            pltpu.VMEM((N_DIR, H_MLSTM, 1, dh), F32),
            pltpu.VMEM((N_DIR, H_MLSTM, 1, 1), F32),
            pltpu.VMEM((t, W_MLSTM), F32),
            pltpu.VMEM((N_DIR, MLSTM_CHUNK, Z_SMALL), F32),
            pltpu.VMEM((N_DIR, 16, MLSTM_CHUNK), F32),
        ],
        compiler_params=_cparams(1),
        name="mlstm_scan",
    )(*args)


def _mix_kernel(x_ref, og_ref, hm_ref, gg_ref, gm_ref, mod_ref, wbg_ref, wbm_ref, wo_ref,
                lg_ref, lb_ref, o_ref, *, cond_row0, tiles_per_cond):
    r = _cond_row(pl.program_id(0), cond_row0, tiles_per_cond)
    y_g = _dot(og_ref[...], wbg_ref[...])
    y_m = _dot(hm_ref[...], wbm_ref[...])
    mix = (jax.nn.sigmoid(gg_ref[...].astype(F32)) * y_g
           + jax.nn.sigmoid(gm_ref[...].astype(F32)) * y_m)
    y = _dot(mix.astype(BF16), wo_ref[...])
    gate = mod_ref[2, pl.ds(r, 1), :]
    o_ref[...] = _layer_norm(ALPHA * x_ref[...] + gate * y, lg_ref[...], lb_ref[...])


def _mix_call(x, og, hm, zb, ada4, wbg, wbm, wo, lng, lnb, *, layer, cond_row0, rows_per_cond):
    m = x.shape[0]
    tm = MIX_TM
    kern = functools.partial(_mix_kernel, cond_row0=cond_row0, tiles_per_cond=rows_per_cond // tm)
    wspec = pl.BlockSpec((None, D_MODEL, D_MODEL), lambda i: (layer, 0, 0))
    return pl.pallas_call(
        kern,
        grid=(m // tm,),
        in_specs=[
            pl.BlockSpec((tm, D_MODEL), lambda i: (i, 0)),
            pl.BlockSpec((tm, V_GLA), lambda i: (i, 0)),
            pl.BlockSpec((tm, W_MLSTM), lambda i: (i, 0)),
            pl.BlockSpec((tm, D_MODEL), lambda i: (i, 7)),
            pl.BlockSpec((tm, D_MODEL), lambda i: (i, 8)),
            pl.BlockSpec((None, 3, 16, D_MODEL), lambda i: (layer, 1, 0, 0)),
            wspec, wspec, wspec,
            pl.BlockSpec((None, 1, D_MODEL), lambda i: (3 * layer + 1, 0, 0)),
            pl.BlockSpec((None, 1, D_MODEL), lambda i: (3 * layer + 1, 0, 0)),
        ],
        out_specs=pl.BlockSpec((tm, D_MODEL), lambda i: (i, 0)),
        out_shape=jax.ShapeDtypeStruct((m, D_MODEL), F32),
        compiler_params=_cparams(1),
        name="mixer_out_ln",
    )(x, og, hm, zb, zb, ada4, wbg, wbm, wo, lng, lnb)


def _grid_pos_embed(t_len):
    rows = t_len // GRID_W
    r = jnp.repeat(jnp.arange(rows), GRID_W).astype(F32)
    col = jnp.tile(jnp.arange(GRID_W), rows).astype(F32)
    nf = D_MODEL // 4
    omega = 1.0 / (10000.0 ** (jnp.arange(nf, dtype=F32) / nf))
    er = r[:, None] * omega
    ec = col[:, None] * omega
    return jnp.concatenate([jnp.sin(er), jnp.cos(er), jnp.sin(ec), jnp.cos(ec)], axis=-1)


def _pack_ffn(w_gate, w_up, w_down):
    return w_gate.astype(BF16), w_up.astype(BF16), w_down.astype(BF16)


def _pack_w_in(w_in):
    o = np.cumsum((0, QK_GLA, QK_GLA, V_GLA, V_GLA, GLA_RANK, GLA_RANK, 2 * W_MLSTM, W_MLSTM, W_MLSTM,
                   H_MLSTM, H_MLSTM, H_MLSTM, H_MLSTM, D_MODEL, D_MODEL))
    o = [int(v) for v in o]
    big = jnp.concatenate([w_in[:, :, o[0]:o[4]], w_in[:, :, o[6]:o[9]], w_in[:, :, o[13]:o[15]]],
                          axis=-1).astype(BF16)
    small = jnp.concatenate([w_in[:, :, o[4]:o[6]], w_in[:, :, o[9]:o[13]]], axis=-1)
    small_p = jnp.pad(small, ((0, 0), (0, 0), (0, Z_SMALL - small.shape[-1]))).astype(BF16)
    return big, small_p


def kernel(x_prompt, x_sample, c, state_gla_s, state_mlstm_c, state_mlstm_n, state_mlstm_m, c_ctx,
           w_ada, b_ada, ffn1_w_gate, ffn1_w_up, ffn1_w_down, w_in, w_decay, b_decay, w_conv, b_conv,
           f_bias, gla_norm_g, mlstm_norm_g, w_br_gla, w_br_mlstm, w_out,
           ffn2_w_gate, ffn2_w_up, ffn2_w_down, ln_g, ln_b):
    bp, tp, _ = x_prompt.shape
    bs, ts, _ = x_sample.shape
    assert bs + 1 <= 16 and tp % FIN_ROWS == 0 and ts % FIN_ROWS == 0

    cond16 = jnp.zeros((16, D_MODEL), F32).at[0].set(c_ctx).at[1:1 + bs].set(c)
    ada4 = _ada_call(cond16, w_ada, b_ada)

    ffn1 = _pack_ffn(ffn1_w_gate, ffn1_w_up, ffn1_w_down)
    ffn2 = _pack_ffn(ffn2_w_gate, ffn2_w_up, ffn2_w_down)
    w_big, w_small = _pack_w_in(w_in)
    wdec = jnp.zeros((DEPTH, N_DIR, Z_SMALL, QK_GLA), F32)
    wdec = wdec.at[:, 0, 0:GLA_RANK].set(w_decay[:, 0]).at[:, 1, GLA_RANK:2 * GLA_RANK].set(w_decay[:, 1])
    bdec = b_decay.reshape(DEPTH, N_DIR, 1, QK_GLA)
    bcol = jnp.zeros((DEPTH, 1, Z_SMALL), F32)
    brow = jnp.zeros((DEPTH, 16, 1), F32)
    for d in range(N_DIR):
        bcol = bcol.at[:, 0, COL_F[d]:COL_F[d] + H_MLSTM].set(f_bias[:, d])
        brow = brow.at[:, ROW_F[d]:ROW_F[d] + H_MLSTM, 0].set(f_bias[:, d])
    bconv = b_conv.reshape(DEPTH, 1, 2 * W_MLSTM)
    gn_gla = gla_norm_g.reshape(DEPTH, 1, DV_GLA)
    gn_ml = mlstm_norm_g.reshape(DEPTH, 1, DH_MLSTM)
    wbg, wbm, wo = w_br_gla.astype(BF16), w_br_mlstm.astype(BF16), w_out.astype(BF16)
    lng = ln_g.reshape(DEPTH * 3, 1, D_MODEL)
    lnb = ln_b.reshape(DEPTH * 3, 1, D_MODEL)
    pos = _grid_pos_embed(ts)

    def run_pass(x, n_seq, seq_len, cond_row0, rows_per_cond, pos_embed, states):
        emit = states is None
        carried = None
        kw = dict(cond_row0=cond_row0, rows_per_cond=rows_per_cond)
        for l in range(DEPTH):
            x = _ffn_call(x, pos_embed if l == 0 else None, ada4, *ffn1, lng, lnb, layer=l, sub=0, **kw)
            zb, zs, zst = _proj_call(x, ada4, w_big, w_small, layer=l, **kw)
            gla = _gla_call(zb, zs, wdec, bdec, gn_gla, None if emit else states[0],
                            None if carried is None else carried[0],
                            layer=l, n_seq=n_seq, seq_len=seq_len, emit_state=emit)
            ml = _mlstm_call(zb, zs, zst, w_conv, bconv, bcol, brow, gn_ml, None if emit else states[1:],
                             None if carried is None else carried[1:],
                             layer=l, n_seq=n_seq, seq_len=seq_len, emit_state=emit)
            x = _mix_call(x, gla[0], ml[0], zb, ada4, wbg, wbm, wo, lng, lnb, layer=l, **kw)
            x = _ffn_call(x, None, ada4, *ffn2, lng, lnb, layer=l, sub=2, **kw)
            if emit:
                carried = (gla[1], ml[1], ml[2], ml[3])
        return x, carried

    m_ctx = bp * tp
    y_p, new_states = run_pass(x_prompt.reshape(m_ctx, D_MODEL), bp, tp, 0, m_ctx, None, None)
    y_s, _ = run_pass(x_sample.reshape(bs * ts, D_MODEL), bs, ts, 1, ts, pos,
                      (state_gla_s, state_mlstm_c, state_mlstm_n, state_mlstm_m))
    return (y_p.reshape(bp, tp, D_MODEL), y_s.reshape(bs, ts, D_MODEL)) + tuple(new_states)
```

```python
import functools

import jax
import jax.numpy as jnp
import numpy as np
from jax import lax
from jax.experimental import pallas as pl
from jax.experimental.pallas import tpu as pltpu

F32 = jnp.float32
BF16 = jnp.bfloat16

D_MODEL = 1024
DEPTH = 2
GRID_W = 64
N_DIR = 2
H_GLA = 4
DK_GLA = 128
DV_GLA = 256
GLA_RANK = 16
GLA_TAU = 16.0
H_MLSTM = 4
DH_MLSTM = 256
D_FF = 2816
N_MOD = 9
LN_EPS = 1e-5
NORM_EPS = 1e-6
QK_GLA = H_GLA * DK_GLA
V_GLA = H_GLA * DV_GLA
W_MLSTM = H_MLSTM * DH_MLSTM
ALPHA = (2.0 * DEPTH) ** 0.25

Z_BIG = 9 * 1024
Z_SMALL = 128
COL_I = (32, 40)
COL_F = (36, 44)
ROW_I = (0, 8)
ROW_F = (4, 12)

LANE = 128
VMEM_LIMIT = 56 * 1024 * 1024

FFN_TM = 1024
FFN_SUB = 512
FFN_CHUNKS = ((0, 1536), (1536, 1280))
PROJ_TM = 1024
PROJ_TN = 1024
MIX_TM = 512
GLA_CHUNK = 64
GLA_SUB = 16
MLSTM_CHUNK = 128
FIN_ROWS = 256


def _dot(a, b):
    return jnp.dot(a, b, preferred_element_type=F32)


def _dot_nt(a, b):
    return lax.dot_general(a, b, (((1,), (1,)), ((), ())), preferred_element_type=F32)


def _dot_tn(a, b):
    return lax.dot_general(a, b, (((0,), (0,)), ((), ())), preferred_element_type=F32)


def _split3(x):
    hi = x.astype(BF16)
    r = x - hi.astype(F32)
    mid = r.astype(BF16)
    lo = (r - mid.astype(F32)).astype(BF16)
    return hi, mid, lo


def _split2(x):
    hi = x.astype(BF16)
    return hi, (x - hi.astype(F32)).astype(BF16)


def _dot_acc(x, w):
    xh, xm, _ = _split3(x)
    wh, wl = _split2(w)
    return _dot(xh, wh) + _dot(xh, wl) + _dot(xm, wh)


def _silu(x):
    return x * jax.nn.sigmoid(x)


def _log_sigmoid(x):
    return jnp.minimum(x, 0.0) - jnp.log1p(jnp.exp(-jnp.abs(x)))


def _layer_norm(y, g, b):
    mu = jnp.mean(y, axis=-1, keepdims=True)
    d = y - mu
    var = jnp.mean(d * d, axis=-1, keepdims=True)
    return d * lax.rsqrt(var + LN_EPS) * g + b


def _cparams(n_axes):
    return pltpu.CompilerParams(
        dimension_semantics=("arbitrary",) * n_axes, vmem_limit_bytes=VMEM_LIMIT)


def _ada_kernel(c_ref, w_ref, b_ref, o_ref):
    o_ref[...] = _dot_acc(_silu(c_ref[...]), w_ref[...]) + b_ref[...]


def _ada_call(cond16, w_ada, b_ada):
    return pl.pallas_call(
        _ada_kernel,
        grid=(DEPTH, N_MOD),
        in_specs=[
            pl.BlockSpec((16, D_MODEL), lambda l, j: (0, 0)),
            pl.BlockSpec((None, D_MODEL, D_MODEL), lambda l, j: (l, 0, j)),
            pl.BlockSpec((None, None, 1, D_MODEL), lambda l, j: (l, j, 0, 0)),
        ],
        out_specs=pl.BlockSpec((None, None, 16, D_MODEL), lambda l, j: (l, j, 0, 0)),
        out_shape=jax.ShapeDtypeStruct((DEPTH, N_MOD, 16, D_MODEL), F32),
        compiler_params=_cparams(2),
        name="ada_mod",
    )(cond16, w_ada, b_ada.reshape(DEPTH, N_MOD, 1, D_MODEL))


def _cond_row(i, cond_row0, tiles_per_cond):
    return cond_row0 + lax.div(i, jnp.int32(tiles_per_cond))


def _ffn_kernel(*refs, has_pos, cond_row0, tiles_per_cond):
    if has_pos:
        x_ref, pos_ref, mod_ref, wg_ref, wu_ref, wd_ref, lg_ref, lb_ref, o_ref = refs
    else:
        x_ref, mod_ref, wg_ref, wu_ref, wd_ref, lg_ref, lb_ref, o_ref = refs
        pos_ref = None
    r = _cond_row(pl.program_id(0), cond_row0, tiles_per_cond)
    sh = mod_ref[0, pl.ds(r, 1), :]
    sc = mod_ref[1, pl.ds(r, 1), :]
    gate = mod_ref[2, pl.ds(r, 1), :]
    for s in range(FFN_TM // FFN_SUB):
        rows = pl.ds(s * FFN_SUB, FFN_SUB)
        x = x_ref[rows, :]
        if has_pos:
            x = x + pos_ref[rows, :]
        hm = (x * (1.0 + sc) + sh).astype(BF16)
        acc = None
        for c0, cw in FFN_CHUNKS:
            g = _dot(hm, wg_ref[:, c0:c0 + cw])
            u = _dot(hm, wu_ref[:, c0:c0 + cw])
            part = _dot((_silu(g) * u).astype(BF16), wd_ref[c0:c0 + cw, :])
            acc = part if acc is None else acc + part
        y = ALPHA * x + (0.5 * gate) * acc
        o_ref[rows, :] = _layer_norm(y, lg_ref[...], lb_ref[...])


def _ffn_call(x, pos, ada4, wg, wu, wd, lng, lnb, *, layer, sub, cond_row0, rows_per_cond):
    m = x.shape[0]
    tm = FFN_TM
    has_pos = pos is not None
    kern = functools.partial(_ffn_kernel, has_pos=has_pos, cond_row0=cond_row0,
                             tiles_per_cond=rows_per_cond // tm)
    once = pl.Buffered(1)
    in_specs = [pl.BlockSpec((tm, D_MODEL), lambda i: (i, 0))]
    args = [x]
    if has_pos:
        assert pos.shape[0] == tm
        in_specs.append(pl.BlockSpec((tm, D_MODEL), lambda i: (0, 0), pipeline_mode=once))
        args.append(pos)
    in_specs += [
        pl.BlockSpec((None, 3, 16, D_MODEL), lambda i: (layer, sub, 0, 0)),
        pl.BlockSpec((None, D_MODEL, D_FF), lambda i: (layer, 0, 0), pipeline_mode=once),
        pl.BlockSpec((None, D_MODEL, D_FF), lambda i: (layer, 0, 0), pipeline_mode=once),
        pl.BlockSpec((None, D_FF, D_MODEL), lambda i: (layer, 0, 0), pipeline_mode=once),
        pl.BlockSpec((None, 1, D_MODEL), lambda i: (3 * layer + sub, 0, 0)),
        pl.BlockSpec((None, 1, D_MODEL), lambda i: (3 * layer + sub, 0, 0)),
    ]
    args += [ada4, wg, wu, wd, lng, lnb]
    return pl.pallas_call(
        kern,
        grid=(m // tm,),
        in_specs=in_specs,
        out_specs=pl.BlockSpec((tm, D_MODEL), lambda i: (i, 0)),
        out_shape=jax.ShapeDtypeStruct((m, D_MODEL), F32),
        compiler_params=_cparams(1),
        name="ffn_ln",
    )(*args)


def _proj_kernel(x_ref, mod_ref, wb_ref, ws_ref, zb_ref, zs_ref, zst_ref, hm_s,
                 *, cond_row0, tiles_per_cond):
    i = pl.program_id(0)
    j = pl.program_id(1)

    @pl.when(j == 0)
    def _():
        r = _cond_row(i, cond_row0, tiles_per_cond)
        sh = mod_ref[0, pl.ds(r, 1), :]
        sc = mod_ref[1, pl.ds(r, 1), :]
        hm = (x_ref[...] * (1.0 + sc) + sh).astype(BF16)
        hm_s[...] = hm
        zs = _dot(hm, ws_ref[...])
        zs_ref[...] = zs
        zst_ref[...] = zs.T[COL_I[0]:COL_I[0] + 16, :]

    zb_ref[...] = _dot(hm_s[...], wb_ref[...]).astype(BF16)


def _proj_call(x, ada4, w_big, w_small, *, layer, cond_row0, rows_per_cond):
    m = x.shape[0]
    tm, tn = PROJ_TM, PROJ_TN
    kern = functools.partial(_proj_kernel, cond_row0=cond_row0, tiles_per_cond=rows_per_cond // tm)
    return pl.pallas_call(
        kern,
        grid=(m // tm, Z_BIG // tn),
        in_specs=[
            pl.BlockSpec((tm, D_MODEL), lambda i, j: (i, 0)),
            pl.BlockSpec((None, 3, 16, D_MODEL), lambda i, j: (layer, 1, 0, 0)),
            pl.BlockSpec((None, D_MODEL, tn), lambda i, j: (layer, 0, j)),
            pl.BlockSpec((None, D_MODEL, Z_SMALL), lambda i, j: (layer, 0, 0)),
        ],
        out_specs=[
            pl.BlockSpec((tm, tn), lambda i, j: (i, j)),
            pl.BlockSpec((tm, Z_SMALL), lambda i, j: (i, 0)),
            pl.BlockSpec((16, tm), lambda i, j: (0, i)),
        ],
        out_shape=[
            jax.ShapeDtypeStruct((m, Z_BIG), BF16),
            jax.ShapeDtypeStruct((m, Z_SMALL), F32),
            jax.ShapeDtypeStruct((16, m), F32),
        ],
        scratch_shapes=[pltpu.VMEM((tm, D_MODEL), BF16)],
        compiler_params=_cparams(2),
        name="mixer_in_proj",
    )(x, ada4, w_big, w_small)


def _gla_kernel(*refs, seq_len, has_state, emit_state, n_carried):
    it = iter(refs)
    q_ref, k_ref, v_ref, r_ref, zs_ref, wdec_ref, bdec_ref, gn_ref = (next(it) for _ in range(8))
    s0_ref = next(it) if has_state else None
    if n_carried:
        next(it)
    og_ref = next(it)
    sout_ref = next(it) if emit_state else None
    la_s, st_s, oacc_s, cum_s = (next(it) for _ in range(4))
    t_len = seq_len
    c_len = GLA_CHUNK
    n_chunks = t_len // c_len

    for rb in range(t_len // FIN_ROWS):
        rows = pl.ds(rb * FIN_ROWS, FIN_ROWS)
        zh, zm, _ = _split3(zs_ref[rows, :])
        for d in range(N_DIR):
            wh, wl = _split2(wdec_ref[d])
            x = _dot(zh, wh) + _dot(zh, wl) + _dot(zm, wh) + bdec_ref[d]
            la_s[d, rows, :] = _log_sigmoid(x) * (1.0 / GLA_TAU)

    for d in range(N_DIR):
        for h in range(H_GLA):
            if has_state:
                st_s[d, h] = s0_ref[d, h].T
            else:
                st_s[d, h] = jnp.zeros((DV_GLA, DK_GLA), F32)
    oacc_s[...] = jnp.zeros_like(oacc_s)

    ri = lax.broadcasted_iota(jnp.int32, (c_len, c_len), 0)
    ci = lax.broadcasted_iota(jnp.int32, (c_len, c_len), 1)
    rr = lax.broadcasted_iota(jnp.int32, (c_len, DK_GLA), 0)
    odd_rows = (lax.shift_right_logical(rr, 4) & 1) == 1
    upper_rows = rr >= 2 * GLA_SUB
    same_sub = lax.shift_right_logical(ri, 4) == lax.shift_right_logical(ci, 4)
    same_half = lax.shift_right_logical(ri, 5) == lax.shift_right_logical(ci, 5)
    causal = (ri >= ci, ri <= ci)
    tri = tuple(jnp.where(c, 1.0, 0.0).astype(BF16) for c in causal)
    pm0 = tuple(same_sub & c for c in causal)
    q1rows = (odd_rows, jnp.logical_not(odd_rows))
    q2rows = (upper_rows, jnp.logical_not(upper_rows))

    def bcast(x, n):
        return jnp.broadcast_to(x, (n, DK_GLA))

    for d in range(N_DIR):
        for cb in range(n_chunks):
            rows = pl.ds(cb * c_len, c_len)
            l3 = _split3(la_s[d, rows, :])
            cum_s[d, rows, :] = _dot(tri[d], l3[0]) + _dot(tri[d], l3[1]) + _dot(tri[d], l3[2])

    def scores(d, h, row0):
        rows = pl.ds(row0, c_len)
        ls = slice(DK_GLA * h, DK_GLA * (h + 1))
        vs = slice(DV_GLA * h, DV_GLA * (h + 1))

        def row(i):
            g = i - i % 8
            return cum_s[d, pl.ds(pl.multiple_of(row0 + g, 8), 8), ls][i - g:i - g + 1, :]

        c = cum_s[d, rows, ls]
        zero = jnp.zeros((GLA_SUB, DK_GLA), F32)
        if d == 0:
            ref0 = jnp.concatenate([zero, bcast(row(15), 16), bcast(row(31), 16), bcast(row(47), 16)], axis=0)
            ref1 = jnp.concatenate([bcast(row(15), 32), bcast(row(47), 32)], axis=0)
            ref2, cend = row(31), row(63)
        else:
            ref0 = jnp.concatenate([bcast(row(16), 16), bcast(row(32), 16), bcast(row(48), 16), zero], axis=0)
            ref1 = jnp.concatenate([bcast(row(16), 32), bcast(row(48), 32)], axis=0)
            ref2, cend = row(32), row(0)
        q = q_ref[rows, ls].astype(F32)
        k = k_ref[rows, ls].astype(F32)
        e0 = c - ref0
        p0 = _dot_nt((q * jnp.exp(e0)).astype(BF16), (k * jnp.exp(-e0)).astype(BF16))
        x1 = jnp.exp(-jnp.abs(c - ref1))
        p1 = _dot_nt(jnp.where(q1rows[d], q * x1, 0.0).astype(BF16),
                     jnp.where(q1rows[d], 0.0, k * x1).astype(BF16))
        x2 = jnp.exp(-jnp.abs(c - ref2))
        p2 = _dot_nt(jnp.where(q2rows[d], q * x2, 0.0).astype(BF16),
                     jnp.where(q2rows[d], 0.0, k * x2).astype(BF16))
        qi = (q * jnp.exp(c)).astype(BF16)
        kst = (k * jnp.exp(cend - c)).astype(BF16)
        return dict(d=d, h=h, rows=rows, vs=vs, p0=p0, p1=p1, p2=p2, qi=qi, kst=kst, dec=jnp.exp(cend))

    def body(j, carry):
        row0 = (pl.multiple_of(j * c_len, c_len), pl.multiple_of((n_chunks - 1 - j) * c_len, c_len))
        work = [scores(d, h, row0[d]) for d in range(N_DIR) for h in range(H_GLA)]
        for w in work:
            d, h = w["d"], w["h"]
            p = jnp.where(pm0[d], w["p0"], jnp.where(same_half, w["p1"], w["p2"]))
            o = _dot(p.astype(BF16), v_ref[w["rows"], w["vs"]]) + _dot_nt(w["qi"], st_s[d, h].astype(BF16))
            oacc_s[w["rows"], w["vs"]] += o
        for w in work:
            d, h = w["d"], w["h"]
            st_s[d, h] = st_s[d, h] * w["dec"] + _dot_tn(v_ref[w["rows"], w["vs"]], w["kst"])
        return carry

    lax.fori_loop(0, n_chunks, body, 0)

    g = gn_ref[...]
    for rb in range(t_len // FIN_ROWS):
        rows = pl.ds(rb * FIN_ROWS, FIN_ROWS)
        for h in range(H_GLA):
            vs = slice(DV_GLA * h, DV_GLA * (h + 1))
            o = oacc_s[rows, vs] * (DK_GLA ** -0.5)
            on = o * lax.rsqrt(jnp.mean(o * o, axis=-1, keepdims=True) + NORM_EPS) * g
            og_ref[rows, vs] = (on * _silu(r_ref[rows, vs].astype(F32))).astype(BF16)

    if emit_state:
        for d in range(N_DIR):
            for h in range(H_GLA):
                sout_ref[d, h] = st_s[d, h].T


def _gla_call(zb, zs, wdec, bdec, gnorm, state, carried, *, layer, n_seq, seq_len, emit_state):
    has_state = state is not None
    t = seq_len
    n_carried = 0 if carried is None else 1
    kern = functools.partial(_gla_kernel, seq_len=t, has_state=has_state, emit_state=emit_state,
                             n_carried=n_carried)
    in_specs = [
        pl.BlockSpec((t, QK_GLA), lambda b: (b, 0)),
        pl.BlockSpec((t, QK_GLA), lambda b: (b, 1)),
        pl.BlockSpec((t, V_GLA), lambda b: (b, 1)),
        pl.BlockSpec((t, V_GLA), lambda b: (b, 2)),
        pl.BlockSpec((t, Z_SMALL), lambda b: (b, 0)),
        pl.BlockSpec((None, N_DIR, Z_SMALL, QK_GLA), lambda b: (layer, 0, 0, 0)),
        pl.BlockSpec((None, N_DIR, 1, QK_GLA), lambda b: (layer, 0, 0, 0)),
        pl.BlockSpec((None, 1, DV_GLA), lambda b: (layer, 0, 0)),
    ]
    args = [zb, zb, zb, zb, zs, wdec, bdec, gnorm]
    if has_state:
        in_specs.append(pl.BlockSpec((None, None, N_DIR, H_GLA, DK_GLA, DV_GLA),
                                     lambda b: (b, layer, 0, 0, 0, 0)))
        args.append(state)
    aliases = {}
    if n_carried:
        aliases[len(args)] = 1
        in_specs.append(pl.BlockSpec(memory_space=pl.ANY))
        args.append(carried)
    out_specs = [pl.BlockSpec((t, V_GLA), lambda b: (b, 0))]
    out_shape = [jax.ShapeDtypeStruct((n_seq * t, V_GLA), BF16)]
    if emit_state:
        out_specs.append(pl.BlockSpec((None, None, N_DIR, H_GLA, DK_GLA, DV_GLA),
                                      lambda b: (b, layer, 0, 0, 0, 0)))
        out_shape.append(jax.ShapeDtypeStruct((n_seq, DEPTH, N_DIR, H_GLA, DK_GLA, DV_GLA), F32))
    return pl.pallas_call(
        kern,
        grid=(n_seq,),
        in_specs=in_specs,
        out_specs=out_specs,
        out_shape=out_shape,
        input_output_aliases=aliases,
        scratch_shapes=[
            pltpu.VMEM((N_DIR, t, QK_GLA), F32),
            pltpu.VMEM((N_DIR, H_GLA, DV_GLA, DK_GLA), F32),
            pltpu.VMEM((t, V_GLA), F32),
            pltpu.VMEM((N_DIR, t, QK_GLA), F32),
        ],
        compiler_params=_cparams(1),
        name="gla_scan",
    )(*args)


def _mlstm_kernel(*refs, seq_len, has_state, emit_state, n_carried):
    it = iter(refs)
    (q_ref, k_ref, v_ref, om_ref, zs_ref, zst_ref, wconv_ref, bconv_ref,
     bcol_ref, brow_ref, gn_ref) = (next(it) for _ in range(11))
    if has_state:
        c0_ref, n0_ref, m0_ref = (next(it) for _ in range(3))
    for _ in range(n_carried):
        next(it)
    hm_ref = next(it)
    if emit_state:
        cout_ref, nout_ref, mout_ref = (next(it) for _ in range(3))
    (qc_s, kc_s, xp_s, lf_s, lft_s, c_s, n_s, m_s, hacc_s, cumc_s, cumr_s) = (next(it) for _ in range(11))
    t_len = seq_len
    c_len = MLSTM_CHUNK
    n_chunks = t_len // c_len
    dh = DH_MLSTM

    xp_s[pl.ds(0, 8), :] = jnp.zeros((8, dh), F32)
    xp_s[pl.ds(8 + t_len, 8), :] = jnp.zeros((8, dh), F32)
    for which, src in enumerate((q_ref, k_ref)):
        for h in range(H_MLSTM):
            hs = slice(dh * h, dh * (h + 1))
            ws = slice(which * W_MLSTM + dh * h, which * W_MLSTM + dh * (h + 1))
            xp_s[pl.ds(8, t_len), :] = src[:, hs].astype(F32)
            w0 = wconv_ref[pl.ds(0, 1), ws]
            w1 = wconv_ref[pl.ds(1, 1), ws]
            w2 = wconv_ref[pl.ds(2, 1), ws]
            bias = bconv_ref[:, ws]
            for rb in range(t_len // FIN_ROWS):
                r0 = rb * FIN_ROWS
                y = (w0 * xp_s[pl.ds(r0 + 7, FIN_ROWS), :] + w1 * xp_s[pl.ds(r0 + 8, FIN_ROWS), :]
                     + w2 * xp_s[pl.ds(r0 + 9, FIN_ROWS), :] + bias)
                a = _silu(y)
                if which == 0:
                    qc_s[pl.ds(r0, FIN_ROWS), hs] = (a * (dh ** -0.5)).astype(BF16)
                else:
                    kc_s[pl.ds(r0, FIN_ROWS), hs] = a

    lf_s[...] = _log_sigmoid(zs_ref[...] + bcol_ref[...])
    zt = zst_ref[...]
    trow = lax.broadcasted_iota(jnp.int32, zt.shape, 0)
    is_f = (lax.shift_right_logical(trow, 2) & 1) == 1
    lft_s[...] = jnp.where(is_f, _log_sigmoid(zt + brow_ref[...]), zt)

    for d in range(N_DIR):
        for h in range(H_MLSTM):
            if has_state:
                c_s[d, h] = c0_ref[d, h]
                n_s[d, h] = n0_ref[d, pl.ds(h, 1), :]
                m_s[d, h] = jnp.broadcast_to(m0_ref[pl.ds(d, 1), pl.ds(h, 1)], (1, LANE))
            else:
                c_s[d, h] = jnp.zeros((dh, dh), F32)
                n_s[d, h] = jnp.zeros((1, dh), F32)
                m_s[d, h] = jnp.zeros((1, LANE), F32)
    hacc_s[...] = jnp.zeros_like(hacc_s)

    ri = lax.broadcasted_iota(jnp.int32, (c_len, c_len), 0)
    ci = lax.broadcasted_iota(jnp.int32, (c_len, c_len), 1)
    causal = (ri >= ci, ri <= ci)
    tri_c = tuple(jnp.where(c, 1.0, 0.0).astype(BF16) for c in causal)
    tri_r = (tri_c[1], tri_c[0])

    for d in range(N_DIR):
        for cb in range(n_chunks):
            rows = pl.ds(cb * c_len, c_len)
            l3 = _split3(lf_s[rows, :])
            cumc_s[d, rows, :] = _dot(tri_c[d], l3[0]) + _dot(tri_c[d], l3[1]) + _dot(tri_c[d], l3[2])
            r3 = _split3(lft_s[:, rows])
            cumr_s[d, :, rows] = _dot(r3[0], tri_r[d]) + _dot(r3[1], tri_r[d]) + _dot(r3[2], tri_r[d])

    ones_c = jnp.ones((c_len, LANE), BF16)

    def lanes2(x):
        return jnp.concatenate([x, x], axis=1)

    def matmuls(d, h, row0):
        rows = pl.ds(row0, c_len)
        hs = slice(dh * h, dh * (h + 1))
        q = qc_s[rows, hs]
        s_raw = _dot_nt(q, kc_s[rows, hs].astype(BF16))
        qc = _dot(q, c_s[d, h].astype(BF16))
        n_rep = jnp.broadcast_to(n_s[d, h], (LANE, dh)).astype(BF16)
        qn = _dot_nt(q, n_rep)
        return dict(d=d, h=h, rows=rows, row0=row0, hs=hs, s_raw=s_raw, qc=qc, qn=qn)

    def gates(w):
        d, h, rows, row0 = w["d"], w["h"], w["rows"], w["row0"]
        cf = COL_F[d] + h
        end = c_len - 1 if d == 0 else 0
        cum_col = cumc_s[d, rows, pl.ds(cf, 1)]
        cum_c = jnp.broadcast_to(cum_col, (c_len, LANE))
        b_c = jnp.broadcast_to(zs_ref[rows, pl.ds(COL_I[d] + h, 1)] - cum_col, (c_len, LANE))
        g_end = end - end % 8
        cum_end = cumc_s[d, pl.ds(pl.multiple_of(row0 + g_end, 8), 8), pl.ds(cf, 1)][end - g_end:end - g_end + 1, :]
        b_r = lft_s[pl.ds(ROW_I[d] + h, 1), rows] - cumr_s[d, pl.ds(ROW_F[d] + h, 1), rows]
        m_prev = m_s[d, h]
        b_m = jnp.where(causal[d], b_r, -jnp.inf)
        m_rel = jnp.maximum(m_prev, jnp.broadcast_to(jnp.max(b_m, axis=1, keepdims=True), (c_len, LANE)))
        w["dmat"] = jnp.exp(b_m - m_rel)
        w["inter"] = jnp.exp(m_prev - m_rel)
        w["floor"] = jnp.exp(-(cum_c + m_rel))
        log_w = cum_end + b_c
        m_new = jnp.maximum(cum_end + m_prev, jnp.max(log_w, axis=0, keepdims=True))
        w["wgt"] = jnp.exp(log_w - m_new)
        w["decay"] = jnp.exp(cum_end + m_prev - m_new)
        w["m_new"] = m_new

    def body(j, carry):
        row0 = (pl.multiple_of(j * c_len, c_len), pl.multiple_of((n_chunks - 1 - j) * c_len, c_len))
        work = [matmuls(d, h, row0[d]) for d in range(N_DIR) for h in range(H_MLSTM)]
        for w in work:
            gates(w)
        for w in work:
            s = w["s_raw"] * w["dmat"]
            s_hi = s.astype(BF16)
            s_lo = (s - s_hi.astype(F32)).astype(BF16)
            w["sv"] = _dot(s_hi, v_ref[w["rows"], w["hs"]])
            w["rsum"] = _dot(s_hi, ones_c) + _dot(s_lo, ones_c)
        for w in work:
            den = w["rsum"] + w["inter"] * w["qn"]
            rn = 1.0 / jnp.maximum(jnp.abs(den), w["floor"])
            hacc_s[w["rows"], w["hs"]] += w["sv"] * lanes2(rn) + w["qc"] * lanes2(w["inter"] * rn)
        for w in work:
            d, h = w["d"], w["h"]
            kw = kc_s[w["rows"], w["hs"]] * lanes2(w["wgt"])
            decay = w["decay"]
            c_s[d, h] = decay[:, :1] * c_s[d, h] + _dot_tn(kw.astype(BF16), v_ref[w["rows"], w["hs"]])
            n_s[d, h] = lanes2(decay) * n_s[d, h] + jnp.sum(kw, axis=0, keepdims=True)
            m_s[d, h] = w["m_new"]
        return carry

    lax.fori_loop(0, n_chunks, body, 0)

    g = gn_ref[...]
    for rb in range(t_len // FIN_ROWS):
        rows = pl.ds(rb * FIN_ROWS, FIN_ROWS)
        for h in range(H_MLSTM):
            hs = slice(dh * h, dh * (h + 1))
            o = hacc_s[rows, hs]
            on = o * lax.rsqrt(jnp.mean(o * o, axis=-1, keepdims=True) + NORM_EPS) * g
            hm_ref[rows, hs] = (jax.nn.sigmoid(om_ref[rows, hs].astype(F32)) * on).astype(BF16)

    if emit_state:
        for d in range(N_DIR):
            for h in range(H_MLSTM):
                cout_ref[d, h] = c_s[d, h]
                nout_ref[d, pl.ds(h, 1), :] = n_s[d, h]
                mout_ref[pl.ds(d, 1), pl.ds(h, 1)] = m_s[d, h][:, :1]


def _mlstm_call(zb, zs, zst, wconv, bconv, bcol, brow, gnorm, states, carried, *, layer, n_seq, seq_len,
                emit_state):
    has_state = states is not None
    t = seq_len
    dh = DH_MLSTM
    n_carried = 0 if carried is None else 3
    kern = functools.partial(_mlstm_kernel, seq_len=t, has_state=has_state, emit_state=emit_state,
                             n_carried=n_carried)
    in_specs = [
        pl.BlockSpec((t, W_MLSTM), lambda b: (b, 3)),
        pl.BlockSpec((t, W_MLSTM), lambda b: (b, 4)),
        pl.BlockSpec((t, W_MLSTM), lambda b: (b, 5)),
        pl.BlockSpec((t, W_MLSTM), lambda b: (b, 6)),
        pl.BlockSpec((t, Z_SMALL), lambda b: (b, 0)),
        pl.BlockSpec((16, t), lambda b: (0, b)),
        pl.BlockSpec((None, 3, 2 * W_MLSTM), lambda b: (layer, 0, 0)),
        pl.BlockSpec((None, 1, 2 * W_MLSTM), lambda b: (layer, 0, 0)),
        pl.BlockSpec((None, 1, Z_SMALL), lambda b: (layer, 0, 0)),
        pl.BlockSpec((None, 16, 1), lambda b: (layer, 0, 0)),
        pl.BlockSpec((None, 1, dh), lambda b: (layer, 0, 0)),
    ]
    args = [zb, zb, zb, zb, zs, zst, wconv, bconv, bcol, brow, gnorm]
    if has_state:
        c0, n0, m0 = states
        in_specs += [
            pl.BlockSpec((None, None, N_DIR, H_MLSTM, dh, dh), lambda b: (b, layer, 0, 0, 0, 0)),
            pl.BlockSpec((None, None, N_DIR, H_MLSTM, dh), lambda b: (b, layer, 0, 0, 0)),
            pl.BlockSpec((None, None, N_DIR, H_MLSTM), lambda b: (b, layer, 0, 0)),
        ]
        args += [c0, n0, m0]
    aliases = {}
    for k in range(n_carried):
        aliases[len(args)] = 1 + k
        in_specs.append(pl.BlockSpec(memory_space=pl.ANY))
        args.append(carried[k])
    out_specs = [pl.BlockSpec((t, W_MLSTM), lambda b: (b, 0))]
    out_shape = [jax.ShapeDtypeStruct((n_seq * t, W_MLSTM), BF16)]
    if emit_state:
        out_specs += [
            pl.BlockSpec((None, None, N_DIR, H_MLSTM, dh, dh), lambda b: (b, layer, 0, 0, 0, 0)),
            pl.BlockSpec((None, None, N_DIR, H_MLSTM, dh), lambda b: (b, layer, 0, 0, 0)),
            pl.BlockSpec((None, None, N_DIR, H_MLSTM), lambda b: (b, layer, 0, 0)),
        ]
        out_shape += [
            jax.ShapeDtypeStruct((n_seq, DEPTH, N_DIR, H_MLSTM, dh, dh), F32),
            jax.ShapeDtypeStruct((n_seq, DEPTH, N_DIR, H_MLSTM, dh), F32),
            jax.ShapeDtypeStruct((n_seq, DEPTH, N_DIR, H_MLSTM), F32),
        ]
    return pl.pallas_call(
        kern,
        grid=(n_seq,),
        in_specs=in_specs,
        out_specs=out_specs,
        out_shape=out_shape,
        input_output_aliases=aliases,
        scratch_shapes=[
            pltpu.VMEM((t, W_MLSTM), BF16),
            pltpu.VMEM((t, W_MLSTM), F32),
            pltpu.VMEM((t + 16, dh), F32),
            pltpu.VMEM((t, Z_SMALL), F32),
            pltpu.VMEM((16, t), F32),
            pltpu.VMEM((N_DIR, H_MLSTM, dh, dh), F32),
            pltpu.VMEM((N_DIR, H_MLSTM, 1, dh), F32),
            pltpu.VMEM((N_DIR, H_MLSTM, 1, LANE), F32),
            pltpu.VMEM((t, W_MLSTM), F32),
            pltpu.VMEM((N_DIR, t, Z_SMALL), F32),
            pltpu.VMEM((N_DIR, 16, t), F32),
        ],
        compiler_params=_cparams(1),
        name="mlstm_scan",
    )(*args)


def _mix_kernel(x_ref, og_ref, hm_ref, gg_ref, gm_ref, mod_ref, wbg_ref, wbm_ref, wo_ref,
                lg_ref, lb_ref, o_ref, *, cond_row0, tiles_per_cond):
    r = _cond_row(pl.program_id(0), cond_row0, tiles_per_cond)
    y_g = _dot(og_ref[...], wbg_ref[...])
    y_m = _dot(hm_ref[...], wbm_ref[...])
    mix = (jax.nn.sigmoid(gg_ref[...].astype(F32)) * y_g
           + jax.nn.sigmoid(gm_ref[...].astype(F32)) * y_m)
    y = _dot(mix.astype(BF16), wo_ref[...])
    gate = mod_ref[2, pl.ds(r, 1), :]
    o_ref[...] = _layer_norm(ALPHA * x_ref[...] + gate * y, lg_ref[...], lb_ref[...])


def _mix_call(x, og, hm, zb, ada4, wbg, wbm, wo, lng, lnb, *, layer, cond_row0, rows_per_cond):
    m = x.shape[0]
    tm = MIX_TM
    kern = functools.partial(_mix_kernel, cond_row0=cond_row0, tiles_per_cond=rows_per_cond // tm)
    wspec = pl.BlockSpec((None, D_MODEL, D_MODEL), lambda i: (layer, 0, 0))
    return pl.pallas_call(
        kern,
        grid=(m // tm,),
        in_specs=[
            pl.BlockSpec((tm, D_MODEL), lambda i: (i, 0)),
            pl.BlockSpec((tm, V_GLA), lambda i: (i, 0)),
            pl.BlockSpec((tm, W_MLSTM), lambda i: (i, 0)),
            pl.BlockSpec((tm, D_MODEL), lambda i: (i, 7)),
            pl.BlockSpec((tm, D_MODEL), lambda i: (i, 8)),
            pl.BlockSpec((None, 3, 16, D_MODEL), lambda i: (layer, 1, 0, 0)),
            wspec, wspec, wspec,
            pl.BlockSpec((None, 1, D_MODEL), lambda i: (3 * layer + 1, 0, 0)),
            pl.BlockSpec((None, 1, D_MODEL), lambda i: (3 * layer + 1, 0, 0)),
        ],
        out_specs=pl.BlockSpec((tm, D_MODEL), lambda i: (i, 0)),
        out_shape=jax.ShapeDtypeStruct((m, D_MODEL), F32),
        compiler_params=_cparams(1),
        name="mixer_out_ln",
    )(x, og, hm, zb, zb, ada4, wbg, wbm, wo, lng, lnb)


def _grid_pos_embed(t_len):
    rows = t_len // GRID_W
    r = jnp.repeat(jnp.arange(rows), GRID_W).astype(F32)
    col = jnp.tile(jnp.arange(GRID_W), rows).astype(F32)
    nf = D_MODEL // 4
    omega = 1.0 / (10000.0 ** (jnp.arange(nf, dtype=F32) / nf))
    er = r[:, None] * omega
    ec = col[:, None] * omega
    return jnp.concatenate([jnp.sin(er), jnp.cos(er), jnp.sin(ec), jnp.cos(ec)], axis=-1)


def _pack_ffn(w_gate, w_up, w_down):
    return w_gate.astype(BF16), w_up.astype(BF16), w_down.astype(BF16)


def _pack_w_in(w_in):
    o = np.cumsum((0, QK_GLA, QK_GLA, V_GLA, V_GLA, GLA_RANK, GLA_RANK, 2 * W_MLSTM, W_MLSTM, W_MLSTM,
                   H_MLSTM, H_MLSTM, H_MLSTM, H_MLSTM, D_MODEL, D_MODEL))
    o = [int(v) for v in o]
    big = jnp.concatenate([w_in[:, :, o[0]:o[4]], w_in[:, :, o[6]:o[9]], w_in[:, :, o[13]:o[15]]],
                          axis=-1).astype(BF16)
    small = jnp.concatenate([w_in[:, :, o[4]:o[6]], w_in[:, :, o[9]:o[13]]], axis=-1)
    small_p = jnp.pad(small, ((0, 0), (0, 0), (0, Z_SMALL - small.shape[-1]))).astype(BF16)
    return big, small_p


def kernel(x_prompt, x_sample, c, state_gla_s, state_mlstm_c, state_mlstm_n, state_mlstm_m, c_ctx,
           w_ada, b_ada, ffn1_w_gate, ffn1_w_up, ffn1_w_down, w_in, w_decay, b_decay, w_conv, b_conv,
           f_bias, gla_norm_g, mlstm_norm_g, w_br_gla, w_br_mlstm, w_out,
           ffn2_w_gate, ffn2_w_up, ffn2_w_down, ln_g, ln_b):
    bp, tp, _ = x_prompt.shape
    bs, ts, _ = x_sample.shape
    assert bs + 1 <= 16 and tp % FIN_ROWS == 0 and ts % FIN_ROWS == 0

    cond16 = jnp.zeros((16, D_MODEL), F32).at[0].set(c_ctx).at[1:1 + bs].set(c)
    ada4 = _ada_call(cond16, w_ada, b_ada)

    ffn1 = _pack_ffn(ffn1_w_gate, ffn1_w_up, ffn1_w_down)
    ffn2 = _pack_ffn(ffn2_w_gate, ffn2_w_up, ffn2_w_down)
    w_big, w_small = _pack_w_in(w_in)
    wdec = jnp.zeros((DEPTH, N_DIR, Z_SMALL, QK_GLA), F32)
    wdec = wdec.at[:, 0, 0:GLA_RANK].set(w_decay[:, 0]).at[:, 1, GLA_RANK:2 * GLA_RANK].set(w_decay[:, 1])
    bdec = b_decay.reshape(DEPTH, N_DIR, 1, QK_GLA)
    bcol = jnp.zeros((DEPTH, 1, Z_SMALL), F32)
    brow = jnp.zeros((DEPTH, 16, 1), F32)
    for d in range(N_DIR):
        bcol = bcol.at[:, 0, COL_F[d]:COL_F[d] + H_MLSTM].set(f_bias[:, d])
        brow = brow.at[:, ROW_F[d]:ROW_F[d] + H_MLSTM, 0].set(f_bias[:, d])
    bconv = b_conv.reshape(DEPTH, 1, 2 * W_MLSTM)
    gn_gla = gla_norm_g.reshape(DEPTH, 1, DV_GLA)
    gn_ml = mlstm_norm_g.reshape(DEPTH, 1, DH_MLSTM)
    wbg, wbm, wo = w_br_gla.astype(BF16), w_br_mlstm.astype(BF16), w_out.astype(BF16)
    lng = ln_g.reshape(DEPTH * 3, 1, D_MODEL)
    lnb = ln_b.reshape(DEPTH * 3, 1, D_MODEL)
    pos = _grid_pos_embed(ts)

    def run_pass(x, n_seq, seq_len, cond_row0, rows_per_cond, pos_embed, states):
        emit = states is None
        carried = None
        kw = dict(cond_row0=cond_row0, rows_per_cond=rows_per_cond)
        for l in range(DEPTH):
            x = _ffn_call(x, pos_embed if l == 0 else None, ada4, *ffn1, lng, lnb, layer=l, sub=0, **kw)
            zb, zs, zst = _proj_call(x, ada4, w_big, w_small, layer=l, **kw)
            gla = _gla_call(zb, zs, wdec, bdec, gn_gla, None if emit else states[0],
                            None if carried is None else carried[0],
                            layer=l, n_seq=n_seq, seq_len=seq_len, emit_state=emit)
            ml = _mlstm_call(zb, zs, zst, w_conv, bconv, bcol, brow, gn_ml, None if emit else states[1:],
                             None if carried is None else carried[1:],
                             layer=l, n_seq=n_seq, seq_len=seq_len, emit_state=emit)
            x = _mix_call(x, gla[0], ml[0], zb, ada4, wbg, wbm, wo, lng, lnb, layer=l, **kw)
            x = _ffn_call(x, None, ada4, *ffn2, lng, lnb, layer=l, sub=2, **kw)
            if emit:
                carried = (gla[1], ml[1], ml[2], ml[3])
        return x, carried

    m_ctx = bp * tp
    y_p, new_states = run_pass(x_prompt.reshape(m_ctx, D_MODEL), bp, tp, 0, m_ctx, None, None)
    y_s, _ = run_pass(x_sample.reshape(bs * ts, D_MODEL), bs, ts, 1, ts, pos,
                      (state_gla_s, state_mlstm_c, state_mlstm_n, state_mlstm_m))
    return (y_p.reshape(bp, tp, D_MODEL), y_s.reshape(bs, ts, D_MODEL)) + tuple(new_states)
```

```python
import functools

import jax
import jax.numpy as jnp
import numpy as np
from jax import lax
from jax.experimental import pallas as pl
from jax.experimental.pallas import tpu as pltpu

F32 = jnp.float32
BF16 = jnp.bfloat16

D_MODEL = 1024
DEPTH = 2
GRID_W = 64
N_DIR = 2
H_GLA = 4
DK_GLA = 128
DV_GLA = 256
GLA_RANK = 16
GLA_TAU = 16.0
H_MLSTM = 4
DH_MLSTM = 256
D_FF = 2816
N_MOD = 9
LN_EPS = 1e-5
NORM_EPS = 1e-6
QK_GLA = H_GLA * DK_GLA
V_GLA = H_GLA * DV_GLA
W_MLSTM = H_MLSTM * DH_MLSTM
ALPHA = (2.0 * DEPTH) ** 0.25

Z_BIG = 9 * 1024
Z_SMALL = 128
SLAB_V_GLA, SLAB_R_GLA, SLAB_Q_MLSTM, SLAB_K_MLSTM, SLAB_V_MLSTM = 1, 2, 3, 4, 5
SLAB_O_MLSTM, SLAB_G_GLA, SLAB_G_MLSTM = 6, 7, 8
COL_I = (32, 40)
COL_F = (36, 44)
ROW_I = (0, 8)
ROW_F = (4, 12)

LANE = 128
VMEM_LIMIT = 56 * 1024 * 1024

FFN_TM = 1024
FFN_SUB = 512
FFN_CHUNKS = ((0, 1536), (1536, 1280))
PROJ_TM = 512
PROJ_TN = 1024
MIX_TM = 512
MIX_SUB = 512
GLA_CHUNK = 64
GLA_SUB = 16
MLSTM_CHUNK = 128
FIN_ROWS = 256


def _dot(a, b):
    return jnp.dot(a, b, preferred_element_type=F32)


def _dot_nt(a, b):
    return lax.dot_general(a, b, (((1,), (1,)), ((), ())), preferred_element_type=F32)


def _dot_tn(a, b):
    return lax.dot_general(a, b, (((0,), (0,)), ((), ())), preferred_element_type=F32)


def _split3(x):
    hi = x.astype(BF16)
    r = x - hi.astype(F32)
    mid = r.astype(BF16)
    lo = (r - mid.astype(F32)).astype(BF16)
    return hi, mid, lo


def _split2(x):
    hi = x.astype(BF16)
    return hi, (x - hi.astype(F32)).astype(BF16)


def _dot_acc(x, w):
    xh, xm, _ = _split3(x)
    wh, wl = _split2(w)
    return _dot(xh, wh) + _dot(xh, wl) + _dot(xm, wh)


def _silu(x):
    return x * jax.nn.sigmoid(x)


def _log_sigmoid(x):
    return jnp.minimum(x, 0.0) - jnp.log(1.0 + jnp.exp(-jnp.abs(x)))


def _layer_norm(y, g, b):
    mu = jnp.mean(y, axis=-1, keepdims=True)
    d = y - mu
    var = jnp.mean(d * d, axis=-1, keepdims=True)
    return d * lax.rsqrt(var + LN_EPS) * g + b


def _cparams(n_axes):
    return pltpu.CompilerParams(
        dimension_semantics=("arbitrary",) * n_axes, vmem_limit_bytes=VMEM_LIMIT)


def _ada_kernel(c_ref, w_ref, b_ref, o_ref):
    o_ref[...] = _dot_acc(_silu(c_ref[...]), w_ref[...]) + b_ref[...]


def _ada_call(cond16, w_ada, b_ada):
    return pl.pallas_call(
        _ada_kernel,
        grid=(DEPTH, N_MOD),
        in_specs=[
            pl.BlockSpec((16, D_MODEL), lambda l, j: (0, 0)),
            pl.BlockSpec((None, D_MODEL, D_MODEL), lambda l, j: (l, 0, j)),
            pl.BlockSpec((None, None, 1, D_MODEL), lambda l, j: (l, j, 0, 0)),
        ],
        out_specs=pl.BlockSpec((None, None, 16, D_MODEL), lambda l, j: (l, j, 0, 0)),
        out_shape=jax.ShapeDtypeStruct((DEPTH, N_MOD, 16, D_MODEL), F32),
        compiler_params=_cparams(2),
        name="ada_mod",
    )(cond16, w_ada, b_ada.reshape(DEPTH, N_MOD, 1, D_MODEL))


def _cond_row(i, cond_row0, tiles_per_cond):
    return cond_row0 + lax.div(i, jnp.int32(tiles_per_cond))


def _ffn_kernel(*refs, has_pos, cond_row0, tiles_per_cond):
    if has_pos:
        x_ref, pos_ref, mod_ref, wg_ref, wu_ref, wd_ref, lg_ref, lb_ref, o_ref = refs
    else:
        x_ref, mod_ref, wg_ref, wu_ref, wd_ref, lg_ref, lb_ref, o_ref = refs
        pos_ref = None
    r = _cond_row(pl.program_id(0), cond_row0, tiles_per_cond)
    sh = mod_ref[0, pl.ds(r, 1), :]
    sc = mod_ref[1, pl.ds(r, 1), :]
    gate = mod_ref[2, pl.ds(r, 1), :]
    for s in range(FFN_TM // FFN_SUB):
        rows = pl.ds(s * FFN_SUB, FFN_SUB)
        x = x_ref[rows, :]
        if has_pos:
            x = x + pos_ref[rows, :]
        hm = (x * (1.0 + sc) + sh).astype(BF16)
        acc = None
        for c0, cw in FFN_CHUNKS:
            g = _dot(hm, wg_ref[:, c0:c0 + cw])
            u = _dot(hm, wu_ref[:, c0:c0 + cw])
            part = _dot((_silu(g) * u).astype(BF16), wd_ref[c0:c0 + cw, :])
            acc = part if acc is None else acc + part
        y = ALPHA * x + (0.5 * gate) * acc
        o_ref[rows, :] = _layer_norm(y, lg_ref[...], lb_ref[...])


def _ffn_call(x, pos, ada4, wg, wu, wd, lng, lnb, *, layer, sub, cond_row0, rows_per_cond):
    m = x.shape[0]
    tm = FFN_TM
    has_pos = pos is not None
    kern = functools.partial(_ffn_kernel, has_pos=has_pos, cond_row0=cond_row0,
                             tiles_per_cond=rows_per_cond // tm)
    once = pl.Buffered(1)
    in_specs = [pl.BlockSpec((tm, D_MODEL), lambda i: (i, 0))]
    args = [x]
    if has_pos:
        assert pos.shape[0] == tm
        in_specs.append(pl.BlockSpec((tm, D_MODEL), lambda i: (0, 0), pipeline_mode=once))
        args.append(pos)
    in_specs += [
        pl.BlockSpec((None, 3, 16, D_MODEL), lambda i: (layer, sub, 0, 0)),
        pl.BlockSpec((None, D_MODEL, D_FF), lambda i: (layer, 0, 0), pipeline_mode=once),
        pl.BlockSpec((None, D_MODEL, D_FF), lambda i: (layer, 0, 0), pipeline_mode=once),
        pl.BlockSpec((None, D_FF, D_MODEL), lambda i: (layer, 0, 0), pipeline_mode=once),
        pl.BlockSpec((None, 1, D_MODEL), lambda i: (3 * layer + sub, 0, 0)),
        pl.BlockSpec((None, 1, D_MODEL), lambda i: (3 * layer + sub, 0, 0)),
    ]
    args += [ada4, wg, wu, wd, lng, lnb]
    return pl.pallas_call(
        kern,
        grid=(m // tm,),
        in_specs=in_specs,
        out_specs=pl.BlockSpec((tm, D_MODEL), lambda i: (i, 0)),
        out_shape=jax.ShapeDtypeStruct((m, D_MODEL), F32),
        compiler_params=_cparams(1),
        name="ffn_ln",
    )(*args)


def _proj_kernel(x_ref, mod_ref, wb_ref, ws_ref, zb_ref, zs_ref, zst_ref, *, cond_row0, tiles_per_cond):
    r = _cond_row(pl.program_id(0), cond_row0, tiles_per_cond)
    sh = mod_ref[0, pl.ds(r, 1), :]
    sc = mod_ref[1, pl.ds(r, 1), :]
    hm = (x_ref[...] * (1.0 + sc) + sh).astype(BF16)
    zs = _dot(hm, ws_ref[...])
    zs_ref[...] = zs
    zst_ref[...] = zs.T[COL_I[0]:COL_I[0] + 16, :]
    for n in range(Z_BIG // PROJ_TN):
        cols = slice(n * PROJ_TN, (n + 1) * PROJ_TN)
        z = _dot(hm, wb_ref[:, cols])
        if n == SLAB_R_GLA:
            z = _silu(z)
        elif n in (SLAB_O_MLSTM, SLAB_G_GLA, SLAB_G_MLSTM):
            z = jax.nn.sigmoid(z)
        zb_ref[:, cols] = z.astype(BF16)


def _proj_call(x, ada4, w_big, w_small, *, layer, cond_row0, rows_per_cond):
    m = x.shape[0]
    tm = PROJ_TM
    kern = functools.partial(_proj_kernel, cond_row0=cond_row0, tiles_per_cond=rows_per_cond // tm)
    once = pl.Buffered(1)
    return pl.pallas_call(
        kern,
        grid=(m // tm,),
        in_specs=[
            pl.BlockSpec((tm, D_MODEL), lambda i: (i, 0)),
            pl.BlockSpec((None, 3, 16, D_MODEL), lambda i: (layer, 1, 0, 0)),
            pl.BlockSpec((None, D_MODEL, Z_BIG), lambda i: (layer, 0, 0), pipeline_mode=once),
            pl.BlockSpec((None, D_MODEL, Z_SMALL), lambda i: (layer, 0, 0), pipeline_mode=once),
        ],
        out_specs=[
            pl.BlockSpec((tm, Z_BIG), lambda i: (i, 0)),
            pl.BlockSpec((tm, Z_SMALL), lambda i: (i, 0)),
            pl.BlockSpec((16, tm), lambda i: (0, i)),
        ],
        out_shape=[
            jax.ShapeDtypeStruct((m, Z_BIG), BF16),
            jax.ShapeDtypeStruct((m, Z_SMALL), F32),
            jax.ShapeDtypeStruct((16, m), F32),
        ],
        compiler_params=_cparams(1),
        name="mixer_in_proj",
    )(x, ada4, w_big, w_small)


def _gla_kernel(*refs, seq_len, has_state, emit_state, n_carried):
    it = iter(refs)
    q_ref, k_ref, v_ref, zs_ref, wdec_ref, bdec_ref = (next(it) for _ in range(6))
    s0_ref = next(it) if has_state else None
    if n_carried:
        next(it)
    oacc_s = next(it)
    sout_ref = next(it) if emit_state else None
    la_s, st_s, cum_s = (next(it) for _ in range(3))
    t_len = seq_len
    c_len = GLA_CHUNK
    n_chunks = t_len // c_len

    for rb in range(t_len // FIN_ROWS):
        rows = pl.ds(rb * FIN_ROWS, FIN_ROWS)
        zh, zm, _ = _split3(zs_ref[rows, :])
        for d in range(N_DIR):
            wh, wl = _split2(wdec_ref[d])
            x = _dot(zh, wh) + _dot(zh, wl) + _dot(zm, wh) + bdec_ref[d]
            la_s[d, rows, :] = _log_sigmoid(x) * (1.0 / GLA_TAU)

    for d in range(N_DIR):
        for h in range(H_GLA):
            if has_state:
                st_s[d, h] = s0_ref[d, h].T
            else:
                st_s[d, h] = jnp.zeros((DV_GLA, DK_GLA), F32)
    oacc_s[...] = jnp.zeros_like(oacc_s)

    ri = lax.broadcasted_iota(jnp.int32, (c_len, c_len), 0)
    ci = lax.broadcasted_iota(jnp.int32, (c_len, c_len), 1)
    rr = lax.broadcasted_iota(jnp.int32, (c_len, DK_GLA), 0)
    odd_rows = (lax.shift_right_logical(rr, 4) & 1) == 1
    upper_rows = rr >= 2 * GLA_SUB
    same_sub = lax.shift_right_logical(ri, 4) == lax.shift_right_logical(ci, 4)
    same_half = lax.shift_right_logical(ri, 5) == lax.shift_right_logical(ci, 5)
    causal = (ri >= ci, ri <= ci)
    tri = tuple(jnp.where(c, 1.0, 0.0).astype(BF16) for c in causal)
    pm0 = tuple(same_sub & c for c in causal)
    q1rows = (odd_rows, jnp.logical_not(odd_rows))
    q2rows = (upper_rows, jnp.logical_not(upper_rows))

    def bcast(x, n):
        return jnp.broadcast_to(x, (n, DK_GLA))

    for d in range(N_DIR):
        for cb in range(n_chunks):
            rows = pl.ds(cb * c_len, c_len)
            l3 = _split3(la_s[d, rows, :])
            cum_s[d, rows, :] = _dot(tri[d], l3[0]) + _dot(tri[d], l3[1]) + _dot(tri[d], l3[2])

    def scores(d, h, row0):
        rows = pl.ds(row0, c_len)
        ls = slice(DK_GLA * h, DK_GLA * (h + 1))
        vs = slice(DV_GLA * h, DV_GLA * (h + 1))

        def row(i):
            g = i - i % 8
            return cum_s[d, pl.ds(pl.multiple_of(row0 + g, 8), 8), ls][i - g:i - g + 1, :]

        c = cum_s[d, rows, ls]
        zero = jnp.zeros((GLA_SUB, DK_GLA), F32)
        if d == 0:
            ref0 = jnp.concatenate([zero, bcast(row(15), 16), bcast(row(31), 16), bcast(row(47), 16)], axis=0)
            ref1 = jnp.concatenate([bcast(row(15), 32), bcast(row(47), 32)], axis=0)
            ref2, cend = row(31), row(63)
        else:
            ref0 = jnp.concatenate([bcast(row(16), 16), bcast(row(32), 16), bcast(row(48), 16), zero], axis=0)
            ref1 = jnp.concatenate([bcast(row(16), 32), bcast(row(48), 32)], axis=0)
            ref2, cend = row(32), row(0)
        q = q_ref[rows, ls].astype(F32)
        k = k_ref[rows, ls].astype(F32)
        e0 = c - ref0
        p0 = _dot_nt((q * jnp.exp(e0)).astype(BF16), (k * jnp.exp(-e0)).astype(BF16))
        x1 = jnp.exp(-jnp.abs(c - ref1))
        p1 = _dot_nt(jnp.where(q1rows[d], q * x1, 0.0).astype(BF16),
                     jnp.where(q1rows[d], 0.0, k * x1).astype(BF16))
        x2 = jnp.exp(-jnp.abs(c - ref2))
        p2 = _dot_nt(jnp.where(q2rows[d], q * x2, 0.0).astype(BF16),
                     jnp.where(q2rows[d], 0.0, k * x2).astype(BF16))
        qi = (q * jnp.exp(c)).astype(BF16)
        kst = (k * jnp.exp(cend - c)).astype(BF16)
        return dict(d=d, h=h, rows=rows, vs=vs, p0=p0, p1=p1, p2=p2, qi=qi, kst=kst, dec=jnp.exp(cend))

    def body(j, carry):
        row0 = (pl.multiple_of(j * c_len, c_len), pl.multiple_of((n_chunks - 1 - j) * c_len, c_len))
        work = [scores(d, h, row0[d]) for d in range(N_DIR) for h in range(H_GLA)]
        for w in work:
            d, h = w["d"], w["h"]
            p = jnp.where(pm0[d], w["p0"], jnp.where(same_half, w["p1"], w["p2"]))
            o = _dot(p.astype(BF16), v_ref[w["rows"], w["vs"]]) + _dot_nt(w["qi"], st_s[d, h].astype(BF16))
            oacc_s[w["rows"], w["vs"]] += o
        for w in work:
            d, h = w["d"], w["h"]
            st_s[d, h] = st_s[d, h] * w["dec"] + _dot_tn(v_ref[w["rows"], w["vs"]], w["kst"])
        return carry

    lax.fori_loop(0, n_chunks, body, 0)

    if emit_state:
        for d in range(N_DIR):
            for h in range(H_GLA):
                sout_ref[d, h] = st_s[d, h].T


def _gla_call(zb, zs, wdec, bdec, state, carried, *, layer, n_seq, seq_len, emit_state):
    has_state = state is not None
    t = seq_len
    n_carried = 0 if carried is None else 1
    kern = functools.partial(_gla_kernel, seq_len=t, has_state=has_state, emit_state=emit_state,
                             n_carried=n_carried)
    in_specs = [
        pl.BlockSpec((t, QK_GLA), lambda b: (b, 0)),
        pl.BlockSpec((t, QK_GLA), lambda b: (b, 1)),
        pl.BlockSpec((t, V_GLA), lambda b: (b, 1)),
        pl.BlockSpec((t, Z_SMALL), lambda b: (b, 0)),
        pl.BlockSpec((None, N_DIR, Z_SMALL, QK_GLA), lambda b: (layer, 0, 0, 0)),
        pl.BlockSpec((None, N_DIR, 1, QK_GLA), lambda b: (layer, 0, 0, 0)),
    ]
    args = [zb, zb, zb, zs, wdec, bdec]
    if has_state:
        in_specs.append(pl.BlockSpec((None, None, N_DIR, H_GLA, DK_GLA, DV_GLA),
                                     lambda b: (b, layer, 0, 0, 0, 0)))
        args.append(state)
    aliases = {}
    if n_carried:
        aliases[len(args)] = 1
        in_specs.append(pl.BlockSpec(memory_space=pl.ANY))
        args.append(carried)
    out_specs = [pl.BlockSpec((t, V_GLA), lambda b: (b, 0))]
    out_shape = [jax.ShapeDtypeStruct((n_seq * t, V_GLA), F32)]
    if emit_state:
        out_specs.append(pl.BlockSpec((None, None, N_DIR, H_GLA, DK_GLA, DV_GLA),
                                      lambda b: (b, layer, 0, 0, 0, 0)))
        out_shape.append(jax.ShapeDtypeStruct((n_seq, DEPTH, N_DIR, H_GLA, DK_GLA, DV_GLA), F32))
    return pl.pallas_call(
        kern,
        grid=(n_seq,),
        in_specs=in_specs,
        out_specs=out_specs,
        out_shape=out_shape,
        input_output_aliases=aliases,
        scratch_shapes=[
            pltpu.VMEM((N_DIR, t, QK_GLA), F32),
            pltpu.VMEM((N_DIR, H_GLA, DV_GLA, DK_GLA), F32),
            pltpu.VMEM((N_DIR, t, QK_GLA), F32),
        ],
        compiler_params=_cparams(1),
        name="gla_scan",
    )(*args)


def _mlstm_kernel(*refs, seq_len, has_state, emit_state, n_carried):
    it = iter(refs)
    (q_ref, k_ref, v_ref, zs_ref, zst_ref, wconv_ref, bconv_ref,
     bcol_ref, brow_ref) = (next(it) for _ in range(9))
    if has_state:
        c0_ref, n0_ref, m0_ref = (next(it) for _ in range(3))
    for _ in range(n_carried):
        next(it)
    hacc_s = next(it)
    if emit_state:
        cout_ref, nout_ref, mout_ref = (next(it) for _ in range(3))
    (qc_s, kc_s, xp_s, lf_s, lft_s, c_s, n_s, m_s, cumc_s, cumr_s) = (next(it) for _ in range(10))
    t_len = seq_len
    c_len = MLSTM_CHUNK
    n_chunks = t_len // c_len
    dh = DH_MLSTM

    xp_s[pl.ds(0, 8), :] = jnp.zeros((8, dh), F32)
    xp_s[pl.ds(8 + t_len, 8), :] = jnp.zeros((8, dh), F32)
    for which, src in enumerate((q_ref, k_ref)):
        for h in range(H_MLSTM):
            hs = slice(dh * h, dh * (h + 1))
            ws = slice(which * W_MLSTM + dh * h, which * W_MLSTM + dh * (h + 1))
            xp_s[pl.ds(8, t_len), :] = src[:, hs].astype(F32)
            w0 = wconv_ref[pl.ds(0, 1), ws]
            w1 = wconv_ref[pl.ds(1, 1), ws]
            w2 = wconv_ref[pl.ds(2, 1), ws]
            bias = bconv_ref[:, ws]
            for rb in range(t_len // FIN_ROWS):
                r0 = rb * FIN_ROWS
                y = (w0 * xp_s[pl.ds(r0 + 7, FIN_ROWS), :] + w1 * xp_s[pl.ds(r0 + 8, FIN_ROWS), :]
                     + w2 * xp_s[pl.ds(r0 + 9, FIN_ROWS), :] + bias)
                a = _silu(y)
                if which == 0:
                    qc_s[pl.ds(r0, FIN_ROWS), hs] = (a * (dh ** -0.5)).astype(BF16)
                else:
                    kc_s[pl.ds(r0, FIN_ROWS), hs] = a

    lf_s[...] = _log_sigmoid(zs_ref[...] + bcol_ref[...])
    zt = zst_ref[...]
    trow = lax.broadcasted_iota(jnp.int32, zt.shape, 0)
    is_f = (lax.shift_right_logical(trow, 2) & 1) == 1
    lft_s[...] = jnp.where(is_f, _log_sigmoid(zt + brow_ref[...]), zt)

    for d in range(N_DIR):
        for h in range(H_MLSTM):
            if has_state:
                c_s[d, h] = c0_ref[d, h]
                n_s[d, h] = n0_ref[d, pl.ds(h, 1), :]
                m_s[d, h] = jnp.broadcast_to(m0_ref[pl.ds(d, 1), pl.ds(h, 1)], (1, LANE))
            else:
                c_s[d, h] = jnp.zeros((dh, dh), F32)
                n_s[d, h] = jnp.zeros((1, dh), F32)
                m_s[d, h] = jnp.zeros((1, LANE), F32)
    hacc_s[...] = jnp.zeros_like(hacc_s)

    ri = lax.broadcasted_iota(jnp.int32, (c_len, c_len), 0)
    ci = lax.broadcasted_iota(jnp.int32, (c_len, c_len), 1)
    causal = (ri >= ci, ri <= ci)
    tri_c = tuple(jnp.where(c, 1.0, 0.0).astype(BF16) for c in causal)
    tri_r = (tri_c[1], tri_c[0])

    for d in range(N_DIR):
        for cb in range(n_chunks):
            rows = pl.ds(cb * c_len, c_len)
            l3 = _split3(lf_s[rows, :])
            cumc_s[d, rows, :] = _dot(tri_c[d], l3[0]) + _dot(tri_c[d], l3[1]) + _dot(tri_c[d], l3[2])
            r3 = _split3(lft_s[:, rows])
            cumr_s[d, :, rows] = _dot(r3[0], tri_r[d]) + _dot(r3[1], tri_r[d]) + _dot(r3[2], tri_r[d])

    ones_c = jnp.ones((c_len, LANE), BF16)

    def lanes2(x):
        return jnp.concatenate([x, x], axis=1)

    def matmuls(d, h, row0):
        rows = pl.ds(row0, c_len)
        hs = slice(dh * h, dh * (h + 1))
        q = qc_s[rows, hs]
        s_raw = _dot_nt(q, kc_s[rows, hs].astype(BF16))
        qc = _dot(q, c_s[d, h].astype(BF16))
        n_rep = jnp.broadcast_to(n_s[d, h], (LANE, dh)).astype(BF16)
        qn = _dot_nt(q, n_rep)
        return dict(d=d, h=h, rows=rows, row0=row0, hs=hs, s_raw=s_raw, qc=qc, qn=qn)

    def gates(w):
        d, h, rows, row0 = w["d"], w["h"], w["rows"], w["row0"]
        cf = COL_F[d] + h
        end = c_len - 1 if d == 0 else 0
        cum_col = cumc_s[d, rows, pl.ds(cf, 1)]
        cum_c = jnp.broadcast_to(cum_col, (c_len, LANE))
        b_c = jnp.broadcast_to(zs_ref[rows, pl.ds(COL_I[d] + h, 1)] - cum_col, (c_len, LANE))
        g_end = end - end % 8
        cum_end = cumc_s[d, pl.ds(pl.multiple_of(row0 + g_end, 8), 8), pl.ds(cf, 1)][end - g_end:end - g_end + 1, :]
        b_r = lft_s[pl.ds(ROW_I[d] + h, 1), rows] - cumr_s[d, pl.ds(ROW_F[d] + h, 1), rows]
        m_prev = m_s[d, h]
        b_m = jnp.where(causal[d], b_r, -jnp.inf)
        m_rel = jnp.maximum(m_prev, jnp.broadcast_to(jnp.max(b_m, axis=1, keepdims=True), (c_len, LANE)))
        w["dmat"] = jnp.exp(b_m - m_rel)
        w["inter"] = jnp.exp(m_prev - m_rel)
        w["floor"] = jnp.exp(-(cum_c + m_rel))
        log_w = cum_end + b_c
        m_new = jnp.maximum(cum_end + m_prev, jnp.max(log_w, axis=0, keepdims=True))
        w["wgt"] = jnp.exp(log_w - m_new)
        w["decay"] = jnp.exp(cum_end + m_prev - m_new)
        w["m_new"] = m_new

    def body(j, carry):
        row0 = (pl.multiple_of(j * c_len, c_len), pl.multiple_of((n_chunks - 1 - j) * c_len, c_len))
        work = [matmuls(d, h, row0[d]) for d in range(N_DIR) for h in range(H_MLSTM)]
        for w in work:
            gates(w)
        for w in work:
            s = w["s_raw"] * w["dmat"]
            s_hi = s.astype(BF16)
            s_lo = (s - s_hi.astype(F32)).astype(BF16)
            w["sv"] = _dot(s_hi, v_ref[w["rows"], w["hs"]])
            w["rsum"] = _dot(s_hi, ones_c) + _dot(s_lo, ones_c)
        for w in work:
            den = w["rsum"] + w["inter"] * w["qn"]
            rn = 1.0 / jnp.maximum(jnp.abs(den), w["floor"])
            hacc_s[w["rows"], w["hs"]] += w["sv"] * lanes2(rn) + w["qc"] * lanes2(w["inter"] * rn)
        for w in work:
            d, h = w["d"], w["h"]
            kw = kc_s[w["rows"], w["hs"]] * lanes2(w["wgt"])
            decay = w["decay"]
            c_s[d, h] = decay[:, :1] * c_s[d, h] + _dot_tn(kw.astype(BF16), v_ref[w["rows"], w["hs"]])
            n_s[d, h] = lanes2(decay) * n_s[d, h] + jnp.sum(kw, axis=0, keepdims=True)
            m_s[d, h] = w["m_new"]
        return carry

    lax.fori_loop(0, n_chunks, body, 0)

    if emit_state:
        for d in range(N_DIR):
            for h in range(H_MLSTM):
                cout_ref[d, h] = c_s[d, h]
                nout_ref[d, pl.ds(h, 1), :] = n_s[d, h]
                mout_ref[pl.ds(d, 1), pl.ds(h, 1)] = m_s[d, h][:, :1]


def _mlstm_call(zb, zs, zst, wconv, bconv, bcol, brow, states, carried, *, layer, n_seq, seq_len,
                emit_state):
    has_state = states is not None
    t = seq_len
    dh = DH_MLSTM
    n_carried = 0 if carried is None else 3
    kern = functools.partial(_mlstm_kernel, seq_len=t, has_state=has_state, emit_state=emit_state,
                             n_carried=n_carried)
    in_specs = [
        pl.BlockSpec((t, W_MLSTM), lambda b: (b, 3)),
        pl.BlockSpec((t, W_MLSTM), lambda b: (b, 4)),
        pl.BlockSpec((t, W_MLSTM), lambda b: (b, 5)),
        pl.BlockSpec((t, Z_SMALL), lambda b: (b, 0)),
        pl.BlockSpec((16, t), lambda b: (0, b)),
        pl.BlockSpec((None, 3, 2 * W_MLSTM), lambda b: (layer, 0, 0)),
        pl.BlockSpec((None, 1, 2 * W_MLSTM), lambda b: (layer, 0, 0)),
        pl.BlockSpec((None, 1, Z_SMALL), lambda b: (layer, 0, 0)),
        pl.BlockSpec((None, 16, 1), lambda b: (layer, 0, 0)),
    ]
    args = [zb, zb, zb, zs, zst, wconv, bconv, bcol, brow]
    if has_state:
        c0, n0, m0 = states
        in_specs += [
            pl.BlockSpec((None, None, N_DIR, H_MLSTM, dh, dh), lambda b: (b, layer, 0, 0, 0, 0)),
            pl.BlockSpec((None, None, N_DIR, H_MLSTM, dh), lambda b: (b, layer, 0, 0, 0)),
            pl.BlockSpec((None, None, N_DIR, H_MLSTM), lambda b: (b, layer, 0, 0)),
        ]
        args += [c0, n0, m0]
    aliases = {}
    for k in range(n_carried):
        aliases[len(args)] = 1 + k
        in_specs.append(pl.BlockSpec(memory_space=pl.ANY))
        args.append(carried[k])
    out_specs = [pl.BlockSpec((t, W_MLSTM), lambda b: (b, 0))]
    out_shape = [jax.ShapeDtypeStruct((n_seq * t, W_MLSTM), F32)]
    if emit_state:
        out_specs += [
            pl.BlockSpec((None, None, N_DIR, H_MLSTM, dh, dh), lambda b: (b, layer, 0, 0, 0, 0)),
            pl.BlockSpec((None, None, N_DIR, H_MLSTM, dh), lambda b: (b, layer, 0, 0, 0)),
            pl.BlockSpec((None, None, N_DIR, H_MLSTM), lambda b: (b, layer, 0, 0)),
        ]
        out_shape += [
            jax.ShapeDtypeStruct((n_seq, DEPTH, N_DIR, H_MLSTM, dh, dh), F32),
            jax.ShapeDtypeStruct((n_seq, DEPTH, N_DIR, H_MLSTM, dh), F32),
            jax.ShapeDtypeStruct((n_seq, DEPTH, N_DIR, H_MLSTM), F32),
        ]
    return pl.pallas_call(
        kern,
        grid=(n_seq,),
        in_specs=in_specs,
        out_specs=out_specs,
        out_shape=out_shape,
        input_output_aliases=aliases,
        scratch_shapes=[
            pltpu.VMEM((t, W_MLSTM), BF16),
            pltpu.VMEM((t, W_MLSTM), F32),
            pltpu.VMEM((t + 16, dh), F32),
            pltpu.VMEM((t, Z_SMALL), F32),
            pltpu.VMEM((16, t), F32),
            pltpu.VMEM((N_DIR, H_MLSTM, dh, dh), F32),
            pltpu.VMEM((N_DIR, H_MLSTM, 1, dh), F32),
            pltpu.VMEM((N_DIR, H_MLSTM, 1, LANE), F32),
            pltpu.VMEM((N_DIR, t, Z_SMALL), F32),
            pltpu.VMEM((N_DIR, 16, t), F32),
        ],
        compiler_params=_cparams(1),
        name="mlstm_scan",
    )(*args)


def _head_rms(o, g, n_heads, width):
    out = []
    for h in range(n_heads):
        oh = o[:, width * h:width * (h + 1)]
        out.append(oh * lax.rsqrt(jnp.mean(oh * oh, axis=-1, keepdims=True) + NORM_EPS) * g)
    return out


def _mix_kernel(x_ref, og_ref, hm_ref, rg_ref, om_ref, gg_ref, gm_ref, mod_ref, gng_ref, gnm_ref,
                wbg_ref, wbm_ref, wo_ref, lg_ref, lb_ref, o_ref, *, cond_row0, tiles_per_cond):
    r = _cond_row(pl.program_id(0), cond_row0, tiles_per_cond)
    gate = mod_ref[2, pl.ds(r, 1), :]
    for s in range(MIX_TM // MIX_SUB):
        rows = pl.ds(s * MIX_SUB, MIX_SUB)
        o_g = jnp.concatenate(_head_rms(og_ref[rows, :] * (DK_GLA ** -0.5), gng_ref[...], H_GLA, DV_GLA), axis=1)
        a_g = (o_g * rg_ref[rows, :].astype(F32)).astype(BF16)
        h_m = jnp.concatenate(_head_rms(hm_ref[rows, :], gnm_ref[...], H_MLSTM, DH_MLSTM), axis=1)
        a_m = (om_ref[rows, :].astype(F32) * h_m).astype(BF16)
        y_g = _dot(a_g, wbg_ref[...])
        y_m = _dot(a_m, wbm_ref[...])
        mix = gg_ref[rows, :].astype(F32) * y_g + gm_ref[rows, :].astype(F32) * y_m
        y = _dot(mix.astype(BF16), wo_ref[...])
        o_ref[rows, :] = _layer_norm(ALPHA * x_ref[rows, :] + gate * y, lg_ref[...], lb_ref[...])


def _mix_call(x, og, hm, zb, ada4, gn_gla, gn_ml, wbg, wbm, wo, lng, lnb, *, layer, cond_row0, rows_per_cond):
    m = x.shape[0]
    tm = MIX_TM
    kern = functools.partial(_mix_kernel, cond_row0=cond_row0, tiles_per_cond=rows_per_cond // tm)
    wspec = pl.BlockSpec((None, D_MODEL, D_MODEL), lambda i: (layer, 0, 0), pipeline_mode=pl.Buffered(1))
    return pl.pallas_call(
        kern,
        grid=(m // tm,),
        in_specs=[
            pl.BlockSpec((tm, D_MODEL), lambda i: (i, 0)),
            pl.BlockSpec((tm, V_GLA), lambda i: (i, 0)),
            pl.BlockSpec((tm, W_MLSTM), lambda i: (i, 0)),
            pl.BlockSpec((tm, V_GLA), lambda i: (i, 2)),
            pl.BlockSpec((tm, W_MLSTM), lambda i: (i, 6)),
            pl.BlockSpec((tm, D_MODEL), lambda i: (i, 7)),
            pl.BlockSpec((tm, D_MODEL), lambda i: (i, 8)),
            pl.BlockSpec((None, 3, 16, D_MODEL), lambda i: (layer, 1, 0, 0)),
            pl.BlockSpec((None, 1, DV_GLA), lambda i: (layer, 0, 0)),
            pl.BlockSpec((None, 1, DH_MLSTM), lambda i: (layer, 0, 0)),
            wspec, wspec, wspec,
            pl.BlockSpec((None, 1, D_MODEL), lambda i: (3 * layer + 1, 0, 0)),
            pl.BlockSpec((None, 1, D_MODEL), lambda i: (3 * layer + 1, 0, 0)),
        ],
        out_specs=pl.BlockSpec((tm, D_MODEL), lambda i: (i, 0)),
        out_shape=jax.ShapeDtypeStruct((m, D_MODEL), F32),
        compiler_params=_cparams(1),
        name="mixer_out_ln",
    )(x, og, hm, zb, zb, zb, zb, ada4, gn_gla, gn_ml, wbg, wbm, wo, lng, lnb)


def _grid_pos_embed(t_len):
    rows = t_len // GRID_W
    r = jnp.repeat(jnp.arange(rows), GRID_W).astype(F32)
    col = jnp.tile(jnp.arange(GRID_W), rows).astype(F32)
    nf = D_MODEL // 4
    omega = 1.0 / (10000.0 ** (jnp.arange(nf, dtype=F32) / nf))
    er = r[:, None] * omega
    ec = col[:, None] * omega
    return jnp.concatenate([jnp.sin(er), jnp.cos(er), jnp.sin(ec), jnp.cos(ec)], axis=-1)


def _pack_ffn(w_gate, w_up, w_down):
    return w_gate.astype(BF16), w_up.astype(BF16), w_down.astype(BF16)


def _pack_w_in(w_in):
    o = np.cumsum((0, QK_GLA, QK_GLA, V_GLA, V_GLA, GLA_RANK, GLA_RANK, 2 * W_MLSTM, W_MLSTM, W_MLSTM,
                   H_MLSTM, H_MLSTM, H_MLSTM, H_MLSTM, D_MODEL, D_MODEL))
    o = [int(v) for v in o]
    big = jnp.concatenate([w_in[:, :, o[0]:o[4]], w_in[:, :, o[6]:o[9]], w_in[:, :, o[13]:o[15]]],
                          axis=-1).astype(BF16)
    small = jnp.concatenate([w_in[:, :, o[4]:o[6]], w_in[:, :, o[9]:o[13]]], axis=-1)
    small_p = jnp.pad(small, ((0, 0), (0, 0), (0, Z_SMALL - small.shape[-1]))).astype(BF16)
    return big, small_p


def kernel(x_prompt, x_sample, c, state_gla_s, state_mlstm_c, state_mlstm_n, state_mlstm_m, c_ctx,
           w_ada, b_ada, ffn1_w_gate, ffn1_w_up, ffn1_w_down, w_in, w_decay, b_decay, w_conv, b_conv,
           f_bias, gla_norm_g, mlstm_norm_g, w_br_gla, w_br_mlstm, w_out,
           ffn2_w_gate, ffn2_w_up, ffn2_w_down, ln_g, ln_b):
    bp, tp, _ = x_prompt.shape
    bs, ts, _ = x_sample.shape
    assert bs + 1 <= 16 and tp % FIN_ROWS == 0 and ts % FIN_ROWS == 0

    cond16 = jnp.zeros((16, D_MODEL), F32).at[0].set(c_ctx).at[1:1 + bs].set(c)
    ada4 = _ada_call(cond16, w_ada, b_ada)

    ffn1 = _pack_ffn(ffn1_w_gate, ffn1_w_up, ffn1_w_down)
    ffn2 = _pack_ffn(ffn2_w_gate, ffn2_w_up, ffn2_w_down)
    w_big, w_small = _pack_w_in(w_in)
    wdec = jnp.zeros((DEPTH, N_DIR, Z_SMALL, QK_GLA), F32)
    wdec = wdec.at[:, 0, 0:GLA_RANK].set(w_decay[:, 0]).at[:, 1, GLA_RANK:2 * GLA_RANK].set(w_decay[:, 1])
    bdec = b_decay.reshape(DEPTH, N_DIR, 1, QK_GLA)
    bcol = jnp.zeros((DEPTH, 1, Z_SMALL), F32)
    brow = jnp.zeros((DEPTH, 16, 1), F32)
    for d in range(N_DIR):
        bcol = bcol.at[:, 0, COL_F[d]:COL_F[d] + H_MLSTM].set(f_bias[:, d])
        brow = brow.at[:, ROW_F[d]:ROW_F[d] + H_MLSTM, 0].set(f_bias[:, d])
    bconv = b_conv.reshape(DEPTH, 1, 2 * W_MLSTM)
    gn_gla = gla_norm_g.reshape(DEPTH, 1, DV_GLA)
    gn_ml = mlstm_norm_g.reshape(DEPTH, 1, DH_MLSTM)
    wbg, wbm, wo = w_br_gla.astype(BF16), w_br_mlstm.astype(BF16), w_out.astype(BF16)
    lng = ln_g.reshape(DEPTH * 3, 1, D_MODEL)
    lnb = ln_b.reshape(DEPTH * 3, 1, D_MODEL)
    pos = _grid_pos_embed(ts)

    def run_pass(x, n_seq, seq_len, cond_row0, rows_per_cond, pos_embed, states):
        emit = states is None
        carried = None
        kw = dict(cond_row0=cond_row0, rows_per_cond=rows_per_cond)
        for l in range(DEPTH):
            x = _ffn_call(x, pos_embed if l == 0 else None, ada4, *ffn1, lng, lnb, layer=l, sub=0, **kw)
            zb, zs, zst = _proj_call(x, ada4, w_big, w_small, layer=l, **kw)
            gla = _gla_call(zb, zs, wdec, bdec, None if emit else states[0],
                            None if carried is None else carried[0],
                            layer=l, n_seq=n_seq, seq_len=seq_len, emit_state=emit)
            ml = _mlstm_call(zb, zs, zst, w_conv, bconv, bcol, brow, None if emit else states[1:],
                             None if carried is None else carried[1:],
                             layer=l, n_seq=n_seq, seq_len=seq_len, emit_state=emit)
            x = _mix_call(x, gla[0], ml[0], zb, ada4, gn_gla, gn_ml, wbg, wbm, wo, lng, lnb, layer=l, **kw)
            x = _ffn_call(x, None, ada4, *ffn2, lng, lnb, layer=l, sub=2, **kw)
            if emit:
                carried = (gla[1], ml[1], ml[2], ml[3])
        return x, carried

    m_ctx = bp * tp
    y_p, new_states = run_pass(x_prompt.reshape(m_ctx, D_MODEL), bp, tp, 0, m_ctx, None, None)
    y_s, _ = run_pass(x_sample.reshape(bs * ts, D_MODEL), bs, ts, 1, ts, pos,
                      (state_gla_s, state_mlstm_c, state_mlstm_n, state_mlstm_m))
    return (y_p.reshape(bp, tp, D_MODEL), y_s.reshape(bs, ts, D_MODEL)) + tuple(new_states)
```

```python
import functools

import jax
import jax.numpy as jnp
import numpy as np
from jax import lax
from jax.experimental import pallas as pl
from jax.experimental.pallas import tpu as pltpu

F32 = jnp.float32
BF16 = jnp.bfloat16

D_MODEL = 1024
DEPTH = 2
GRID_W = 64
N_DIR = 2
H_GLA = 4
DK_GLA = 128
DV_GLA = 256
GLA_RANK = 16
GLA_TAU = 16.0
H_MLSTM = 4
DH_MLSTM = 256
D_FF = 2816
N_MOD = 9
LN_EPS = 1e-5
NORM_EPS = 1e-6
QK_GLA = H_GLA * DK_GLA
V_GLA = H_GLA * DV_GLA
W_MLSTM = H_MLSTM * DH_MLSTM
ALPHA = (2.0 * DEPTH) ** 0.25

Z_BIG = 9 * 1024
Z_SMALL = 128
SLAB_V_GLA, SLAB_R_GLA, SLAB_Q_MLSTM, SLAB_K_MLSTM, SLAB_V_MLSTM = 1, 2, 3, 4, 5
SLAB_O_MLSTM, SLAB_G_GLA, SLAB_G_MLSTM = 6, 7, 8
COL_I = (32, 40)
COL_F = (36, 44)
ROW_I = (0, 8)
ROW_F = (4, 12)

LANE = 128
SUBLANE = 8
VMEM_LIMIT = 56 * 1024 * 1024

FFN_TM = 1024
FFN_SUB = 512
FFN_CHUNKS = ((0, 1536), (1536, 1280))
PROJ_TM = 512
PROJ_TN = 1024
MIX_TM = 512
FUSE_TM = 256
GLA_CHUNK = 64
GLA_SUB = 16
MLSTM_CHUNK = 128
FIN_ROWS = 256
MAX_UNROLLED_CHUNKS = 4


def _dot(a, b):
    return jnp.dot(a, b, preferred_element_type=F32)


def _dot_nt(a, b):
    return lax.dot_general(a, b, (((1,), (1,)), ((), ())), preferred_element_type=F32)


def _dot_tn(a, b):
    return lax.dot_general(a, b, (((0,), (0,)), ((), ())), preferred_element_type=F32)


def _split3(x):
    hi = x.astype(BF16)
    r = x - hi.astype(F32)
    mid = r.astype(BF16)
    lo = (r - mid.astype(F32)).astype(BF16)
    return hi, mid, lo


def _split2(x):
    hi = x.astype(BF16)
    return hi, (x - hi.astype(F32)).astype(BF16)


def _dot_acc(x, w):
    xh, xm, _ = _split3(x)
    wh, wl = _split2(w)
    return _dot(xh, wh) + _dot(xh, wl) + _dot(xm, wh)


def _silu(x):
    return x * jax.nn.sigmoid(x)


def _log_sigmoid(x):
    return jnp.minimum(x, 0.0) - jnp.log(1.0 + jnp.exp(-jnp.abs(x)))


def _layer_norm(y, g, b):
    mu = jnp.mean(y, axis=-1, keepdims=True)
    d = y - mu
    var = jnp.mean(d * d, axis=-1, keepdims=True)
    return d * lax.rsqrt(var + LN_EPS) * g + b


def _aligned(x, m):
    return x if isinstance(x, int) else pl.multiple_of(x, m)


def _cparams(n_axes):
    return pltpu.CompilerParams(
        dimension_semantics=("arbitrary",) * n_axes, vmem_limit_bytes=VMEM_LIMIT)


def _cond_row(i, cond_row0, tiles_per_cond):
    return cond_row0 + lax.div(i, jnp.int32(tiles_per_cond))


def _run(call):
    return pl.pallas_call(
        call["kernel"],
        grid=call["grid"],
        in_specs=call["in_specs"],
        out_specs=call["out_specs"],
        out_shape=call["out_shape"],
        scratch_shapes=call["scratch"],
        input_output_aliases=call["aliases"],
        compiler_params=_cparams(len(call["grid"])),
        name=call["name"],
    )(*call["args"])


def _run_fused(scan, mm, name):
    assert scan["grid"] == mm["grid"] and not mm["aliases"]
    counts = [len(scan["args"]), len(mm["args"]), len(scan["out_shape"]), len(mm["out_shape"]),
              len(scan["scratch"]), len(mm["scratch"])]

    def kern(*refs):
        parts, i = [], 0
        for n in counts:
            parts.append(tuple(refs[i:i + n]))
            i += n
        in_a, in_b, out_a, out_b, scr_a, scr_b = parts
        scan["kernel"](*(in_a + out_a + scr_a), hooks=mm["pieces"](in_b + out_b + scr_b))

    outs = pl.pallas_call(
        kern,
        grid=scan["grid"],
        in_specs=scan["in_specs"] + mm["in_specs"],
        out_specs=scan["out_specs"] + mm["out_specs"],
        out_shape=scan["out_shape"] + mm["out_shape"],
        scratch_shapes=scan["scratch"] + mm["scratch"],
        input_output_aliases=scan["aliases"],
        compiler_params=_cparams(len(scan["grid"])),
        name=name,
    )(*scan["args"], *mm["args"])
    n_a = counts[2]
    return outs[:n_a], outs[n_a:]


def _hook_runner(hooks):
    pending = list(hooks)

    def hook(n=1):
        for _ in range(n):
            if pending:
                pending.pop(0)()

    def flush():
        while pending:
            pending.pop(0)()

    return hook, flush


def _ada_kernel(c_ref, w_ref, b_ref, o_ref):
    o_ref[...] = _dot_acc(_silu(c_ref[...]), w_ref[...]) + b_ref[...]


def _ada_call(cond16, w_ada, b_ada):
    return pl.pallas_call(
        _ada_kernel,
        grid=(DEPTH, N_MOD),
        in_specs=[
            pl.BlockSpec((16, D_MODEL), lambda l, j: (0, 0)),
            pl.BlockSpec((None, D_MODEL, D_MODEL), lambda l, j: (l, 0, j)),
            pl.BlockSpec((None, None, 1, D_MODEL), lambda l, j: (l, j, 0, 0)),
        ],
        out_specs=pl.BlockSpec((None, None, 16, D_MODEL), lambda l, j: (l, j, 0, 0)),
        out_shape=jax.ShapeDtypeStruct((DEPTH, N_MOD, 16, D_MODEL), F32),
        compiler_params=_cparams(2),
        name="ada_mod",
    )(cond16, w_ada, b_ada.reshape(DEPTH, N_MOD, 1, D_MODEL))


def _ffn_pieces(refs, *, has_pos, cond_row0, tiles_per_cond, tm, sub):
    if has_pos:
        x_ref, pos_ref, mod_ref, wg_ref, wu_ref, wd_ref, lg_ref, lb_ref, o_ref = refs
    else:
        x_ref, mod_ref, wg_ref, wu_ref, wd_ref, lg_ref, lb_ref, o_ref = refs
        pos_ref = None
    st = {}
    pieces = []

    def start(s):
        rows = pl.ds(s * sub, sub)
        if s == 0:
            r = _cond_row(pl.program_id(0), cond_row0, tiles_per_cond)
            st["sh"] = mod_ref[0, pl.ds(r, 1), :]
            st["sc"] = mod_ref[1, pl.ds(r, 1), :]
            st["gate"] = mod_ref[2, pl.ds(r, 1), :]
        x = x_ref[rows, :]
        if has_pos:
            x = x + pos_ref[rows, :]
        st["x"] = x
        st["hm"] = (x * (1.0 + st["sc"]) + st["sh"]).astype(BF16)
        st["acc"] = None

    def gate_mm(c0, cw):
        st["g"] = _dot(st["hm"], wg_ref[:, c0:c0 + cw])

    def up_mm(c0, cw):
        st["u"] = _dot(st["hm"], wu_ref[:, c0:c0 + cw])

    def down_mm(c0, cw):
        part = _dot((_silu(st["g"]) * st["u"]).astype(BF16), wd_ref[c0:c0 + cw, :])
        st["acc"] = part if st["acc"] is None else st["acc"] + part

    def finish(s):
        y = ALPHA * st["x"] + (0.5 * st["gate"]) * st["acc"]
        o_ref[pl.ds(s * sub, sub), :] = _layer_norm(y, lg_ref[...], lb_ref[...])

    for s in range(tm // sub):
        pieces.append(functools.partial(start, s))
        for c0, cw in FFN_CHUNKS:
            pieces += [functools.partial(gate_mm, c0, cw), functools.partial(up_mm, c0, cw),
                       functools.partial(down_mm, c0, cw)]
        pieces.append(functools.partial(finish, s))
    return pieces


def _ffn_call(x, pos, ada4, wg, wu, wd, lng, lnb, *, layer, sub, cond_row0, rows_per_cond, tm=FFN_TM):
    m = x.shape[0]
    has_pos = pos is not None
    static = dict(has_pos=has_pos, cond_row0=cond_row0, tiles_per_cond=rows_per_cond // tm,
                  tm=tm, sub=min(tm, FFN_SUB))
    pieces = functools.partial(_ffn_pieces, **static)

    def kern(*refs):
        for p in pieces(refs):
            p()

    once = pl.Buffered(1)
    in_specs = [pl.BlockSpec((tm, D_MODEL), lambda i: (i, 0))]
    args = [x]
    if has_pos:
        n_pos = pos.shape[0] // tm
        if n_pos == 1:
            in_specs.append(pl.BlockSpec((tm, D_MODEL), lambda i: (0, 0), pipeline_mode=once))
        else:
            in_specs.append(pl.BlockSpec((tm, D_MODEL), lambda i: (lax.rem(i, jnp.int32(n_pos)), 0)))
        args.append(pos)
    in_specs += [
        pl.BlockSpec((None, 3, 16, D_MODEL), lambda i: (layer, sub, 0, 0)),
        pl.BlockSpec((None, D_MODEL, D_FF), lambda i: (layer, 0, 0), pipeline_mode=once),
        pl.BlockSpec((None, D_MODEL, D_FF), lambda i: (layer, 0, 0), pipeline_mode=once),
        pl.BlockSpec((None, D_FF, D_MODEL), lambda i: (layer, 0, 0), pipeline_mode=once),
        pl.BlockSpec((None, 1, D_MODEL), lambda i: (3 * layer + sub, 0, 0)),
        pl.BlockSpec((None, 1, D_MODEL), lambda i: (3 * layer + sub, 0, 0)),
    ]
    args += [ada4, wg, wu, wd, lng, lnb]
    return dict(kernel=kern, pieces=pieces, grid=(m // tm,), in_specs=in_specs, args=args,
                out_specs=[pl.BlockSpec((tm, D_MODEL), lambda i: (i, 0))],
                out_shape=[jax.ShapeDtypeStruct((m, D_MODEL), F32)],
                scratch=[], aliases={}, name="ffn_ln")


def _proj_pieces(refs, *, cond_row0, tiles_per_cond):
    x_ref, mod_ref, wb_ref, ws_ref, zb_ref, zs_ref, zst_ref = refs
    st = {}

    def start():
        r = _cond_row(pl.program_id(0), cond_row0, tiles_per_cond)
        sh = mod_ref[0, pl.ds(r, 1), :]
        sc = mod_ref[1, pl.ds(r, 1), :]
        st["hm"] = (x_ref[...] * (1.0 + sc) + sh).astype(BF16)
        zs = _dot(st["hm"], ws_ref[...])
        zs_ref[...] = zs
        zst_ref[...] = zs.T[COL_I[0]:COL_I[0] + 16, :]

    def slab(n):
        cols = slice(n * PROJ_TN, (n + 1) * PROJ_TN)
        z = _dot(st["hm"], wb_ref[:, cols])
        if n == SLAB_R_GLA:
            z = _silu(z)
        elif n in (SLAB_O_MLSTM, SLAB_G_GLA, SLAB_G_MLSTM):
            z = jax.nn.sigmoid(z)
        zb_ref[:, cols] = z.astype(BF16)

    return [start] + [functools.partial(slab, n) for n in range(Z_BIG // PROJ_TN)]


def _proj_call(x, ada4, w_big, w_small, *, layer, cond_row0, rows_per_cond, tm=PROJ_TM):
    m = x.shape[0]
    pieces = functools.partial(_proj_pieces, cond_row0=cond_row0, tiles_per_cond=rows_per_cond // tm)

    def kern(*refs):
        for p in pieces(refs):
            p()

    once = pl.Buffered(1)
    return dict(
        kernel=kern, pieces=pieces, grid=(m // tm,),
        in_specs=[
            pl.BlockSpec((tm, D_MODEL), lambda i: (i, 0)),
            pl.BlockSpec((None, 3, 16, D_MODEL), lambda i: (layer, 1, 0, 0)),
            pl.BlockSpec((None, D_MODEL, Z_BIG), lambda i: (layer, 0, 0), pipeline_mode=once),
            pl.BlockSpec((None, D_MODEL, Z_SMALL), lambda i: (layer, 0, 0), pipeline_mode=once),
        ],
        args=[x, ada4, w_big, w_small],
        out_specs=[
            pl.BlockSpec((tm, Z_BIG), lambda i: (i, 0)),
            pl.BlockSpec((tm, Z_SMALL), lambda i: (i, 0)),
            pl.BlockSpec((16, tm), lambda i: (0, i)),
        ],
        out_shape=[
            jax.ShapeDtypeStruct((m, Z_BIG), BF16),
            jax.ShapeDtypeStruct((m, Z_SMALL), F32),
            jax.ShapeDtypeStruct((16, m), F32),
        ],
        scratch=[], aliases={}, name="mixer_in_proj")


def _gla_kernel(*refs, seq_len, has_state, emit_state, n_carried, hooks=()):
    it = iter(refs)
    q_ref, k_ref, v_ref, zs_ref, wdec_ref, bdec_ref = (next(it) for _ in range(6))
    s0_ref = next(it) if has_state else None
    if n_carried:
        next(it)
    oacc_s = next(it)
    sout_ref = next(it) if emit_state else None
    la_s, st_s, cum_s = (next(it) for _ in range(3))
    t_len = seq_len
    c_len = GLA_CHUNK
    n_chunks = t_len // c_len
    unrolled = n_chunks <= MAX_UNROLLED_CHUNKS
    assert unrolled or not hooks
    hook, flush = _hook_runner(hooks)
    hook(4)

    for rb in range(t_len // FIN_ROWS):
        rows = pl.ds(rb * FIN_ROWS, FIN_ROWS)
        zh, zm, _ = _split3(zs_ref[rows, :])
        for d in range(N_DIR):
            wh, wl = _split2(wdec_ref[d])
            x = _dot(zh, wh) + _dot(zh, wl) + _dot(zm, wh) + bdec_ref[d]
            la_s[d, rows, :] = _log_sigmoid(x) * (1.0 / GLA_TAU)

    for d in range(N_DIR):
        for h in range(H_GLA):
            if has_state:
                st_s[d, h] = s0_ref[d, h].T
            else:
                st_s[d, h] = jnp.zeros((DV_GLA, DK_GLA), F32)
    oacc_s[...] = jnp.zeros_like(oacc_s)

    ri = lax.broadcasted_iota(jnp.int32, (c_len, c_len), 0)
    ci = lax.broadcasted_iota(jnp.int32, (c_len, c_len), 1)
    rr = lax.broadcasted_iota(jnp.int32, (c_len, DK_GLA), 0)
    odd_rows = (lax.shift_right_logical(rr, 4) & 1) == 1
    upper_rows = rr >= 2 * GLA_SUB
    same_sub = lax.shift_right_logical(ri, 4) == lax.shift_right_logical(ci, 4)
    same_half = lax.shift_right_logical(ri, 5) == lax.shift_right_logical(ci, 5)
    causal = (ri >= ci, ri <= ci)
    tri = tuple(jnp.where(c, 1.0, 0.0).astype(BF16) for c in causal)
    pm0 = tuple(same_sub & c for c in causal)
    q1rows = (odd_rows, jnp.logical_not(odd_rows))
    q2rows = (upper_rows, jnp.logical_not(upper_rows))

    def bcast(x, n):
        return jnp.broadcast_to(x, (n, DK_GLA))

    for d in range(N_DIR):
        for cb in range(n_chunks):
            rows = pl.ds(cb * c_len, c_len)
            l3 = _split3(la_s[d, rows, :])
            cum_s[d, rows, :] = _dot(tri[d], l3[0]) + _dot(tri[d], l3[1]) + _dot(tri[d], l3[2])
    hook()

    def scores(d, h, row0):
        rows = pl.ds(row0, c_len)
        ls = slice(DK_GLA * h, DK_GLA * (h + 1))
        vs = slice(DV_GLA * h, DV_GLA * (h + 1))

        def row(i):
            g = i - i % SUBLANE
            return cum_s[d, pl.ds(_aligned(row0 + g, SUBLANE), SUBLANE), ls][i - g:i - g + 1, :]

        c = cum_s[d, rows, ls]
        zero = jnp.zeros((GLA_SUB, DK_GLA), F32)
        if d == 0:
            ref0 = jnp.concatenate([zero, bcast(row(15), 16), bcast(row(31), 16), bcast(row(47), 16)], axis=0)
            ref1 = jnp.concatenate([bcast(row(15), 32), bcast(row(47), 32)], axis=0)
            ref2, cend = row(31), row(63)
        else:
            ref0 = jnp.concatenate([bcast(row(16), 16), bcast(row(32), 16), bcast(row(48), 16), zero], axis=0)
            ref1 = jnp.concatenate([bcast(row(16), 32), bcast(row(48), 32)], axis=0)
            ref2, cend = row(32), row(0)
        q = q_ref[rows, ls].astype(F32)
        k = k_ref[rows, ls].astype(F32)
        e0 = c - ref0
        p0 = _dot_nt((q * jnp.exp(e0)).astype(BF16), (k * jnp.exp(-e0)).astype(BF16))
        x1 = jnp.exp(-jnp.abs(c - ref1))
        p1 = _dot_nt(jnp.where(q1rows[d], q * x1, 0.0).astype(BF16),
                     jnp.where(q1rows[d], 0.0, k * x1).astype(BF16))
        x2 = jnp.exp(-jnp.abs(c - ref2))
        p2 = _dot_nt(jnp.where(q2rows[d], q * x2, 0.0).astype(BF16),
                     jnp.where(q2rows[d], 0.0, k * x2).astype(BF16))
        qi = (q * jnp.exp(c)).astype(BF16)
        kst = (k * jnp.exp(cend - c)).astype(BF16)
        return dict(d=d, h=h, rows=rows, vs=vs, p0=p0, p1=p1, p2=p2, qi=qi, kst=kst, dec=jnp.exp(cend))

    def body(j, carry):
        row0 = (_aligned(j * c_len, c_len), _aligned((n_chunks - 1 - j) * c_len, c_len))
        work = [scores(d, h, row0[d]) for d in range(N_DIR) for h in range(H_GLA)]
        hook()
        for w in work:
            d, h = w["d"], w["h"]
            p = jnp.where(pm0[d], w["p0"], jnp.where(same_half, w["p1"], w["p2"]))
            o = _dot(p.astype(BF16), v_ref[w["rows"], w["vs"]]) + _dot_nt(w["qi"], st_s[d, h].astype(BF16))
            oacc_s[w["rows"], w["vs"]] += o
        hook()
        for w in work:
            d, h = w["d"], w["h"]
            st_s[d, h] = st_s[d, h] * w["dec"] + _dot_tn(v_ref[w["rows"], w["vs"]], w["kst"])
        return carry

    if unrolled:
        for j in range(n_chunks):
            body(j, 0)
    else:
        lax.fori_loop(0, n_chunks, body, 0)
    flush()

    if emit_state:
        for d in range(N_DIR):
            for h in range(H_GLA):
                sout_ref[d, h] = st_s[d, h].T


def _gla_call(zb, zs, wdec, bdec, state, carried, *, layer, n_seq, seq_len, emit_state):
    has_state = state is not None
    t = seq_len
    n_carried = 0 if carried is None else 1
    kern = functools.partial(_gla_kernel, seq_len=t, has_state=has_state, emit_state=emit_state,
                             n_carried=n_carried)
    in_specs = [
        pl.BlockSpec((t, QK_GLA), lambda b: (b, 0)),
        pl.BlockSpec((t, QK_GLA), lambda b: (b, 1)),
        pl.BlockSpec((t, V_GLA), lambda b: (b, SLAB_V_GLA)),
        pl.BlockSpec((t, Z_SMALL), lambda b: (b, 0)),
        pl.BlockSpec((None, N_DIR, Z_SMALL, QK_GLA), lambda b: (layer, 0, 0, 0)),
        pl.BlockSpec((None, N_DIR, 1, QK_GLA), lambda b: (layer, 0, 0, 0)),
    ]
    args = [zb, zb, zb, zs, wdec, bdec]
    if has_state:
        in_specs.append(pl.BlockSpec((None, None, N_DIR, H_GLA, DK_GLA, DV_GLA),
                                     lambda b: (b, layer, 0, 0, 0, 0)))
        args.append(state)
    aliases = {}
    if n_carried:
        aliases[len(args)] = 1
        in_specs.append(pl.BlockSpec(memory_space=pl.ANY))
        args.append(carried)
    out_specs = [pl.BlockSpec((t, V_GLA), lambda b: (b, 0))]
    out_shape = [jax.ShapeDtypeStruct((n_seq * t, V_GLA), F32)]
    if emit_state:
        out_specs.append(pl.BlockSpec((None, None, N_DIR, H_GLA, DK_GLA, DV_GLA),
                                      lambda b: (b, layer, 0, 0, 0, 0)))
        out_shape.append(jax.ShapeDtypeStruct((n_seq, DEPTH, N_DIR, H_GLA, DK_GLA, DV_GLA), F32))
    scratch = [
        pltpu.VMEM((N_DIR, t, QK_GLA), F32),
        pltpu.VMEM((N_DIR, H_GLA, DV_GLA, DK_GLA), F32),
        pltpu.VMEM((N_DIR, t, QK_GLA), F32),
    ]
    return dict(kernel=kern, grid=(n_seq,), in_specs=in_specs, args=args, out_specs=out_specs,
                out_shape=out_shape, scratch=scratch, aliases=aliases, name="gla_scan")


def _mlstm_kernel(*refs, seq_len, has_state, emit_state, n_carried, hooks=()):
    it = iter(refs)
    (q_ref, k_ref, v_ref, zs_ref, zst_ref, wconv_ref, bconv_ref,
     bcol_ref, brow_ref) = (next(it) for _ in range(9))
    if has_state:
        c0_ref, n0_ref, m0_ref = (next(it) for _ in range(3))
    for _ in range(n_carried):
        next(it)
    hacc_s = next(it)
    if emit_state:
        cout_ref, nout_ref, mout_ref = (next(it) for _ in range(3))
    (qc_s, kc_s, xp_s, lf_s, lft_s, c_s, n_s, m_s, cumc_s, cumr_s) = (next(it) for _ in range(10))
    t_len = seq_len
    c_len = MLSTM_CHUNK
    n_chunks = t_len // c_len
    dh = DH_MLSTM
    unrolled = n_chunks <= MAX_UNROLLED_CHUNKS
    assert unrolled or not hooks
    hook, flush = _hook_runner(hooks)
    hook(6)

    xp_s[pl.ds(0, 8), :] = jnp.zeros((8, dh), F32)
    xp_s[pl.ds(8 + t_len, 8), :] = jnp.zeros((8, dh), F32)
    for which, src in enumerate((q_ref, k_ref)):
        for h in range(H_MLSTM):
            hs = slice(dh * h, dh * (h + 1))
            ws = slice(which * W_MLSTM + dh * h, which * W_MLSTM + dh * (h + 1))
            xp_s[pl.ds(8, t_len), :] = src[:, hs].astype(F32)
            w0 = wconv_ref[pl.ds(0, 1), ws]
            w1 = wconv_ref[pl.ds(1, 1), ws]
            w2 = wconv_ref[pl.ds(2, 1), ws]
            bias = bconv_ref[:, ws]
            for rb in range(t_len // FIN_ROWS):
                r0 = rb * FIN_ROWS
                y = (w0 * xp_s[pl.ds(r0 + 7, FIN_ROWS), :] + w1 * xp_s[pl.ds(r0 + 8, FIN_ROWS), :]
                     + w2 * xp_s[pl.ds(r0 + 9, FIN_ROWS), :] + bias)
                a = _silu(y)
                if which == 0:
                    qc_s[pl.ds(r0, FIN_ROWS), hs] = (a * (dh ** -0.5)).astype(BF16)
                else:
                    kc_s[pl.ds(r0, FIN_ROWS), hs] = a

    lf_s[...] = _log_sigmoid(zs_ref[...] + bcol_ref[...])
    zt = zst_ref[...]
    trow = lax.broadcasted_iota(jnp.int32, zt.shape, 0)
    is_f = (lax.shift_right_logical(trow, 2) & 1) == 1
    lft_s[...] = jnp.where(is_f, _log_sigmoid(zt + brow_ref[...]), zt)

    for d in range(N_DIR):
        for h in range(H_MLSTM):
            if has_state:
                c_s[d, h] = c0_ref[d, h]
                n_s[d, h] = n0_ref[d, pl.ds(h, 1), :]
                m_s[d, h] = jnp.broadcast_to(m0_ref[pl.ds(d, 1), pl.ds(h, 1)], (1, LANE))
            else:
                c_s[d, h] = jnp.zeros((dh, dh), F32)
                n_s[d, h] = jnp.zeros((1, dh), F32)
                m_s[d, h] = jnp.zeros((1, LANE), F32)
    hacc_s[...] = jnp.zeros_like(hacc_s)

    ri = lax.broadcasted_iota(jnp.int32, (c_len, c_len), 0)
    ci = lax.broadcasted_iota(jnp.int32, (c_len, c_len), 1)
    causal = (ri >= ci, ri <= ci)
    tri_c = tuple(jnp.where(c, 1.0, 0.0).astype(BF16) for c in causal)
    tri_r = (tri_c[1], tri_c[0])

    for d in range(N_DIR):
        for cb in range(n_chunks):
            rows = pl.ds(cb * c_len, c_len)
            l3 = _split3(lf_s[rows, :])
            cumc_s[d, rows, :] = _dot(tri_c[d], l3[0]) + _dot(tri_c[d], l3[1]) + _dot(tri_c[d], l3[2])
            r3 = _split3(lft_s[:, rows])
            cumr_s[d, :, rows] = _dot(r3[0], tri_r[d]) + _dot(r3[1], tri_r[d]) + _dot(r3[2], tri_r[d])

    ones_c = jnp.ones((c_len, LANE), BF16)

    def lanes2(x):
        return jnp.concatenate([x, x], axis=1)

    def matmuls(d, h, row0):
        rows = pl.ds(row0, c_len)
        hs = slice(dh * h, dh * (h + 1))
        q = qc_s[rows, hs]
        s_raw = _dot_nt(q, kc_s[rows, hs].astype(BF16))
        qc = _dot(q, c_s[d, h].astype(BF16))
        n_rep = jnp.broadcast_to(n_s[d, h], (LANE, dh)).astype(BF16)
        qn = _dot_nt(q, n_rep)
        return dict(d=d, h=h, rows=rows, row0=row0, hs=hs, s_raw=s_raw, qc=qc, qn=qn)

    def gates(w):
        d, h, rows, row0 = w["d"], w["h"], w["rows"], w["row0"]
        cf = COL_F[d] + h
        end = c_len - 1 if d == 0 else 0
        cum_col = cumc_s[d, rows, pl.ds(cf, 1)]
        cum_c = jnp.broadcast_to(cum_col, (c_len, LANE))
        b_c = jnp.broadcast_to(zs_ref[rows, pl.ds(COL_I[d] + h, 1)] - cum_col, (c_len, LANE))
        g_end = end - end % SUBLANE
        cum_end = cumc_s[d, pl.ds(_aligned(row0 + g_end, SUBLANE), SUBLANE), pl.ds(cf, 1)][
            end - g_end:end - g_end + 1, :]
        b_r = lft_s[pl.ds(ROW_I[d] + h, 1), rows] - cumr_s[d, pl.ds(ROW_F[d] + h, 1), rows]
        m_prev = m_s[d, h]
        b_m = jnp.where(causal[d], b_r, -jnp.inf)
        m_rel = jnp.maximum(m_prev, jnp.broadcast_to(jnp.max(b_m, axis=1, keepdims=True), (c_len, LANE)))
        w["dmat"] = jnp.exp(b_m - m_rel)
        w["inter"] = jnp.exp(m_prev - m_rel)
        w["floor"] = jnp.exp(-(cum_c + m_rel))
        log_w = cum_end + b_c
        m_new = jnp.maximum(cum_end + m_prev, jnp.max(log_w, axis=0, keepdims=True))
        w["wgt"] = jnp.exp(log_w - m_new)
        w["decay"] = jnp.exp(cum_end + m_prev - m_new)
        w["m_new"] = m_new

    def body(j, carry):
        row0 = (_aligned(j * c_len, c_len), _aligned((n_chunks - 1 - j) * c_len, c_len))
        work = [matmuls(d, h, row0[d]) for d in range(N_DIR) for h in range(H_MLSTM)]
        hook()
        for w in work:
            gates(w)
        for w in work:
            s = w["s_raw"] * w["dmat"]
            s_hi = s.astype(BF16)
            s_lo = (s - s_hi.astype(F32)).astype(BF16)
            w["sv"] = _dot(s_hi, v_ref[w["rows"], w["hs"]])
            w["rsum"] = _dot(s_hi, ones_c) + _dot(s_lo, ones_c)
        hook()
        for w in work:
            den = w["rsum"] + w["inter"] * w["qn"]
            rn = 1.0 / jnp.maximum(jnp.abs(den), w["floor"])
            hacc_s[w["rows"], w["hs"]] += w["sv"] * lanes2(rn) + w["qc"] * lanes2(w["inter"] * rn)
        for w in work:
            d, h = w["d"], w["h"]
            kw = kc_s[w["rows"], w["hs"]] * lanes2(w["wgt"])
            decay = w["decay"]
            c_s[d, h] = decay[:, :1] * c_s[d, h] + _dot_tn(kw.astype(BF16), v_ref[w["rows"], w["hs"]])
            n_s[d, h] = lanes2(decay) * n_s[d, h] + jnp.sum(kw, axis=0, keepdims=True)
            m_s[d, h] = w["m_new"]
        return carry

    if unrolled:
        for j in range(n_chunks):
            body(j, 0)
    else:
        lax.fori_loop(0, n_chunks, body, 0)
    flush()

    if emit_state:
        for d in range(N_DIR):
            for h in range(H_MLSTM):
                cout_ref[d, h] = c_s[d, h]
                nout_ref[d, pl.ds(h, 1), :] = n_s[d, h]
                mout_ref[pl.ds(d, 1), pl.ds(h, 1)] = m_s[d, h][:, :1]


def _mlstm_call(zb, zs, zst, wconv, bconv, bcol, brow, states, carried, *, layer, n_seq, seq_len,
                emit_state):
    has_state = states is not None
    t = seq_len
    dh = DH_MLSTM
    n_carried = 0 if carried is None else 3
    kern = functools.partial(_mlstm_kernel, seq_len=t, has_state=has_state, emit_state=emit_state,
                             n_carried=n_carried)
    in_specs = [
        pl.BlockSpec((t, W_MLSTM), lambda b: (b, SLAB_Q_MLSTM)),
        pl.BlockSpec((t, W_MLSTM), lambda b: (b, SLAB_K_MLSTM)),
        pl.BlockSpec((t, W_MLSTM), lambda b: (b, SLAB_V_MLSTM)),
        pl.BlockSpec((t, Z_SMALL), lambda b: (b, 0)),
        pl.BlockSpec((16, t), lambda b: (0, b)),
        pl.BlockSpec((None, 3, 2 * W_MLSTM), lambda b: (layer, 0, 0)),
        pl.BlockSpec((None, 1, 2 * W_MLSTM), lambda b: (layer, 0, 0)),
        pl.BlockSpec((None, 1, Z_SMALL), lambda b: (layer, 0, 0)),
        pl.BlockSpec((None, 16, 1), lambda b: (layer, 0, 0)),
    ]
    args = [zb, zb, zb, zs, zst, wconv, bconv, bcol, brow]
    if has_state:
        c0, n0, m0 = states
        in_specs += [
            pl.BlockSpec((None, None, N_DIR, H_MLSTM, dh, dh), lambda b: (b, layer, 0, 0, 0, 0)),
            pl.BlockSpec((None, None, N_DIR, H_MLSTM, dh), lambda b: (b, layer, 0, 0, 0)),
            pl.BlockSpec((None, None, N_DIR, H_MLSTM), lambda b: (b, layer, 0, 0)),
        ]
        args += [c0, n0, m0]
    aliases = {}
    for k in range(n_carried):
        aliases[len(args)] = 1 + k
        in_specs.append(pl.BlockSpec(memory_space=pl.ANY))
        args.append(carried[k])
    out_specs = [pl.BlockSpec((t, W_MLSTM), lambda b: (b, 0))]
    out_shape = [jax.ShapeDtypeStruct((n_seq * t, W_MLSTM), F32)]
    if emit_state:
        out_specs += [
            pl.BlockSpec((None, None, N_DIR, H_MLSTM, dh, dh), lambda b: (b, layer, 0, 0, 0, 0)),
            pl.BlockSpec((None, None, N_DIR, H_MLSTM, dh), lambda b: (b, layer, 0, 0, 0)),
            pl.BlockSpec((None, None, N_DIR, H_MLSTM), lambda b: (b, layer, 0, 0)),
        ]
        out_shape += [
            jax.ShapeDtypeStruct((n_seq, DEPTH, N_DIR, H_MLSTM, dh, dh), F32),
            jax.ShapeDtypeStruct((n_seq, DEPTH, N_DIR, H_MLSTM, dh), F32),
            jax.ShapeDtypeStruct((n_seq, DEPTH, N_DIR, H_MLSTM), F32),
        ]
    scratch = [
        pltpu.VMEM((t, W_MLSTM), BF16),
        pltpu.VMEM((t, W_MLSTM), F32),
        pltpu.VMEM((t + 16, dh), F32),
        pltpu.VMEM((t, Z_SMALL), F32),
        pltpu.VMEM((16, t), F32),
        pltpu.VMEM((N_DIR, H_MLSTM, dh, dh), F32),
        pltpu.VMEM((N_DIR, H_MLSTM, 1, dh), F32),
        pltpu.VMEM((N_DIR, H_MLSTM, 1, LANE), F32),
        pltpu.VMEM((N_DIR, t, Z_SMALL), F32),
        pltpu.VMEM((N_DIR, 16, t), F32),
    ]
    return dict(kernel=kern, grid=(n_seq,), in_specs=in_specs, args=args, out_specs=out_specs,
                out_shape=out_shape, scratch=scratch, aliases=aliases, name="mlstm_scan")


def _head_rms(o, g, n_heads, width):
    out = []
    for h in range(n_heads):
        oh = o[:, width * h:width * (h + 1)]
        out.append(oh * lax.rsqrt(jnp.mean(oh * oh, axis=-1, keepdims=True) + NORM_EPS) * g)
    return out


def _mix_kernel(x_ref, og_ref, hm_ref, rg_ref, om_ref, gg_ref, gm_ref, mod_ref, gng_ref, gnm_ref,
                wbg_ref, wbm_ref, wo_ref, lg_ref, lb_ref, o_ref, *, cond_row0, tiles_per_cond):
    r = _cond_row(pl.program_id(0), cond_row0, tiles_per_cond)
    gate = mod_ref[2, pl.ds(r, 1), :]
    o_g = jnp.concatenate(_head_rms(og_ref[...] * (DK_GLA ** -0.5), gng_ref[...], H_GLA, DV_GLA), axis=1)
    a_g = (o_g * rg_ref[...].astype(F32)).astype(BF16)
    h_m = jnp.concatenate(_head_rms(hm_ref[...], gnm_ref[...], H_MLSTM, DH_MLSTM), axis=1)
    a_m = (om_ref[...].astype(F32) * h_m).astype(BF16)
    y_g = _dot(a_g, wbg_ref[...])
    y_m = _dot(a_m, wbm_ref[...])
    mix = gg_ref[...].astype(F32) * y_g + gm_ref[...].astype(F32) * y_m
    y = _dot(mix.astype(BF16), wo_ref[...])
    o_ref[...] = _layer_norm(ALPHA * x_ref[...] + gate * y, lg_ref[...], lb_ref[...])


def _mix_call(x, og, hm, zb, ada4, gn_gla, gn_ml, wbg, wbm, wo, lng, lnb, *, layer, cond_row0, rows_per_cond):
    m = x.shape[0]
    tm = MIX_TM
    kern = functools.partial(_mix_kernel, cond_row0=cond_row0, tiles_per_cond=rows_per_cond // tm)
    wspec = pl.BlockSpec((None, D_MODEL, D_MODEL), lambda i: (layer, 0, 0), pipeline_mode=pl.Buffered(1))
    return dict(
        kernel=kern, grid=(m // tm,),
        in_specs=[
            pl.BlockSpec((tm, D_MODEL), lambda i: (i, 0)),
            pl.BlockSpec((tm, V_GLA), lambda i: (i, 0)),
            pl.BlockSpec((tm, W_MLSTM), lambda i: (i, 0)),
            pl.BlockSpec((tm, V_GLA), lambda i: (i, SLAB_R_GLA)),
            pl.BlockSpec((tm, W_MLSTM), lambda i: (i, SLAB_O_MLSTM)),
            pl.BlockSpec((tm, D_MODEL), lambda i: (i, SLAB_G_GLA)),
            pl.BlockSpec((tm, D_MODEL), lambda i: (i, SLAB_G_MLSTM)),
            pl.BlockSpec((None, 3, 16, D_MODEL), lambda i: (layer, 1, 0, 0)),
            pl.BlockSpec((None, 1, DV_GLA), lambda i: (layer, 0, 0)),
            pl.BlockSpec((None, 1, DH_MLSTM), lambda i: (layer, 0, 0)),
            wspec, wspec, wspec,
            pl.BlockSpec((None, 1, D_MODEL), lambda i: (3 * layer + 1, 0, 0)),
            pl.BlockSpec((None, 1, D_MODEL), lambda i: (3 * layer + 1, 0, 0)),
        ],
        args=[x, og, hm, zb, zb, zb, zb, ada4, gn_gla, gn_ml, wbg, wbm, wo, lng, lnb],
        out_specs=[pl.BlockSpec((tm, D_MODEL), lambda i: (i, 0))],
        out_shape=[jax.ShapeDtypeStruct((m, D_MODEL), F32)],
        scratch=[], aliases={}, name="mixer_out_ln")


def _grid_pos_embed(t_len):
    rows = t_len // GRID_W
    r = jnp.repeat(jnp.arange(rows), GRID_W).astype(F32)
    col = jnp.tile(jnp.arange(GRID_W), rows).astype(F32)
    nf = D_MODEL // 4
    omega = 1.0 / (10000.0 ** (jnp.arange(nf, dtype=F32) / nf))
    er = r[:, None] * omega
    ec = col[:, None] * omega
    return jnp.concatenate([jnp.sin(er), jnp.cos(er), jnp.sin(ec), jnp.cos(ec)], axis=-1)


def _pack_ffn(w_gate, w_up, w_down):
    return w_gate.astype(BF16), w_up.astype(BF16), w_down.astype(BF16)


def _pack_w_in(w_in):
    o = np.cumsum((0, QK_GLA, QK_GLA, V_GLA, V_GLA, GLA_RANK, GLA_RANK, 2 * W_MLSTM, W_MLSTM, W_MLSTM,
                   H_MLSTM, H_MLSTM, H_MLSTM, H_MLSTM, D_MODEL, D_MODEL))
    o = [int(v) for v in o]
    big = jnp.concatenate([w_in[:, :, o[0]:o[4]], w_in[:, :, o[6]:o[9]], w_in[:, :, o[13]:o[15]]],
                          axis=-1).astype(BF16)
    small = jnp.concatenate([w_in[:, :, o[4]:o[6]], w_in[:, :, o[9]:o[13]]], axis=-1)
    small_p = jnp.pad(small, ((0, 0), (0, 0), (0, Z_SMALL - small.shape[-1]))).astype(BF16)
    return big, small_p


def kernel(x_prompt, x_sample, c, state_gla_s, state_mlstm_c, state_mlstm_n, state_mlstm_m, c_ctx,
           w_ada, b_ada, ffn1_w_gate, ffn1_w_up, ffn1_w_down, w_in, w_decay, b_decay, w_conv, b_conv,
           f_bias, gla_norm_g, mlstm_norm_g, w_br_gla, w_br_mlstm, w_out,
           ffn2_w_gate, ffn2_w_up, ffn2_w_down, ln_g, ln_b):
    bp, tp, _ = x_prompt.shape
    bs, ts, _ = x_sample.shape
    assert bs + 1 <= 16 and tp % FIN_ROWS == 0 and ts % FIN_ROWS == 0

    cond16 = jnp.zeros((16, D_MODEL), F32).at[0].set(c_ctx).at[1:1 + bs].set(c)
    ada4 = _ada_call(cond16, w_ada, b_ada)

    ffn1 = _pack_ffn(ffn1_w_gate, ffn1_w_up, ffn1_w_down)
    ffn2 = _pack_ffn(ffn2_w_gate, ffn2_w_up, ffn2_w_down)
    w_big, w_small = _pack_w_in(w_in)
    wdec = jnp.zeros((DEPTH, N_DIR, Z_SMALL, QK_GLA), F32)
    wdec = wdec.at[:, 0, 0:GLA_RANK].set(w_decay[:, 0]).at[:, 1, GLA_RANK:2 * GLA_RANK].set(w_decay[:, 1])
    bdec = b_decay.reshape(DEPTH, N_DIR, 1, QK_GLA)
    bcol = jnp.zeros((DEPTH, 1, Z_SMALL), F32)
    brow = jnp.zeros((DEPTH, 16, 1), F32)
    for d in range(N_DIR):
        bcol = bcol.at[:, 0, COL_F[d]:COL_F[d] + H_MLSTM].set(f_bias[:, d])
        brow = brow.at[:, ROW_F[d]:ROW_F[d] + H_MLSTM, 0].set(f_bias[:, d])
    bconv = b_conv.reshape(DEPTH, 1, 2 * W_MLSTM)
    gn_gla = gla_norm_g.reshape(DEPTH, 1, DV_GLA)
    gn_ml = mlstm_norm_g.reshape(DEPTH, 1, DH_MLSTM)
    wbg, wbm, wo = w_br_gla.astype(BF16), w_br_mlstm.astype(BF16), w_out.astype(BF16)
    lng = ln_g.reshape(DEPTH * 3, 1, D_MODEL)
    lnb = ln_b.reshape(DEPTH * 3, 1, D_MODEL)
    pos = _grid_pos_embed(ts)

    m_ctx, m_dec = bp * tp, bs * ts
    ctx = dict(cond_row0=0, rows_per_cond=m_ctx)
    dec = dict(cond_row0=1, rows_per_cond=ts)
    dec_states = (state_gla_s, state_mlstm_c, state_mlstm_n, state_mlstm_m)
    fuse = (m_dec == bp * FUSE_TM and ts % FUSE_TM == 0
            and tp // GLA_CHUNK <= MAX_UNROLLED_CHUNKS and tp // MLSTM_CHUNK <= MAX_UNROLLED_CHUNKS)

    def ffn(x, pos_embed, weights, layer, sub, where, **kw):
        return _ffn_call(x, pos_embed, ada4, *weights, lng, lnb, layer=layer, sub=sub, **where, **kw)

    def proj(x, layer, where, **kw):
        return _proj_call(x, ada4, w_big, w_small, layer=layer, **where, **kw)

    def gla(z, layer, state, carried, n_seq, seq_len):
        return _gla_call(z[0], z[1], wdec, bdec, state, carried, layer=layer, n_seq=n_seq, seq_len=seq_len,
                         emit_state=state is None)

    def mlstm(z, layer, states, carried, n_seq, seq_len):
        return _mlstm_call(z[0], z[1], z[2], w_conv, bconv, bcol, brow, states, carried, layer=layer,
                           n_seq=n_seq, seq_len=seq_len, emit_state=states is None)

    def mix(x, og, hm, z, layer, where):
        return _run(_mix_call(x, og, hm, z[0], ada4, gn_gla, gn_ml, wbg, wbm, wo, lng, lnb,
                              layer=layer, **where))[0]

    xc = x_prompt.reshape(m_ctx, D_MODEL)
    xd = x_sample.reshape(m_dec, D_MODEL)
    carried = None
    for l in range(DEPTH):
        xc = _run(ffn(xc, None, ffn1, l, 0, ctx))[0]
        zc = _run(proj(xc, l, ctx))
        ml_c = mlstm(zc, l, None, None if carried is None else carried[1:], bp, tp)
        gl_c = gla(zc, l, None, None if carried is None else carried[0], bp, tp)
        pos_l = pos if l == 0 else None
        if fuse:
            ml_out, (xd,) = _run_fused(ml_c, ffn(xd, pos_l, ffn1, l, 0, dec, tm=FUSE_TM), "mlstm_scan_ffn")
            gl_out, zd = _run_fused(gl_c, proj(xd, l, dec, tm=FUSE_TM), "gla_scan_in_proj")
        else:
            ml_out, gl_out = _run(ml_c), _run(gl_c)
            xd = _run(ffn(xd, pos_l, ffn1, l, 0, dec))[0]
            zd = _run(proj(xd, l, dec))
        carried = (gl_out[1], ml_out[1], ml_out[2], ml_out[3])
        xc = mix(xc, gl_out[0], ml_out[0], zc, l, ctx)
        xc = _run(ffn(xc, None, ffn2, l, 2, ctx))[0]

        og_d = _run(gla(zd, l, dec_states[0], None, bs, ts))[0]
        hm_d = _run(mlstm(zd, l, dec_states[1:], None, bs, ts))[0]
        xd = mix(xd, og_d, hm_d, zd, l, dec)
        xd = _run(ffn(xd, None, ffn2, l, 2, dec))[0]

    return (xc.reshape(bp, tp, D_MODEL), xd.reshape(bs, ts, D_MODEL)) + tuple(carried)
```

```python
import functools

import jax
import jax.numpy as jnp
import numpy as np
from jax import lax
from jax.experimental import pallas as pl
from jax.experimental.pallas import tpu as pltpu

F32 = jnp.float32
BF16 = jnp.bfloat16

D_MODEL = 1024
DEPTH = 2
GRID_W = 64
N_DIR = 2
H_GLA = 4
DK_GLA = 128
DV_GLA = 256
GLA_RANK = 16
GLA_TAU = 16.0
H_MLSTM = 4
DH_MLSTM = 256
D_FF = 2816
N_MOD = 9
LN_EPS = 1e-5
NORM_EPS = 1e-6
QK_GLA = H_GLA * DK_GLA
V_GLA = H_GLA * DV_GLA
W_MLSTM = H_MLSTM * DH_MLSTM
ALPHA = (2.0 * DEPTH) ** 0.25

Z_BIG = 9 * 1024
Z_SMALL = 128
SLAB_V_GLA, SLAB_R_GLA, SLAB_Q_MLSTM, SLAB_K_MLSTM, SLAB_V_MLSTM = 1, 2, 3, 4, 5
SLAB_O_MLSTM, SLAB_G_GLA, SLAB_G_MLSTM = 6, 7, 8
W_SPLIT = (3, 7)
COL_I = (32, 40)
COL_F = (36, 44)
ROW_I = (0, 8)
ROW_F = (4, 12)

LANE = 128
SUBLANE = 8
VMEM_LIMIT = 56 * 1024 * 1024

FFN_TM = 1024
FFN_SUB = 512
FFN_CHUNKS = ((0, 1536), (1536, 1280))
PROJ_TM = 512
PROJ_TN = 1024
MIX_TM = 512
FUSE_TM = 256
GLA_CHUNK = 64
GLA_SUB = 16
MLSTM_CHUNK = 128
FIN_ROWS = 256
MAX_UNROLLED_CHUNKS = 4


def _dot(a, b):
    return jnp.dot(a, b, preferred_element_type=F32)


def _dot_nt(a, b):
    return lax.dot_general(a, b, (((1,), (1,)), ((), ())), preferred_element_type=F32)


def _dot_tn(a, b):
    return lax.dot_general(a, b, (((0,), (0,)), ((), ())), preferred_element_type=F32)


def _split3(x):
    hi = x.astype(BF16)
    r = x - hi.astype(F32)
    mid = r.astype(BF16)
    lo = (r - mid.astype(F32)).astype(BF16)
    return hi, mid, lo


def _split2(x):
    hi = x.astype(BF16)
    return hi, (x - hi.astype(F32)).astype(BF16)


def _dot_acc(x, w):
    xh, xm, _ = _split3(x)
    wh, wl = _split2(w)
    return _dot(xh, wh) + _dot(xh, wl) + _dot(xm, wh)


def _silu(x):
    return x * jax.nn.sigmoid(x)


def _log_sigmoid(x):
    return jnp.minimum(x, 0.0) - jnp.log(1.0 + jnp.exp(-jnp.abs(x)))


def _layer_norm(y, g, b):
    mu = jnp.mean(y, axis=-1, keepdims=True)
    d = y - mu
    var = jnp.mean(d * d, axis=-1, keepdims=True)
    return d * lax.rsqrt(var + LN_EPS) * g + b


def _aligned(x, m):
    return x if isinstance(x, int) else pl.multiple_of(x, m)


def _cparams(n_axes):
    return pltpu.CompilerParams(
        dimension_semantics=("arbitrary",) * n_axes, vmem_limit_bytes=VMEM_LIMIT)


def _cond_row(i, cond_row0, tiles_per_cond):
    return cond_row0 + lax.div(i, jnp.int32(tiles_per_cond))


def _run(call):
    return pl.pallas_call(
        call["kernel"],
        grid=call["grid"],
        in_specs=call["in_specs"],
        out_specs=call["out_specs"],
        out_shape=call["out_shape"],
        scratch_shapes=call["scratch"],
        input_output_aliases=call["aliases"],
        compiler_params=_cparams(len(call["grid"])),
        name=call["name"],
    )(*call["args"])


def _run_fused(scan, mm, name):
    assert scan["grid"] == mm["grid"] and not mm["aliases"]
    counts = [len(scan["args"]), len(mm["args"]), len(scan["out_shape"]), len(mm["out_shape"]),
              len(scan["scratch"]), len(mm["scratch"])]

    def kern(*refs):
        parts, i = [], 0
        for n in counts:
            parts.append(tuple(refs[i:i + n]))
            i += n
        in_a, in_b, out_a, out_b, scr_a, scr_b = parts
        scan["kernel"](*(in_a + out_a + scr_a), hooks=mm["pieces"](in_b + out_b + scr_b))

    outs = pl.pallas_call(
        kern,
        grid=scan["grid"],
        in_specs=scan["in_specs"] + mm["in_specs"],
        out_specs=scan["out_specs"] + mm["out_specs"],
        out_shape=scan["out_shape"] + mm["out_shape"],
        scratch_shapes=scan["scratch"] + mm["scratch"],
        input_output_aliases=scan["aliases"],
        compiler_params=_cparams(len(scan["grid"])),
        name=name,
    )(*scan["args"], *mm["args"])
    n_a = counts[2]
    return outs[:n_a], outs[n_a:]


def _hook_runner(hooks):
    pending = list(hooks)

    def hook(n=1):
        for _ in range(n):
            if pending:
                pending.pop(0)()

    def flush():
        while pending:
            pending.pop(0)()

    return hook, flush


def _ada_kernel(c_ref, w_ref, b_ref, o_ref):
    o_ref[...] = _dot_acc(_silu(c_ref[...]), w_ref[...]) + b_ref[...]


def _ada_call(cond16, w_ada, b_ada):
    return pl.pallas_call(
        _ada_kernel,
        grid=(DEPTH, N_MOD),
        in_specs=[
            pl.BlockSpec((16, D_MODEL), lambda l, j: (0, 0)),
            pl.BlockSpec((None, D_MODEL, D_MODEL), lambda l, j: (l, 0, j)),
            pl.BlockSpec((None, None, 1, D_MODEL), lambda l, j: (l, j, 0, 0)),
        ],
        out_specs=pl.BlockSpec((None, None, 16, D_MODEL), lambda l, j: (l, j, 0, 0)),
        out_shape=jax.ShapeDtypeStruct((DEPTH, N_MOD, 16, D_MODEL), F32),
        compiler_params=_cparams(2),
        name="ada_mod",
    )(cond16, w_ada, b_ada.reshape(DEPTH, N_MOD, 1, D_MODEL))


def _ffn_pieces(refs, *, has_pos, cond_row0, tiles_per_cond, tm, sub):
    if has_pos:
        x_ref, pos_ref, mod_ref, wg_ref, wu_ref, wd_ref, lg_ref, lb_ref, o_ref = refs
    else:
        x_ref, mod_ref, wg_ref, wu_ref, wd_ref, lg_ref, lb_ref, o_ref = refs
        pos_ref = None
    st = {}
    pieces = []

    def start(s):
        rows = pl.ds(s * sub, sub)
        if s == 0:
            r = _cond_row(pl.program_id(0), cond_row0, tiles_per_cond)
            st["sh"] = mod_ref[0, pl.ds(r, 1), :]
            st["sc"] = mod_ref[1, pl.ds(r, 1), :]
            st["gate"] = mod_ref[2, pl.ds(r, 1), :]
        x = x_ref[rows, :]
        if has_pos:
            x = x + pos_ref[rows, :]
        st["x"] = x
        st["hm"] = (x * (1.0 + st["sc"]) + st["sh"]).astype(BF16)
        st["acc"] = None

    def gate_mm(c0, cw):
        st["g"] = _dot(st["hm"], wg_ref[:, c0:c0 + cw])

    def up_mm(c0, cw):
        st["u"] = _dot(st["hm"], wu_ref[:, c0:c0 + cw])

    def down_mm(c0, cw):
        part = _dot((_silu(st["g"]) * st["u"]).astype(BF16), wd_ref[c0:c0 + cw, :])
        st["acc"] = part if st["acc"] is None else st["acc"] + part

    def finish(s):
        y = ALPHA * st["x"] + (0.5 * st["gate"]) * st["acc"]
        o_ref[pl.ds(s * sub, sub), :] = _layer_norm(y, lg_ref[...], lb_ref[...])

    for s in range(tm // sub):
        pieces.append(functools.partial(start, s))
        for c0, cw in FFN_CHUNKS:
            pieces += [functools.partial(gate_mm, c0, cw), functools.partial(up_mm, c0, cw),
                       functools.partial(down_mm, c0, cw)]
        pieces.append(functools.partial(finish, s))
    return pieces


def _ffn_call(x, pos, ada4, wg, wu, wd, lng, lnb, *, layer, sub, cond_row0, rows_per_cond, tm=FFN_TM):
    m = x.shape[0]
    has_pos = pos is not None
    static = dict(has_pos=has_pos, cond_row0=cond_row0, tiles_per_cond=rows_per_cond // tm,
                  tm=tm, sub=min(tm, FFN_SUB))
    pieces = functools.partial(_ffn_pieces, **static)

    def kern(*refs):
        for p in pieces(refs):
            p()

    once = pl.Buffered(1)
    in_specs = [pl.BlockSpec((tm, D_MODEL), lambda i: (i, 0))]
    args = [x]
    if has_pos:
        n_pos = pos.shape[0] // tm
        if n_pos == 1:
            in_specs.append(pl.BlockSpec((tm, D_MODEL), lambda i: (0, 0), pipeline_mode=once))
        else:
            in_specs.append(pl.BlockSpec((tm, D_MODEL), lambda i: (lax.rem(i, jnp.int32(n_pos)), 0)))
        args.append(pos)
    in_specs += [
        pl.BlockSpec((None, 3, 16, D_MODEL), lambda i: (layer, sub, 0, 0)),
        pl.BlockSpec((None, D_MODEL, D_FF), lambda i: (layer, 0, 0), pipeline_mode=once),
        pl.BlockSpec((None, D_MODEL, D_FF), lambda i: (layer, 0, 0), pipeline_mode=once),
        pl.BlockSpec((None, D_FF, D_MODEL), lambda i: (layer, 0, 0), pipeline_mode=once),
        pl.BlockSpec((None, 1, D_MODEL), lambda i: (3 * layer + sub, 0, 0)),
        pl.BlockSpec((None, 1, D_MODEL), lambda i: (3 * layer + sub, 0, 0)),
    ]
    args += [ada4, wg, wu, wd, lng, lnb]
    return dict(kernel=kern, pieces=pieces, grid=(m // tm,), in_specs=in_specs, args=args,
                out_specs=[pl.BlockSpec((tm, D_MODEL), lambda i: (i, 0))],
                out_shape=[jax.ShapeDtypeStruct((m, D_MODEL), F32)],
                scratch=[], aliases={}, name="ffn_ln")


def _proj_pieces(refs, *, cond_row0, tiles_per_cond):
    x_ref, mod_ref, wa_ref, wb_ref, wc_ref, ws_ref, zb_ref, zs_ref, zst_ref = refs
    st = {}

    def weight(n):
        for ref, first in ((wa_ref, 0), (wb_ref, W_SPLIT[0]), (wc_ref, W_SPLIT[1])):
            local = n - first
            if 0 <= local * PROJ_TN < ref.shape[1]:
                return ref[:, local * PROJ_TN:(local + 1) * PROJ_TN]
        raise ValueError(n)

    def start():
        r = _cond_row(pl.program_id(0), cond_row0, tiles_per_cond)
        sh = mod_ref[0, pl.ds(r, 1), :]
        sc = mod_ref[1, pl.ds(r, 1), :]
        st["hm"] = (x_ref[...] * (1.0 + sc) + sh).astype(BF16)
        zs = _dot(st["hm"], ws_ref[...])
        zs_ref[...] = zs
        zst_ref[...] = zs.T[COL_I[0]:COL_I[0] + 16, :]

    def slab(n):
        cols = slice(n * PROJ_TN, (n + 1) * PROJ_TN)
        z = _dot(st["hm"], weight(n))
        if n == SLAB_R_GLA:
            z = _silu(z)
        elif n in (SLAB_O_MLSTM, SLAB_G_GLA, SLAB_G_MLSTM):
            z = jax.nn.sigmoid(z)
        zb_ref[:, cols] = z.astype(BF16)

    return [start] + [functools.partial(slab, n) for n in range(Z_BIG // PROJ_TN)]


def _proj_call(x, ada4, w_big, w_small, *, layer, cond_row0, rows_per_cond, tm=PROJ_TM):
    m = x.shape[0]
    w_a, w_b, w_c = w_big
    assert (W_SPLIT[0] * PROJ_TN + w_b.shape[-1] + w_c.shape[-1] == Z_BIG
            and w_b.shape[-1] == (W_SPLIT[1] - W_SPLIT[0]) * PROJ_TN)
    pieces = functools.partial(_proj_pieces, cond_row0=cond_row0, tiles_per_cond=rows_per_cond // tm)

    def kern(*refs):
        for p in pieces(refs):
            p()

    once = pl.Buffered(1)
    return dict(
        kernel=kern, pieces=pieces, grid=(m // tm,),
        in_specs=[
            pl.BlockSpec((tm, D_MODEL), lambda i: (i, 0)),
            pl.BlockSpec((None, 3, 16, D_MODEL), lambda i: (layer, 1, 0, 0)),
            pl.BlockSpec((None, D_MODEL, W_SPLIT[0] * PROJ_TN), lambda i: (layer, 0, 0), pipeline_mode=once),
            pl.BlockSpec((None, D_MODEL, w_b.shape[-1]), lambda i: (layer, 0, 0), pipeline_mode=once),
            pl.BlockSpec((None, D_MODEL, w_c.shape[-1]), lambda i: (layer, 0, 0), pipeline_mode=once),
            pl.BlockSpec((None, D_MODEL, Z_SMALL), lambda i: (layer, 0, 0), pipeline_mode=once),
        ],
        args=[x, ada4, w_a, w_b, w_c, w_small],
        out_specs=[
            pl.BlockSpec((tm, Z_BIG), lambda i: (i, 0)),
            pl.BlockSpec((tm, Z_SMALL), lambda i: (i, 0)),
            pl.BlockSpec((16, tm), lambda i: (0, i)),
        ],
        out_shape=[
            jax.ShapeDtypeStruct((m, Z_BIG), BF16),
            jax.ShapeDtypeStruct((m, Z_SMALL), F32),
            jax.ShapeDtypeStruct((16, m), F32),
        ],
        scratch=[], aliases={}, name="mixer_in_proj")


def _gla_kernel(*refs, seq_len, has_state, emit_state, n_carried, hooks=()):
    it = iter(refs)
    q_ref, k_ref, v_ref, zs_ref, wdec_ref, bdec_ref = (next(it) for _ in range(6))
    s0_ref = next(it) if has_state else None
    if n_carried:
        next(it)
    oacc_s = next(it)
    sout_ref = next(it) if emit_state else None
    la_s, st_s, cum_s = (next(it) for _ in range(3))
    t_len = seq_len
    c_len = GLA_CHUNK
    n_chunks = t_len // c_len
    unrolled = n_chunks <= MAX_UNROLLED_CHUNKS
    assert unrolled or not hooks
    hook, flush = _hook_runner(hooks)
    hook(4)

    for rb in range(t_len // FIN_ROWS):
        rows = pl.ds(rb * FIN_ROWS, FIN_ROWS)
        zh, zm, _ = _split3(zs_ref[rows, :])
        for d in range(N_DIR):
            wh, wl = _split2(wdec_ref[d])
            x = _dot(zh, wh) + _dot(zh, wl) + _dot(zm, wh) + bdec_ref[d]
            la_s[d, rows, :] = _log_sigmoid(x) * (1.0 / GLA_TAU)

    for d in range(N_DIR):
        for h in range(H_GLA):
            if has_state:
                st_s[d, h] = s0_ref[d, h].T
            else:
                st_s[d, h] = jnp.zeros((DV_GLA, DK_GLA), F32)
    oacc_s[...] = jnp.zeros_like(oacc_s)

    ri = lax.broadcasted_iota(jnp.int32, (c_len, c_len), 0)
    ci = lax.broadcasted_iota(jnp.int32, (c_len, c_len), 1)
    rr = lax.broadcasted_iota(jnp.int32, (c_len, DK_GLA), 0)
    odd_rows = (lax.shift_right_logical(rr, 4) & 1) == 1
    upper_rows = rr >= 2 * GLA_SUB
    same_sub = lax.shift_right_logical(ri, 4) == lax.shift_right_logical(ci, 4)
    same_half = lax.shift_right_logical(ri, 5) == lax.shift_right_logical(ci, 5)
    causal = (ri >= ci, ri <= ci)
    tri = tuple(jnp.where(c, 1.0, 0.0).astype(BF16) for c in causal)
    pm0 = tuple(same_sub & c for c in causal)
    q1rows = (odd_rows, jnp.logical_not(odd_rows))
    q2rows = (upper_rows, jnp.logical_not(upper_rows))

    def bcast(x, n):
        return jnp.broadcast_to(x, (n, DK_GLA))

    for d in range(N_DIR):
        for cb in range(n_chunks):
            rows = pl.ds(cb * c_len, c_len)
            l3 = _split3(la_s[d, rows, :])
            cum_s[d, rows, :] = _dot(tri[d], l3[0]) + _dot(tri[d], l3[1]) + _dot(tri[d], l3[2])
    hook()

    def scores(d, h, row0):
        rows = pl.ds(row0, c_len)
        ls = slice(DK_GLA * h, DK_GLA * (h + 1))
        vs = slice(DV_GLA * h, DV_GLA * (h + 1))

        def row(i):
            g = i - i % SUBLANE
            return cum_s[d, pl.ds(_aligned(row0 + g, SUBLANE), SUBLANE), ls][i - g:i - g + 1, :]

        c = cum_s[d, rows, ls]
        zero = jnp.zeros((GLA_SUB, DK_GLA), F32)
        if d == 0:
            ref0 = jnp.concatenate([zero, bcast(row(15), 16), bcast(row(31), 16), bcast(row(47), 16)], axis=0)
            ref1 = jnp.concatenate([bcast(row(15), 32), bcast(row(47), 32)], axis=0)
            ref2, cend = row(31), row(63)
        else:
            ref0 = jnp.concatenate([bcast(row(16), 16), bcast(row(32), 16), bcast(row(48), 16), zero], axis=0)
            ref1 = jnp.concatenate([bcast(row(16), 32), bcast(row(48), 32)], axis=0)
            ref2, cend = row(32), row(0)
        q = q_ref[rows, ls].astype(F32)
        k = k_ref[rows, ls].astype(F32)
        e0 = c - ref0
        p0 = _dot_nt((q * jnp.exp(e0)).astype(BF16), (k * jnp.exp(-e0)).astype(BF16))
        x1 = jnp.exp(-jnp.abs(c - ref1))
        p1 = _dot_nt(jnp.where(q1rows[d], q * x1, 0.0).astype(BF16),
                     jnp.where(q1rows[d], 0.0, k * x1).astype(BF16))
        x2 = jnp.exp(-jnp.abs(c - ref2))
        p2 = _dot_nt(jnp.where(q2rows[d], q * x2, 0.0).astype(BF16),
                     jnp.where(q2rows[d], 0.0, k * x2).astype(BF16))
        qi = (q * jnp.exp(c)).astype(BF16)
        kst = (k * jnp.exp(cend - c)).astype(BF16)
        return dict(d=d, h=h, rows=rows, vs=vs, p0=p0, p1=p1, p2=p2, qi=qi, kst=kst, dec=jnp.exp(cend))

    def body(j, carry):
        row0 = (_aligned(j * c_len, c_len), _aligned((n_chunks - 1 - j) * c_len, c_len))
        work = [scores(d, h, row0[d]) for d in range(N_DIR) for h in range(H_GLA)]
        hook()
        for w in work:
            d, h = w["d"], w["h"]
            p = jnp.where(pm0[d], w["p0"], jnp.where(same_half, w["p1"], w["p2"]))
            o = _dot(p.astype(BF16), v_ref[w["rows"], w["vs"]]) + _dot_nt(w["qi"], st_s[d, h].astype(BF16))
            oacc_s[w["rows"], w["vs"]] += o
        hook()
        for w in work:
            d, h = w["d"], w["h"]
            st_s[d, h] = st_s[d, h] * w["dec"] + _dot_tn(v_ref[w["rows"], w["vs"]], w["kst"])
        return carry

    if unrolled:
        for j in range(n_chunks):
            body(j, 0)
    else:
        lax.fori_loop(0, n_chunks, body, 0)
    flush()

    if emit_state:
        for d in range(N_DIR):
            for h in range(H_GLA):
                sout_ref[d, h] = st_s[d, h].T


def _gla_call(zb, zs, wdec, bdec, state, carried, *, layer, n_seq, seq_len, emit_state):
    has_state = state is not None
    t = seq_len
    n_carried = 0 if carried is None else 1
    kern = functools.partial(_gla_kernel, seq_len=t, has_state=has_state, emit_state=emit_state,
                             n_carried=n_carried)
    in_specs = [
        pl.BlockSpec((t, QK_GLA), lambda b: (b, 0)),
        pl.BlockSpec((t, QK_GLA), lambda b: (b, 1)),
        pl.BlockSpec((t, V_GLA), lambda b: (b, SLAB_V_GLA)),
        pl.BlockSpec((t, Z_SMALL), lambda b: (b, 0)),
        pl.BlockSpec((None, N_DIR, Z_SMALL, QK_GLA), lambda b: (layer, 0, 0, 0)),
        pl.BlockSpec((None, N_DIR, 1, QK_GLA), lambda b: (layer, 0, 0, 0)),
    ]
    args = [zb, zb, zb, zs, wdec, bdec]
    if has_state:
        in_specs.append(pl.BlockSpec((None, None, N_DIR, H_GLA, DK_GLA, DV_GLA),
                                     lambda b: (b, layer, 0, 0, 0, 0)))
        args.append(state)
    aliases = {}
    if n_carried:
        aliases[len(args)] = 1
        in_specs.append(pl.BlockSpec(memory_space=pl.ANY))
        args.append(carried)
    out_specs = [pl.BlockSpec((t, V_GLA), lambda b: (b, 0))]
    out_shape = [jax.ShapeDtypeStruct((n_seq * t, V_GLA), F32)]
    if emit_state:
        out_specs.append(pl.BlockSpec((None, None, N_DIR, H_GLA, DK_GLA, DV_GLA),
                                      lambda b: (b, layer, 0, 0, 0, 0)))
        out_shape.append(jax.ShapeDtypeStruct((n_seq, DEPTH, N_DIR, H_GLA, DK_GLA, DV_GLA), F32))
    scratch = [
        pltpu.VMEM((N_DIR, t, QK_GLA), F32),
        pltpu.VMEM((N_DIR, H_GLA, DV_GLA, DK_GLA), F32),
        pltpu.VMEM((N_DIR, t, QK_GLA), F32),
    ]
    return dict(kernel=kern, grid=(n_seq,), in_specs=in_specs, args=args, out_specs=out_specs,
                out_shape=out_shape, scratch=scratch, aliases=aliases, name="gla_scan")


def _mlstm_kernel(*refs, seq_len, has_state, emit_state, n_carried, reduce_on_mxu, hooks=()):
    it = iter(refs)
    (q_ref, k_ref, v_ref, zs_ref, zst_ref, wconv_ref, bconv_ref,
     bcol_ref, brow_ref) = (next(it) for _ in range(9))
    if has_state:
        c0_ref, n0_ref, m0_ref = (next(it) for _ in range(3))
    for _ in range(n_carried):
        next(it)
    hacc_s = next(it)
    if emit_state:
        cout_ref, nout_ref, mout_ref = (next(it) for _ in range(3))
    (qc_s, kc_s, xp_s, lf_s, lft_s, c_s, n_s, m_s, cumc_s, cumr_s) = (next(it) for _ in range(10))
    t_len = seq_len
    c_len = MLSTM_CHUNK
    n_chunks = t_len // c_len
    dh = DH_MLSTM
    unrolled = n_chunks <= MAX_UNROLLED_CHUNKS
    assert unrolled or not hooks
    hook, flush = _hook_runner(hooks)
    hook(6)

    xp_s[pl.ds(0, 8), :] = jnp.zeros((8, dh), F32)
    xp_s[pl.ds(8 + t_len, 8), :] = jnp.zeros((8, dh), F32)
    for which, src in enumerate((q_ref, k_ref)):
        for h in range(H_MLSTM):
            hs = slice(dh * h, dh * (h + 1))
            ws = slice(which * W_MLSTM + dh * h, which * W_MLSTM + dh * (h + 1))
            xp_s[pl.ds(8, t_len), :] = src[:, hs].astype(F32)
            w0 = wconv_ref[pl.ds(0, 1), ws]
            w1 = wconv_ref[pl.ds(1, 1), ws]
            w2 = wconv_ref[pl.ds(2, 1), ws]
            bias = bconv_ref[:, ws]
            for rb in range(t_len // FIN_ROWS):
                r0 = rb * FIN_ROWS
                y = (w0 * xp_s[pl.ds(r0 + 7, FIN_ROWS), :] + w1 * xp_s[pl.ds(r0 + 8, FIN_ROWS), :]
                     + w2 * xp_s[pl.ds(r0 + 9, FIN_ROWS), :] + bias)
                a = _silu(y)
                if which == 0:
                    qc_s[pl.ds(r0, FIN_ROWS), hs] = (a * (dh ** -0.5)).astype(BF16)
                else:
                    kc_s[pl.ds(r0, FIN_ROWS), hs] = a

    lf_s[...] = _log_sigmoid(zs_ref[...] + bcol_ref[...])
    zt = zst_ref[...]
    trow = lax.broadcasted_iota(jnp.int32, zt.shape, 0)
    is_f = (lax.shift_right_logical(trow, 2) & 1) == 1
    lft_s[...] = jnp.where(is_f, _log_sigmoid(zt + brow_ref[...]), zt)

    for d in range(N_DIR):
        for h in range(H_MLSTM):
            if has_state:
                c_s[d, h] = c0_ref[d, h]
                n_s[d, h] = n0_ref[d, pl.ds(h, 1), :]
                m_s[d, h] = jnp.broadcast_to(m0_ref[pl.ds(d, 1), pl.ds(h, 1)], (1, LANE))
            else:
                c_s[d, h] = jnp.zeros((dh, dh), F32)
                n_s[d, h] = jnp.zeros((1, dh), F32)
                m_s[d, h] = jnp.zeros((1, LANE), F32)
    hacc_s[...] = jnp.zeros_like(hacc_s)

    ri = lax.broadcasted_iota(jnp.int32, (c_len, c_len), 0)
    ci = lax.broadcasted_iota(jnp.int32, (c_len, c_len), 1)
    causal = (ri >= ci, ri <= ci)
    tri_c = tuple(jnp.where(c, 1.0, 0.0).astype(BF16) for c in causal)
    tri_r = (tri_c[1], tri_c[0])

    for d in range(N_DIR):
        for cb in range(n_chunks):
            rows = pl.ds(cb * c_len, c_len)
            l3 = _split3(lf_s[rows, :])
            cumc_s[d, rows, :] = _dot(tri_c[d], l3[0]) + _dot(tri_c[d], l3[1]) + _dot(tri_c[d], l3[2])
            r3 = _split3(lft_s[:, rows])
            cumr_s[d, :, rows] = _dot(r3[0], tri_r[d]) + _dot(r3[1], tri_r[d]) + _dot(r3[2], tri_r[d])

    ones_c = jnp.ones((c_len, LANE), BF16)

    def lanes2(x):
        return jnp.concatenate([x, x], axis=1)

    def matmuls(d, h, row0):
        rows = pl.ds(row0, c_len)
        hs = slice(dh * h, dh * (h + 1))
        q = qc_s[rows, hs]
        s_raw = _dot_nt(q, kc_s[rows, hs].astype(BF16))
        qc = _dot(q, c_s[d, h].astype(BF16))
        if reduce_on_mxu:
            n_rep = jnp.broadcast_to(n_s[d, h], (LANE, dh)).astype(BF16)
            qn = _dot_nt(q, n_rep)
        else:
            qn = jnp.sum(q.astype(F32) * n_s[d, h], axis=1, keepdims=True)
        return dict(d=d, h=h, rows=rows, row0=row0, hs=hs, s_raw=s_raw, qc=qc, qn=qn)

    def gates(w):
        d, h, rows, row0 = w["d"], w["h"], w["rows"], w["row0"]
        cf = COL_F[d] + h
        end = c_len - 1 if d == 0 else 0
        cum_col = cumc_s[d, rows, pl.ds(cf, 1)]
        cum_c = jnp.broadcast_to(cum_col, (c_len, LANE))
        b_c = jnp.broadcast_to(zs_ref[rows, pl.ds(COL_I[d] + h, 1)] - cum_col, (c_len, LANE))
        g_end = end - end % SUBLANE
        cum_end = cumc_s[d, pl.ds(_aligned(row0 + g_end, SUBLANE), SUBLANE), pl.ds(cf, 1)][
            end - g_end:end - g_end + 1, :]
        b_r = lft_s[pl.ds(ROW_I[d] + h, 1), rows] - cumr_s[d, pl.ds(ROW_F[d] + h, 1), rows]
        m_prev = m_s[d, h]
        b_m = jnp.where(causal[d], b_r, -jnp.inf)
        m_rel = jnp.maximum(m_prev, jnp.broadcast_to(jnp.max(b_m, axis=1, keepdims=True), (c_len, LANE)))
        w["dmat"] = jnp.exp(b_m - m_rel)
        w["inter"] = jnp.exp(m_prev - m_rel)
        w["floor"] = jnp.exp(-(cum_c + m_rel))
        log_w = cum_end + b_c
        m_new = jnp.maximum(cum_end + m_prev, jnp.max(log_w, axis=0, keepdims=True))
        w["wgt"] = jnp.exp(log_w - m_new)
        w["decay"] = jnp.exp(cum_end + m_prev - m_new)
        w["m_new"] = m_new

    def body(j, carry):
        row0 = (_aligned(j * c_len, c_len), _aligned((n_chunks - 1 - j) * c_len, c_len))
        work = [matmuls(d, h, row0[d]) for d in range(N_DIR) for h in range(H_MLSTM)]
        hook()
        for w in work:
            gates(w)
        for w in work:
            s = w["s_raw"] * w["dmat"]
            s_hi = s.astype(BF16)
            w["sv"] = _dot(s_hi, v_ref[w["rows"], w["hs"]])
            if reduce_on_mxu:
                s_lo = (s - s_hi.astype(F32)).astype(BF16)
                w["rsum"] = _dot(s_hi, ones_c) + _dot(s_lo, ones_c)
            else:
                w["rsum"] = jnp.sum(s, axis=1, keepdims=True)
        hook()
        for w in work:
            den = w["rsum"] + w["inter"] * w["qn"]
            rn = 1.0 / jnp.maximum(jnp.abs(den), w["floor"])
            hacc_s[w["rows"], w["hs"]] += w["sv"] * lanes2(rn) + w["qc"] * lanes2(w["inter"] * rn)
        for w in work:
            d, h = w["d"], w["h"]
            kw = kc_s[w["rows"], w["hs"]] * lanes2(w["wgt"])
            decay = w["decay"]
            c_s[d, h] = decay[:, :1] * c_s[d, h] + _dot_tn(kw.astype(BF16), v_ref[w["rows"], w["hs"]])
            n_s[d, h] = lanes2(decay) * n_s[d, h] + jnp.sum(kw, axis=0, keepdims=True)
            m_s[d, h] = w["m_new"]
        return carry

    if unrolled:
        for j in range(n_chunks):
            body(j, 0)
    else:
        lax.fori_loop(0, n_chunks, body, 0)
    flush()

    if emit_state:
        for d in range(N_DIR):
            for h in range(H_MLSTM):
                cout_ref[d, h] = c_s[d, h]
                nout_ref[d, pl.ds(h, 1), :] = n_s[d, h]
                mout_ref[pl.ds(d, 1), pl.ds(h, 1)] = m_s[d, h][:, :1]


def _mlstm_call(zb, zs, zst, wconv, bconv, bcol, brow, states, carried, *, layer, n_seq, seq_len,
                emit_state, reduce_on_mxu):
    has_state = states is not None
    t = seq_len
    dh = DH_MLSTM
    n_carried = 0 if carried is None else 3
    kern = functools.partial(_mlstm_kernel, seq_len=t, has_state=has_state, emit_state=emit_state,
                             n_carried=n_carried, reduce_on_mxu=reduce_on_mxu)
    in_specs = [
        pl.BlockSpec((t, W_MLSTM), lambda b: (b, SLAB_Q_MLSTM)),
        pl.BlockSpec((t, W_MLSTM), lambda b: (b, SLAB_K_MLSTM)),
        pl.BlockSpec((t, W_MLSTM), lambda b: (b, SLAB_V_MLSTM)),
        pl.BlockSpec((t, Z_SMALL), lambda b: (b, 0)),
        pl.BlockSpec((16, t), lambda b: (0, b)),
        pl.BlockSpec((None, 3, 2 * W_MLSTM), lambda b: (layer, 0, 0)),
        pl.BlockSpec((None, 1, 2 * W_MLSTM), lambda b: (layer, 0, 0)),
        pl.BlockSpec((None, 1, Z_SMALL), lambda b: (layer, 0, 0)),
        pl.BlockSpec((None, 16, 1), lambda b: (layer, 0, 0)),
    ]
    args = [zb, zb, zb, zs, zst, wconv, bconv, bcol, brow]
    if has_state:
        c0, n0, m0 = states
        in_specs += [
            pl.BlockSpec((None, None, N_DIR, H_MLSTM, dh, dh), lambda b: (b, layer, 0, 0, 0, 0)),
            pl.BlockSpec((None, None, N_DIR, H_MLSTM, dh), lambda b: (b, layer, 0, 0, 0)),
            pl.BlockSpec((None, None, N_DIR, H_MLSTM), lambda b: (b, layer, 0, 0)),
        ]
        args += [c0, n0, m0]
    aliases = {}
    for k in range(n_carried):
        aliases[len(args)] = 1 + k
        in_specs.append(pl.BlockSpec(memory_space=pl.ANY))
        args.append(carried[k])
    out_specs = [pl.BlockSpec((t, W_MLSTM), lambda b: (b, 0))]
    out_shape = [jax.ShapeDtypeStruct((n_seq * t, W_MLSTM), F32)]
    if emit_state:
        out_specs += [
            pl.BlockSpec((None, None, N_DIR, H_MLSTM, dh, dh), lambda b: (b, layer, 0, 0, 0, 0)),
            pl.BlockSpec((None, None, N_DIR, H_MLSTM, dh), lambda b: (b, layer, 0, 0, 0)),
            pl.BlockSpec((None, None, N_DIR, H_MLSTM), lambda b: (b, layer, 0, 0)),
        ]
        out_shape += [
            jax.ShapeDtypeStruct((n_seq, DEPTH, N_DIR, H_MLSTM, dh, dh), F32),
            jax.ShapeDtypeStruct((n_seq, DEPTH, N_DIR, H_MLSTM, dh), F32),
            jax.ShapeDtypeStruct((n_seq, DEPTH, N_DIR, H_MLSTM), F32),
        ]
    scratch = [
        pltpu.VMEM((t, W_MLSTM), BF16),
        pltpu.VMEM((t, W_MLSTM), F32),
        pltpu.VMEM((t + 16, dh), F32),
        pltpu.VMEM((t, Z_SMALL), F32),
        pltpu.VMEM((16, t), F32),
        pltpu.VMEM((N_DIR, H_MLSTM, dh, dh), F32),
        pltpu.VMEM((N_DIR, H_MLSTM, 1, dh), F32),
        pltpu.VMEM((N_DIR, H_MLSTM, 1, LANE), F32),
        pltpu.VMEM((N_DIR, t, Z_SMALL), F32),
        pltpu.VMEM((N_DIR, 16, t), F32),
    ]
    return dict(kernel=kern, grid=(n_seq,), in_specs=in_specs, args=args, out_specs=out_specs,
                out_shape=out_shape, scratch=scratch, aliases=aliases, name="mlstm_scan")


def _head_rms(o, g, n_heads, width):
    out = []
    for h in range(n_heads):
        oh = o[:, width * h:width * (h + 1)]
        out.append(oh * lax.rsqrt(jnp.mean(oh * oh, axis=-1, keepdims=True) + NORM_EPS) * g)
    return out


def _mix_kernel(x_ref, og_ref, hm_ref, rg_ref, om_ref, gg_ref, gm_ref, mod_ref, gng_ref, gnm_ref,
                wbg_ref, wbm_ref, wo_ref, lg_ref, lb_ref, o_ref, *, cond_row0, tiles_per_cond):
    r = _cond_row(pl.program_id(0), cond_row0, tiles_per_cond)
    gate = mod_ref[2, pl.ds(r, 1), :]
    o_g = jnp.concatenate(_head_rms(og_ref[...] * (DK_GLA ** -0.5), gng_ref[...], H_GLA, DV_GLA), axis=1)
    a_g = (o_g * rg_ref[...].astype(F32)).astype(BF16)
    h_m = jnp.concatenate(_head_rms(hm_ref[...], gnm_ref[...], H_MLSTM, DH_MLSTM), axis=1)
    a_m = (om_ref[...].astype(F32) * h_m).astype(BF16)
    y_g = _dot(a_g, wbg_ref[...])
    y_m = _dot(a_m, wbm_ref[...])
    mix = gg_ref[...].astype(F32) * y_g + gm_ref[...].astype(F32) * y_m
    y = _dot(mix.astype(BF16), wo_ref[...])
    o_ref[...] = _layer_norm(ALPHA * x_ref[...] + gate * y, lg_ref[...], lb_ref[...])


def _mix_call(x, og, hm, zb, ada4, gn_gla, gn_ml, wbg, wbm, wo, lng, lnb, *, layer, cond_row0, rows_per_cond):
    m = x.shape[0]
    tm = MIX_TM
    kern = functools.partial(_mix_kernel, cond_row0=cond_row0, tiles_per_cond=rows_per_cond // tm)
    wspec = pl.BlockSpec((None, D_MODEL, D_MODEL), lambda i: (layer, 0, 0), pipeline_mode=pl.Buffered(1))
    return dict(
        kernel=kern, grid=(m // tm,),
        in_specs=[
            pl.BlockSpec((tm, D_MODEL), lambda i: (i, 0)),
            pl.BlockSpec((tm, V_GLA), lambda i: (i, 0)),
            pl.BlockSpec((tm, W_MLSTM), lambda i: (i, 0)),
            pl.BlockSpec((tm, V_GLA), lambda i: (i, SLAB_R_GLA)),
            pl.BlockSpec((tm, W_MLSTM), lambda i: (i, SLAB_O_MLSTM)),
            pl.BlockSpec((tm, D_MODEL), lambda i: (i, SLAB_G_GLA)),
            pl.BlockSpec((tm, D_MODEL), lambda i: (i, SLAB_G_MLSTM)),
            pl.BlockSpec((None, 3, 16, D_MODEL), lambda i: (layer, 1, 0, 0)),
            pl.BlockSpec((None, 1, DV_GLA), lambda i: (layer, 0, 0)),
            pl.BlockSpec((None, 1, DH_MLSTM), lambda i: (layer, 0, 0)),
            wspec, wspec, wspec,
            pl.BlockSpec((None, 1, D_MODEL), lambda i: (3 * layer + 1, 0, 0)),
            pl.BlockSpec((None, 1, D_MODEL), lambda i: (3 * layer + 1, 0, 0)),
        ],
        args=[x, og, hm, zb, zb, zb, zb, ada4, gn_gla, gn_ml, wbg, wbm, wo, lng, lnb],
        out_specs=[pl.BlockSpec((tm, D_MODEL), lambda i: (i, 0))],
        out_shape=[jax.ShapeDtypeStruct((m, D_MODEL), F32)],
        scratch=[], aliases={}, name="mixer_out_ln")


def _grid_pos_embed(t_len):
    rows = t_len // GRID_W
    r = jnp.repeat(jnp.arange(rows), GRID_W).astype(F32)
    col = jnp.tile(jnp.arange(GRID_W), rows).astype(F32)
    nf = D_MODEL // 4
    omega = 1.0 / (10000.0 ** (jnp.arange(nf, dtype=F32) / nf))
    er = r[:, None] * omega
    ec = col[:, None] * omega
    return jnp.concatenate([jnp.sin(er), jnp.cos(er), jnp.sin(ec), jnp.cos(ec)], axis=-1)


def _pack_ffn(w_gate, w_up, w_down):
    return w_gate.astype(BF16), w_up.astype(BF16), w_down.astype(BF16)


def _pack_w_in(w_in):
    o = np.cumsum((0, QK_GLA, QK_GLA, V_GLA, V_GLA, GLA_RANK, GLA_RANK, 2 * W_MLSTM, W_MLSTM, W_MLSTM,
                   H_MLSTM, H_MLSTM, H_MLSTM, H_MLSTM, D_MODEL, D_MODEL))
    o = [int(v) for v in o]
    assert o[4] == W_SPLIT[0] * PROJ_TN
    w16 = w_in.astype(BF16)
    big = (w16, w16[:, :, o[6]:o[9]], w16[:, :, o[13]:o[15]])
    small = jnp.concatenate([w16[:, :, o[4]:o[6]], w16[:, :, o[9]:o[13]]], axis=-1)
    small_p = jnp.pad(small, ((0, 0), (0, 0), (0, Z_SMALL - small.shape[-1])))
    return big, small_p


def kernel(x_prompt, x_sample, c, state_gla_s, state_mlstm_c, state_mlstm_n, state_mlstm_m, c_ctx,
           w_ada, b_ada, ffn1_w_gate, ffn1_w_up, ffn1_w_down, w_in, w_decay, b_decay, w_conv, b_conv,
           f_bias, gla_norm_g, mlstm_norm_g, w_br_gla, w_br_mlstm, w_out,
           ffn2_w_gate, ffn2_w_up, ffn2_w_down, ln_g, ln_b):
    bp, tp, _ = x_prompt.shape
    bs, ts, _ = x_sample.shape
    assert bs + 1 <= 16 and tp % FIN_ROWS == 0 and ts % FIN_ROWS == 0

    cond16 = jnp.zeros((16, D_MODEL), F32).at[0].set(c_ctx).at[1:1 + bs].set(c)
    ada4 = _ada_call(cond16, w_ada, b_ada)

    ffn1 = _pack_ffn(ffn1_w_gate, ffn1_w_up, ffn1_w_down)
    ffn2 = _pack_ffn(ffn2_w_gate, ffn2_w_up, ffn2_w_down)
    w_big, w_small = _pack_w_in(w_in)
    wdec = jnp.zeros((DEPTH, N_DIR, Z_SMALL, QK_GLA), F32)
    wdec = wdec.at[:, 0, 0:GLA_RANK].set(w_decay[:, 0]).at[:, 1, GLA_RANK:2 * GLA_RANK].set(w_decay[:, 1])
    bdec = b_decay.reshape(DEPTH, N_DIR, 1, QK_GLA)
    bcol = jnp.zeros((DEPTH, 1, Z_SMALL), F32)
    brow = jnp.zeros((DEPTH, 16, 1), F32)
    for d in range(N_DIR):
        bcol = bcol.at[:, 0, COL_F[d]:COL_F[d] + H_MLSTM].set(f_bias[:, d])
        brow = brow.at[:, ROW_F[d]:ROW_F[d] + H_MLSTM, 0].set(f_bias[:, d])
    bconv = b_conv.reshape(DEPTH, 1, 2 * W_MLSTM)
    gn_gla = gla_norm_g.reshape(DEPTH, 1, DV_GLA)
    gn_ml = mlstm_norm_g.reshape(DEPTH, 1, DH_MLSTM)
    wbg, wbm, wo = w_br_gla.astype(BF16), w_br_mlstm.astype(BF16), w_out.astype(BF16)
    lng = ln_g.reshape(DEPTH * 3, 1, D_MODEL)
    lnb = ln_b.reshape(DEPTH * 3, 1, D_MODEL)
    pos = _grid_pos_embed(ts)

    m_ctx, m_dec = bp * tp, bs * ts
    ctx = dict(cond_row0=0, rows_per_cond=m_ctx)
    dec = dict(cond_row0=1, rows_per_cond=ts)
    dec_states = (state_gla_s, state_mlstm_c, state_mlstm_n, state_mlstm_m)
    fuse = (m_dec == bp * FUSE_TM and ts % FUSE_TM == 0
            and tp // GLA_CHUNK <= MAX_UNROLLED_CHUNKS and tp // MLSTM_CHUNK <= MAX_UNROLLED_CHUNKS)

    def ffn(x, pos_embed, weights, layer, sub, where, **kw):
        return _ffn_call(x, pos_embed, ada4, *weights, lng, lnb, layer=layer, sub=sub, **where, **kw)

    def proj(x, layer, where, **kw):
        return _proj_call(x, ada4, w_big, w_small, layer=layer, **where, **kw)

    def gla(z, layer, state, carried, n_seq, seq_len):
        return _gla_call(z[0], z[1], wdec, bdec, state, carried, layer=layer, n_seq=n_seq, seq_len=seq_len,
                         emit_state=state is None)

    def mlstm(z, layer, states, carried, n_seq, seq_len, reduce_on_mxu):
        return _mlstm_call(z[0], z[1], z[2], w_conv, bconv, bcol, brow, states, carried, layer=layer,
                           n_seq=n_seq, seq_len=seq_len, emit_state=states is None,
                           reduce_on_mxu=reduce_on_mxu)

    def mix(x, og, hm, z, layer, where):
        return _run(_mix_call(x, og, hm, z[0], ada4, gn_gla, gn_ml, wbg, wbm, wo, lng, lnb,
                              layer=layer, **where))[0]

    xc = x_prompt.reshape(m_ctx, D_MODEL)
    xd = x_sample.reshape(m_dec, D_MODEL)
    carried = None
    for l in range(DEPTH):
        xc = _run(ffn(xc, None, ffn1, l, 0, ctx))[0]
        zc = _run(proj(xc, l, ctx))
        ml_c = mlstm(zc, l, None, None if carried is None else carried[1:], bp, tp, reduce_on_mxu=not fuse)
        gl_c = gla(zc, l, None, None if carried is None else carried[0], bp, tp)
        pos_l = pos if l == 0 else None
        if fuse:
            ml_out, (xd,) = _run_fused(ml_c, ffn(xd, pos_l, ffn1, l, 0, dec, tm=FUSE_TM), "mlstm_scan_ffn")
            gl_out, zd = _run_fused(gl_c, proj(xd, l, dec, tm=FUSE_TM), "gla_scan_in_proj")
        else:
            ml_out, gl_out = _run(ml_c), _run(gl_c)
            xd = _run(ffn(xd, pos_l, ffn1, l, 0, dec))[0]
            zd = _run(proj(xd, l, dec))
        carried = (gl_out[1], ml_out[1], ml_out[2], ml_out[3])
        xc = mix(xc, gl_out[0], ml_out[0], zc, l, ctx)
        xc = _run(ffn(xc, None, ffn2, l, 2, ctx))[0]

        og_d = _run(gla(zd, l, dec_states[0], None, bs, ts))[0]
        hm_d = _run(mlstm(zd, l, dec_states[1:], None, bs, ts, reduce_on_mxu=True))[0]
        xd = mix(xd, og_d, hm_d, zd, l, dec)
        xd = _run(ffn(xd, None, ffn2, l, 2, dec))[0]

    return (xc.reshape(bp, tp, D_MODEL), xd.reshape(bs, ts, D_MODEL)) + tuple(carried)
```

```python
import functools

import jax
import jax.numpy as jnp
import numpy as np
from jax import lax
from jax.experimental import pallas as pl
from jax.experimental.pallas import tpu as pltpu

F32 = jnp.float32
BF16 = jnp.bfloat16

D_MODEL = 1024
DEPTH = 2
GRID_W = 64
N_DIR = 2
H_GLA = 4
DK_GLA = 128
DV_GLA = 256
GLA_RANK = 16
GLA_TAU = 16.0
H_MLSTM = 4
DH_MLSTM = 256
D_FF = 2816
N_MOD = 9
LN_EPS = 1e-5
NORM_EPS = 1e-6
QK_GLA = H_GLA * DK_GLA
V_GLA = H_GLA * DV_GLA
W_MLSTM = H_MLSTM * DH_MLSTM
ALPHA = (2.0 * DEPTH) ** 0.25

Z_BIG = 9 * 1024
Z_SMALL = 128
SLAB_V_GLA, SLAB_R_GLA, SLAB_Q_MLSTM, SLAB_K_MLSTM, SLAB_V_MLSTM = 1, 2, 3, 4, 5
SLAB_O_MLSTM, SLAB_G_GLA, SLAB_G_MLSTM = 6, 7, 8
W_SPLIT = (3, 7)
COL_I = (32, 40)
COL_F = (36, 44)
ROW_I = (0, 8)
ROW_F = (4, 12)

LANE = 128
SUBLANE = 8
VMEM_LIMIT = 56 * 1024 * 1024

FFN_TM = 1024
FFN_SUB = 512
FFN_CHUNKS = ((0, 1536), (1536, 1280))
PROJ_TM = 512
PROJ_TN = 1024
MIX_TM = 512
FUSE_TM = 256
GLA_CHUNK = 64
GLA_SUB = 16
MLSTM_CHUNK = 128
FIN_ROWS = 256
MAX_UNROLLED_CHUNKS = 4


def _dot(a, b):
    return jnp.dot(a, b, preferred_element_type=F32)


def _dot_nt(a, b):
    return lax.dot_general(a, b, (((1,), (1,)), ((), ())), preferred_element_type=F32)


def _dot_tn(a, b):
    return lax.dot_general(a, b, (((0,), (0,)), ((), ())), preferred_element_type=F32)


def _split3(x):
    hi = x.astype(BF16)
    r = x - hi.astype(F32)
    mid = r.astype(BF16)
    lo = (r - mid.astype(F32)).astype(BF16)
    return hi, mid, lo


def _split2(x):
    hi = x.astype(BF16)
    return hi, (x - hi.astype(F32)).astype(BF16)


def _dot_acc(x, w):
    xh, xm, _ = _split3(x)
    wh, wl = _split2(w)
    return _dot(xh, wh) + _dot(xh, wl) + _dot(xm, wh)


def _silu(x):
    return x * jax.nn.sigmoid(x)


def _log_sigmoid(x):
    return jnp.minimum(x, 0.0) - jnp.log(1.0 + jnp.exp(-jnp.abs(x)))


def _layer_norm(y, g, b):
    mu = jnp.mean(y, axis=-1, keepdims=True)
    d = y - mu
    var = jnp.mean(d * d, axis=-1, keepdims=True)
    return d * lax.rsqrt(var + LN_EPS) * g + b


def _aligned(x, m):
    return x if isinstance(x, int) else pl.multiple_of(x, m)


def _cparams(n_axes):
    return pltpu.CompilerParams(
        dimension_semantics=("arbitrary",) * n_axes, vmem_limit_bytes=VMEM_LIMIT)


def _cond_row(i, cond_row0, tiles_per_cond):
    return cond_row0 + lax.div(i, jnp.int32(tiles_per_cond))


def _run(call):
    return pl.pallas_call(
        call["kernel"],
        grid=call["grid"],
        in_specs=call["in_specs"],
        out_specs=call["out_specs"],
        out_shape=call["out_shape"],
        scratch_shapes=call["scratch"],
        input_output_aliases=call["aliases"],
        compiler_params=_cparams(len(call["grid"])),
        name=call["name"],
    )(*call["args"])


def _run_fused(scan, mm, name):
    assert scan["grid"] == mm["grid"] and not mm["aliases"]
    counts = [len(scan["args"]), len(mm["args"]), len(scan["out_shape"]), len(mm["out_shape"]),
              len(scan["scratch"]), len(mm["scratch"])]

    def kern(*refs):
        parts, i = [], 0
        for n in counts:
            parts.append(tuple(refs[i:i + n]))
            i += n
        in_a, in_b, out_a, out_b, scr_a, scr_b = parts
        scan["kernel"](*(in_a + out_a + scr_a), hooks=mm["pieces"](in_b + out_b + scr_b))

    outs = pl.pallas_call(
        kern,
        grid=scan["grid"],
        in_specs=scan["in_specs"] + mm["in_specs"],
        out_specs=scan["out_specs"] + mm["out_specs"],
        out_shape=scan["out_shape"] + mm["out_shape"],
        scratch_shapes=scan["scratch"] + mm["scratch"],
        input_output_aliases=scan["aliases"],
        compiler_params=_cparams(len(scan["grid"])),
        name=name,
    )(*scan["args"], *mm["args"])
    n_a = counts[2]
    return outs[:n_a], outs[n_a:]


def _hook_runner(hooks):
    pending = list(hooks)

    def hook(n=1):
        for _ in range(n):
            if pending:
                pending.pop(0)()

    def flush():
        while pending:
            pending.pop(0)()

    return hook, flush


def _ada_kernel(c_ref, w_ref, b_ref, o_ref):
    o_ref[...] = _dot_acc(_silu(c_ref[...]), w_ref[...]) + b_ref[...]


def _ada_call(cond16, w_ada, b_ada):
    return pl.pallas_call(
        _ada_kernel,
        grid=(DEPTH, N_MOD),
        in_specs=[
            pl.BlockSpec((16, D_MODEL), lambda l, j: (0, 0)),
            pl.BlockSpec((None, D_MODEL, D_MODEL), lambda l, j: (l, 0, j)),
            pl.BlockSpec((None, None, 1, D_MODEL), lambda l, j: (l, j, 0, 0)),
        ],
        out_specs=pl.BlockSpec((None, None, 16, D_MODEL), lambda l, j: (l, j, 0, 0)),
        out_shape=jax.ShapeDtypeStruct((DEPTH, N_MOD, 16, D_MODEL), F32),
        compiler_params=_cparams(2),
        name="ada_mod",
    )(cond16, w_ada, b_ada.reshape(DEPTH, N_MOD, 1, D_MODEL))


def _ffn_pieces(refs, *, has_pos, cond_row0, tiles_per_cond, tm, sub):
    if has_pos:
        x_ref, pos_ref, mod_ref, wg_ref, wu_ref, wd_ref, lg_ref, lb_ref, o_ref = refs
    else:
        x_ref, mod_ref, wg_ref, wu_ref, wd_ref, lg_ref, lb_ref, o_ref = refs
        pos_ref = None
    st = {}
    pieces = []

    def start(s):
        rows = pl.ds(s * sub, sub)
        if s == 0:
            r = _cond_row(pl.program_id(0), cond_row0, tiles_per_cond)
            st["sh"] = mod_ref[0, pl.ds(r, 1), :]
            st["sc"] = mod_ref[1, pl.ds(r, 1), :]
            st["gate"] = mod_ref[2, pl.ds(r, 1), :]
        x = x_ref[rows, :]
        if has_pos:
            x = x + pos_ref[rows, :]
        st["x"] = x
        st["hm"] = (x * (1.0 + st["sc"]) + st["sh"]).astype(BF16)
        st["acc"] = None

    def gate_mm(c0, cw):
        st["g"] = _dot(st["hm"], wg_ref[:, c0:c0 + cw])

    def up_mm(c0, cw):
        st["u"] = _dot(st["hm"], wu_ref[:, c0:c0 + cw])

    def down_mm(c0, cw):
        part = _dot((_silu(st["g"]) * st["u"]).astype(BF16), wd_ref[c0:c0 + cw, :])
        st["acc"] = part if st["acc"] is None else st["acc"] + part

    def finish(s):
        y = ALPHA * st["x"] + (0.5 * st["gate"]) * st["acc"]
        o_ref[pl.ds(s * sub, sub), :] = _layer_norm(y, lg_ref[...], lb_ref[...])

    for s in range(tm // sub):
        pieces.append(functools.partial(start, s))
        for c0, cw in FFN_CHUNKS:
            pieces += [functools.partial(gate_mm, c0, cw), functools.partial(up_mm, c0, cw),
                       functools.partial(down_mm, c0, cw)]
        pieces.append(functools.partial(finish, s))
    return pieces


def _ffn_call(x, pos, ada4, wg, wu, wd, lng, lnb, *, layer, sub, cond_row0, rows_per_cond, tm=FFN_TM):
    m = x.shape[0]
    has_pos = pos is not None
    static = dict(has_pos=has_pos, cond_row0=cond_row0, tiles_per_cond=rows_per_cond // tm,
                  tm=tm, sub=min(tm, FFN_SUB))
    pieces = functools.partial(_ffn_pieces, **static)

    def kern(*refs):
        for p in pieces(refs):
            p()

    once = pl.Buffered(1)
    in_specs = [pl.BlockSpec((tm, D_MODEL), lambda i: (i, 0))]
    args = [x]
    if has_pos:
        n_pos = pos.shape[0] // tm
        if n_pos == 1:
            in_specs.append(pl.BlockSpec((tm, D_MODEL), lambda i: (0, 0), pipeline_mode=once))
        else:
            in_specs.append(pl.BlockSpec((tm, D_MODEL), lambda i: (lax.rem(i, jnp.int32(n_pos)), 0)))
        args.append(pos)
    in_specs += [
        pl.BlockSpec((None, 3, 16, D_MODEL), lambda i: (layer, sub, 0, 0)),
        pl.BlockSpec((None, D_MODEL, D_FF), lambda i: (layer, 0, 0), pipeline_mode=once),
        pl.BlockSpec((None, D_MODEL, D_FF), lambda i: (layer, 0, 0), pipeline_mode=once),
        pl.BlockSpec((None, D_FF, D_MODEL), lambda i: (layer, 0, 0), pipeline_mode=once),
        pl.BlockSpec((None, 1, D_MODEL), lambda i: (3 * layer + sub, 0, 0)),
        pl.BlockSpec((None, 1, D_MODEL), lambda i: (3 * layer + sub, 0, 0)),
    ]
    args += [ada4, wg, wu, wd, lng, lnb]
    return dict(kernel=kern, pieces=pieces, grid=(m // tm,), in_specs=in_specs, args=args,
                out_specs=[pl.BlockSpec((tm, D_MODEL), lambda i: (i, 0))],
                out_shape=[jax.ShapeDtypeStruct((m, D_MODEL), F32)],
                scratch=[], aliases={}, name="ffn_ln")


def _proj_pieces(refs, *, cond_row0, tiles_per_cond):
    x_ref, mod_ref, wa_ref, wb_ref, wc_ref, ws_ref, zb_ref, zs_ref, zst_ref = refs
    st = {}

    def weight(n):
        for ref, first in ((wa_ref, 0), (wb_ref, W_SPLIT[0]), (wc_ref, W_SPLIT[1])):
            local = n - first
            if 0 <= local * PROJ_TN < ref.shape[1]:
                return ref[:, local * PROJ_TN:(local + 1) * PROJ_TN].astype(BF16)
        raise ValueError(n)

    def start():
        r = _cond_row(pl.program_id(0), cond_row0, tiles_per_cond)
        sh = mod_ref[0, pl.ds(r, 1), :]
        sc = mod_ref[1, pl.ds(r, 1), :]
        st["hm"] = (x_ref[...] * (1.0 + sc) + sh).astype(BF16)
        zs = _dot(st["hm"], ws_ref[...])
        zs_ref[...] = zs
        zst_ref[...] = zs.T[COL_I[0]:COL_I[0] + 16, :]

    def slab(n):
        cols = slice(n * PROJ_TN, (n + 1) * PROJ_TN)
        z = _dot(st["hm"], weight(n))
        if n == SLAB_R_GLA:
            z = _silu(z)
        elif n in (SLAB_O_MLSTM, SLAB_G_GLA, SLAB_G_MLSTM):
            z = jax.nn.sigmoid(z)
        zb_ref[:, cols] = z.astype(BF16)

    return [start] + [functools.partial(slab, n) for n in range(Z_BIG // PROJ_TN)]


def _proj_call(x, ada4, w_big, w_small, *, layer, cond_row0, rows_per_cond, tm=PROJ_TM):
    m = x.shape[0]
    w_a, w_b, w_c = w_big
    assert (W_SPLIT[0] * PROJ_TN + w_b.shape[-1] + w_c.shape[-1] == Z_BIG
            and w_b.shape[-1] == (W_SPLIT[1] - W_SPLIT[0]) * PROJ_TN)
    pieces = functools.partial(_proj_pieces, cond_row0=cond_row0, tiles_per_cond=rows_per_cond // tm)

    def kern(*refs):
        for p in pieces(refs):
            p()

    once = pl.Buffered(1)
    return dict(
        kernel=kern, pieces=pieces, grid=(m // tm,),
        in_specs=[
            pl.BlockSpec((tm, D_MODEL), lambda i: (i, 0)),
            pl.BlockSpec((None, 3, 16, D_MODEL), lambda i: (layer, 1, 0, 0)),
            pl.BlockSpec((None, D_MODEL, W_SPLIT[0] * PROJ_TN), lambda i: (layer, 0, 0), pipeline_mode=once),
            pl.BlockSpec((None, D_MODEL, w_b.shape[-1]), lambda i: (layer, 0, 0), pipeline_mode=once),
            pl.BlockSpec((None, D_MODEL, w_c.shape[-1]), lambda i: (layer, 0, 0), pipeline_mode=once),
            pl.BlockSpec((None, D_MODEL, Z_SMALL), lambda i: (layer, 0, 0), pipeline_mode=once),
        ],
        args=[x, ada4, w_a, w_b, w_c, w_small],
        out_specs=[
            pl.BlockSpec((tm, Z_BIG), lambda i: (i, 0)),
            pl.BlockSpec((tm, Z_SMALL), lambda i: (i, 0)),
            pl.BlockSpec((16, tm), lambda i: (0, i)),
        ],
        out_shape=[
            jax.ShapeDtypeStruct((m, Z_BIG), BF16),
            jax.ShapeDtypeStruct((m, Z_SMALL), F32),
            jax.ShapeDtypeStruct((16, m), F32),
        ],
        scratch=[], aliases={}, name="mixer_in_proj")


def _gla_kernel(*refs, seq_len, has_state, emit_state, n_carried, hooks=()):
    it = iter(refs)
    q_ref, k_ref, v_ref, zs_ref, wdec_ref, bdec_ref = (next(it) for _ in range(6))
    s0_ref = next(it) if has_state else None
    if n_carried:
        next(it)
    oacc_s = next(it)
    sout_ref = next(it) if emit_state else None
    la_s, st_s, cum_s = (next(it) for _ in range(3))
    t_len = seq_len
    c_len = GLA_CHUNK
    n_chunks = t_len // c_len
    unrolled = n_chunks <= MAX_UNROLLED_CHUNKS
    assert unrolled or not hooks
    hook, flush = _hook_runner(hooks)
    hook(4)

    for rb in range(t_len // FIN_ROWS):
        rows = pl.ds(rb * FIN_ROWS, FIN_ROWS)
        zh, zm, _ = _split3(zs_ref[rows, :])
        for d in range(N_DIR):
            wh, wl = _split2(wdec_ref[d])
            x = _dot(zh, wh) + _dot(zh, wl) + _dot(zm, wh) + bdec_ref[d]
            la_s[d, rows, :] = _log_sigmoid(x) * (1.0 / GLA_TAU)

    for d in range(N_DIR):
        for h in range(H_GLA):
            if has_state:
                st_s[d, h] = s0_ref[d, h].T
            else:
                st_s[d, h] = jnp.zeros((DV_GLA, DK_GLA), F32)
    oacc_s[...] = jnp.zeros_like(oacc_s)

    ri = lax.broadcasted_iota(jnp.int32, (c_len, c_len), 0)
    ci = lax.broadcasted_iota(jnp.int32, (c_len, c_len), 1)
    rr = lax.broadcasted_iota(jnp.int32, (c_len, DK_GLA), 0)
    odd_rows = (lax.shift_right_logical(rr, 4) & 1) == 1
    upper_rows = rr >= 2 * GLA_SUB
    same_sub = lax.shift_right_logical(ri, 4) == lax.shift_right_logical(ci, 4)
    same_half = lax.shift_right_logical(ri, 5) == lax.shift_right_logical(ci, 5)
    causal = (ri >= ci, ri <= ci)
    tri = tuple(jnp.where(c, 1.0, 0.0).astype(BF16) for c in causal)
    pm0 = tuple(same_sub & c for c in causal)
    q1rows = (odd_rows, jnp.logical_not(odd_rows))
    q2rows = (upper_rows, jnp.logical_not(upper_rows))

    def bcast(x, n):
        return jnp.broadcast_to(x, (n, DK_GLA))

    for d in range(N_DIR):
        for cb in range(n_chunks):
            rows = pl.ds(cb * c_len, c_len)
            l3 = _split3(la_s[d, rows, :])
            cum_s[d, rows, :] = _dot(tri[d], l3[0]) + _dot(tri[d], l3[1]) + _dot(tri[d], l3[2])
    hook()

    def scores(d, h, row0):
        rows = pl.ds(row0, c_len)
        ls = slice(DK_GLA * h, DK_GLA * (h + 1))
        vs = slice(DV_GLA * h, DV_GLA * (h + 1))

        def row(i):
            g = i - i % SUBLANE
            return cum_s[d, pl.ds(_aligned(row0 + g, SUBLANE), SUBLANE), ls][i - g:i - g + 1, :]

        c = cum_s[d, rows, ls]
        zero = jnp.zeros((GLA_SUB, DK_GLA), F32)
        if d == 0:
            ref0 = jnp.concatenate([zero, bcast(row(15), 16), bcast(row(31), 16), bcast(row(47), 16)], axis=0)
            ref1 = jnp.concatenate([bcast(row(15), 32), bcast(row(47), 32)], axis=0)
            ref2, cend = row(31), row(63)
        else:
            ref0 = jnp.concatenate([bcast(row(16), 16), bcast(row(32), 16), bcast(row(48), 16), zero], axis=0)
            ref1 = jnp.concatenate([bcast(row(16), 32), bcast(row(48), 32)], axis=0)
            ref2, cend = row(32), row(0)
        q = q_ref[rows, ls].astype(F32)
        k = k_ref[rows, ls].astype(F32)
        e0 = c - ref0
        p0 = _dot_nt((q * jnp.exp(e0)).astype(BF16), (k * jnp.exp(-e0)).astype(BF16))
        x1 = jnp.exp(-jnp.abs(c - ref1))
        p1 = _dot_nt(jnp.where(q1rows[d], q * x1, 0.0).astype(BF16),
                     jnp.where(q1rows[d], 0.0, k * x1).astype(BF16))
        x2 = jnp.exp(-jnp.abs(c - ref2))
        p2 = _dot_nt(jnp.where(q2rows[d], q * x2, 0.0).astype(BF16),
                     jnp.where(q2rows[d], 0.0, k * x2).astype(BF16))
        qi = (q * jnp.exp(c)).astype(BF16)
        kst = (k * jnp.exp(cend - c)).astype(BF16)
        return dict(d=d, h=h, rows=rows, vs=vs, p0=p0, p1=p1, p2=p2, qi=qi, kst=kst, dec=jnp.exp(cend))

    def body(j, carry):
        row0 = (_aligned(j * c_len, c_len), _aligned((n_chunks - 1 - j) * c_len, c_len))
        work = [scores(d, h, row0[d]) for d in range(N_DIR) for h in range(H_GLA)]
        hook()
        for w in work:
            d, h = w["d"], w["h"]
            p = jnp.where(pm0[d], w["p0"], jnp.where(same_half, w["p1"], w["p2"]))
            o = _dot(p.astype(BF16), v_ref[w["rows"], w["vs"]]) + _dot_nt(w["qi"], st_s[d, h].astype(BF16))
            oacc_s[w["rows"], w["vs"]] += o
        hook()
        for w in work:
            d, h = w["d"], w["h"]
            st_s[d, h] = st_s[d, h] * w["dec"] + _dot_tn(v_ref[w["rows"], w["vs"]], w["kst"])
        return carry

    if unrolled:
        for j in range(n_chunks):
            body(j, 0)
    else:
        lax.fori_loop(0, n_chunks, body, 0)
    flush()

    if emit_state:
        for d in range(N_DIR):
            for h in range(H_GLA):
                sout_ref[d, h] = st_s[d, h].T


def _gla_call(zb, zs, wdec, bdec, state, carried, *, layer, n_seq, seq_len, emit_state):
    has_state = state is not None
    t = seq_len
    n_carried = 0 if carried is None else 1
    kern = functools.partial(_gla_kernel, seq_len=t, has_state=has_state, emit_state=emit_state,
                             n_carried=n_carried)
    in_specs = [
        pl.BlockSpec((t, QK_GLA), lambda b: (b, 0)),
        pl.BlockSpec((t, QK_GLA), lambda b: (b, 1)),
        pl.BlockSpec((t, V_GLA), lambda b: (b, SLAB_V_GLA)),
        pl.BlockSpec((t, Z_SMALL), lambda b: (b, 0)),
        pl.BlockSpec((None, N_DIR, Z_SMALL, QK_GLA), lambda b: (layer, 0, 0, 0)),
        pl.BlockSpec((None, N_DIR, 1, QK_GLA), lambda b: (layer, 0, 0, 0)),
    ]
    args = [zb, zb, zb, zs, wdec, bdec]
    if has_state:
        in_specs.append(pl.BlockSpec((None, None, N_DIR, H_GLA, DK_GLA, DV_GLA),
                                     lambda b: (b, layer, 0, 0, 0, 0)))
        args.append(state)
    aliases = {}
    if n_carried:
        aliases[len(args)] = 1
        in_specs.append(pl.BlockSpec(memory_space=pl.ANY))
        args.append(carried)
    out_specs = [pl.BlockSpec((t, V_GLA), lambda b: (b, 0))]
    out_shape = [jax.ShapeDtypeStruct((n_seq * t, V_GLA), F32)]
    if emit_state:
        out_specs.append(pl.BlockSpec((None, None, N_DIR, H_GLA, DK_GLA, DV_GLA),
                                      lambda b: (b, layer, 0, 0, 0, 0)))
        out_shape.append(jax.ShapeDtypeStruct((n_seq, DEPTH, N_DIR, H_GLA, DK_GLA, DV_GLA), F32))
    scratch = [
        pltpu.VMEM((N_DIR, t, QK_GLA), F32),
        pltpu.VMEM((N_DIR, H_GLA, DV_GLA, DK_GLA), F32),
        pltpu.VMEM((N_DIR, t, QK_GLA), F32),
    ]
    return dict(kernel=kern, grid=(n_seq,), in_specs=in_specs, args=args, out_specs=out_specs,
                out_shape=out_shape, scratch=scratch, aliases=aliases, name="gla_scan")


def _mlstm_kernel(*refs, seq_len, has_state, emit_state, n_carried, reduce_on_mxu, hooks=()):
    it = iter(refs)
    (q_ref, k_ref, v_ref, zs_ref, zst_ref, wconv_ref, bconv_ref,
     bcol_ref, brow_ref) = (next(it) for _ in range(9))
    if has_state:
        c0_ref, n0_ref, m0_ref = (next(it) for _ in range(3))
    for _ in range(n_carried):
        next(it)
    hacc_s = next(it)
    if emit_state:
        cout_ref, nout_ref, mout_ref = (next(it) for _ in range(3))
    (qc_s, kc_s, xp_s, lf_s, lft_s, c_s, n_s, m_s, cumc_s, cumr_s) = (next(it) for _ in range(10))
    t_len = seq_len
    c_len = MLSTM_CHUNK
    n_chunks = t_len // c_len
    dh = DH_MLSTM
    unrolled = n_chunks <= MAX_UNROLLED_CHUNKS
    assert unrolled or not hooks
    hook, flush = _hook_runner(hooks)
    hook(6)

    xp_s[pl.ds(0, 8), :] = jnp.zeros((8, dh), F32)
    xp_s[pl.ds(8 + t_len, 8), :] = jnp.zeros((8, dh), F32)
    for which, src in enumerate((q_ref, k_ref)):
        for h in range(H_MLSTM):
            hs = slice(dh * h, dh * (h + 1))
            ws = slice(which * W_MLSTM + dh * h, which * W_MLSTM + dh * (h + 1))
            xp_s[pl.ds(8, t_len), :] = src[:, hs].astype(F32)
            w0 = wconv_ref[pl.ds(0, 1), ws]
            w1 = wconv_ref[pl.ds(1, 1), ws]
            w2 = wconv_ref[pl.ds(2, 1), ws]
            bias = bconv_ref[:, ws]
            for rb in range(t_len // FIN_ROWS):
                r0 = rb * FIN_ROWS
                y = (w0 * xp_s[pl.ds(r0 + 7, FIN_ROWS), :] + w1 * xp_s[pl.ds(r0 + 8, FIN_ROWS), :]
                     + w2 * xp_s[pl.ds(r0 + 9, FIN_ROWS), :] + bias)
                a = _silu(y)
                if which == 0:
                    qc_s[pl.ds(r0, FIN_ROWS), hs] = (a * (dh ** -0.5)).astype(BF16)
                else:
                    kc_s[pl.ds(r0, FIN_ROWS), hs] = a

    lf_s[...] = _log_sigmoid(zs_ref[...] + bcol_ref[...])
    zt = zst_ref[...]
    trow = lax.broadcasted_iota(jnp.int32, zt.shape, 0)
    is_f = (lax.shift_right_logical(trow, 2) & 1) == 1
    lft_s[...] = jnp.where(is_f, _log_sigmoid(zt + brow_ref[...]), zt)

    for d in range(N_DIR):
        for h in range(H_MLSTM):
            if has_state:
                c_s[d, h] = c0_ref[d, h]
                n_s[d, h] = n0_ref[d, pl.ds(h, 1), :]
                m_s[d, h] = jnp.broadcast_to(m0_ref[pl.ds(d, 1), pl.ds(h, 1)], (1, LANE))
            else:
                c_s[d, h] = jnp.zeros((dh, dh), F32)
                n_s[d, h] = jnp.zeros((1, dh), F32)
                m_s[d, h] = jnp.zeros((1, LANE), F32)
    hacc_s[...] = jnp.zeros_like(hacc_s)

    ri = lax.broadcasted_iota(jnp.int32, (c_len, c_len), 0)
    ci = lax.broadcasted_iota(jnp.int32, (c_len, c_len), 1)
    causal = (ri >= ci, ri <= ci)
    tri_c = tuple(jnp.where(c, 1.0, 0.0).astype(BF16) for c in causal)
    tri_r = (tri_c[1], tri_c[0])

    for d in range(N_DIR):
        for cb in range(n_chunks):
            rows = pl.ds(cb * c_len, c_len)
            l3 = _split3(lf_s[rows, :])
            cumc_s[d, rows, :] = _dot(tri_c[d], l3[0]) + _dot(tri_c[d], l3[1]) + _dot(tri_c[d], l3[2])
            r3 = _split3(lft_s[:, rows])
            cumr_s[d, :, rows] = _dot(r3[0], tri_r[d]) + _dot(r3[1], tri_r[d]) + _dot(r3[2], tri_r[d])

    ones_c = jnp.ones((c_len, LANE), BF16)

    def lanes2(x):
        return jnp.concatenate([x, x], axis=1)

    def matmuls(d, h, row0):
        rows = pl.ds(row0, c_len)
        hs = slice(dh * h, dh * (h + 1))
        q = qc_s[rows, hs]
        s_raw = _dot_nt(q, kc_s[rows, hs].astype(BF16))
        qc = _dot(q, c_s[d, h].astype(BF16))
        if reduce_on_mxu:
            n_rep = jnp.broadcast_to(n_s[d, h], (LANE, dh)).astype(BF16)
            qn = _dot_nt(q, n_rep)
        else:
            qn = jnp.sum(q.astype(F32) * n_s[d, h], axis=1, keepdims=True)
        return dict(d=d, h=h, rows=rows, row0=row0, hs=hs, s_raw=s_raw, qc=qc, qn=qn)

    def gates(w):
        d, h, rows, row0 = w["d"], w["h"], w["rows"], w["row0"]
        cf = COL_F[d] + h
        end = c_len - 1 if d == 0 else 0
        cum_col = cumc_s[d, rows, pl.ds(cf, 1)]
        cum_c = jnp.broadcast_to(cum_col, (c_len, LANE))
        b_c = jnp.broadcast_to(zs_ref[rows, pl.ds(COL_I[d] + h, 1)] - cum_col, (c_len, LANE))
        g_end = end - end % SUBLANE
        cum_end = cumc_s[d, pl.ds(_aligned(row0 + g_end, SUBLANE), SUBLANE), pl.ds(cf, 1)][
            end - g_end:end - g_end + 1, :]
        b_r = lft_s[pl.ds(ROW_I[d] + h, 1), rows] - cumr_s[d, pl.ds(ROW_F[d] + h, 1), rows]
        m_prev = m_s[d, h]
        b_m = jnp.where(causal[d], b_r, -jnp.inf)
        m_rel = jnp.maximum(m_prev, jnp.broadcast_to(jnp.max(b_m, axis=1, keepdims=True), (c_len, LANE)))
        w["dmat"] = jnp.exp(b_m - m_rel)
        w["inter"] = jnp.exp(m_prev - m_rel)
        w["floor"] = jnp.exp(-(cum_c + m_rel))
        log_w = cum_end + b_c
        m_new = jnp.maximum(cum_end + m_prev, jnp.max(log_w, axis=0, keepdims=True))
        w["wgt"] = jnp.exp(log_w - m_new)
        w["decay"] = jnp.exp(cum_end + m_prev - m_new)
        w["m_new"] = m_new

    def body(j, carry):
        row0 = (_aligned(j * c_len, c_len), _aligned((n_chunks - 1 - j) * c_len, c_len))
        work = [matmuls(d, h, row0[d]) for d in range(N_DIR) for h in range(H_MLSTM)]
        hook()
        for w in work:
            gates(w)
        for w in work:
            s = w["s_raw"] * w["dmat"]
            s_hi = s.astype(BF16)
            w["sv"] = _dot(s_hi, v_ref[w["rows"], w["hs"]])
            if reduce_on_mxu:
                s_lo = (s - s_hi.astype(F32)).astype(BF16)
                w["rsum"] = _dot(s_hi, ones_c) + _dot(s_lo, ones_c)
            else:
                w["rsum"] = jnp.sum(s, axis=1, keepdims=True)
        hook()
        for w in work:
            den = w["rsum"] + w["inter"] * w["qn"]
            rn = 1.0 / jnp.maximum(jnp.abs(den), w["floor"])
            hacc_s[w["rows"], w["hs"]] += w["sv"] * lanes2(rn) + w["qc"] * lanes2(w["inter"] * rn)
        for w in work:
            d, h = w["d"], w["h"]
            kw = kc_s[w["rows"], w["hs"]] * lanes2(w["wgt"])
            decay = w["decay"]
            c_s[d, h] = decay[:, :1] * c_s[d, h] + _dot_tn(kw.astype(BF16), v_ref[w["rows"], w["hs"]])
            n_s[d, h] = lanes2(decay) * n_s[d, h] + jnp.sum(kw, axis=0, keepdims=True)
            m_s[d, h] = w["m_new"]
        return carry

    if unrolled:
        for j in range(n_chunks):
            body(j, 0)
    else:
        lax.fori_loop(0, n_chunks, body, 0)
    flush()

    if emit_state:
        for d in range(N_DIR):
            for h in range(H_MLSTM):
                cout_ref[d, h] = c_s[d, h]
                nout_ref[d, pl.ds(h, 1), :] = n_s[d, h]
                mout_ref[pl.ds(d, 1), pl.ds(h, 1)] = m_s[d, h][:, :1]


def _mlstm_call(zb, zs, zst, wconv, bconv, bcol, brow, states, carried, *, layer, n_seq, seq_len,
                emit_state, reduce_on_mxu):
    has_state = states is not None
    t = seq_len
    dh = DH_MLSTM
    n_carried = 0 if carried is None else 3
    kern = functools.partial(_mlstm_kernel, seq_len=t, has_state=has_state, emit_state=emit_state,
                             n_carried=n_carried, reduce_on_mxu=reduce_on_mxu)
    in_specs = [
        pl.BlockSpec((t, W_MLSTM), lambda b: (b, SLAB_Q_MLSTM)),
        pl.BlockSpec((t, W_MLSTM), lambda b: (b, SLAB_K_MLSTM)),
        pl.BlockSpec((t, W_MLSTM), lambda b: (b, SLAB_V_MLSTM)),
        pl.BlockSpec((t, Z_SMALL), lambda b: (b, 0)),
        pl.BlockSpec((16, t), lambda b: (0, b)),
        pl.BlockSpec((None, 3, 2 * W_MLSTM), lambda b: (layer, 0, 0)),
        pl.BlockSpec((None, 1, 2 * W_MLSTM), lambda b: (layer, 0, 0)),
        pl.BlockSpec((None, 1, Z_SMALL), lambda b: (layer, 0, 0)),
        pl.BlockSpec((None, 16, 1), lambda b: (layer, 0, 0)),
    ]
    args = [zb, zb, zb, zs, zst, wconv, bconv, bcol, brow]
    if has_state:
        c0, n0, m0 = states
        in_specs += [
            pl.BlockSpec((None, None, N_DIR, H_MLSTM, dh, dh), lambda b: (b, layer, 0, 0, 0, 0)),
            pl.BlockSpec((None, None, N_DIR, H_MLSTM, dh), lambda b: (b, layer, 0, 0, 0)),
            pl.BlockSpec((None, None, N_DIR, H_MLSTM), lambda b: (b, layer, 0, 0)),
        ]
        args += [c0, n0, m0]
    aliases = {}
    for k in range(n_carried):
        aliases[len(args)] = 1 + k
        in_specs.append(pl.BlockSpec(memory_space=pl.ANY))
        args.append(carried[k])
    out_specs = [pl.BlockSpec((t, W_MLSTM), lambda b: (b, 0))]
    out_shape = [jax.ShapeDtypeStruct((n_seq * t, W_MLSTM), F32)]
    if emit_state:
        out_specs += [
            pl.BlockSpec((None, None, N_DIR, H_MLSTM, dh, dh), lambda b: (b, layer, 0, 0, 0, 0)),
            pl.BlockSpec((None, None, N_DIR, H_MLSTM, dh), lambda b: (b, layer, 0, 0, 0)),
            pl.BlockSpec((None, None, N_DIR, H_MLSTM), lambda b: (b, layer, 0, 0)),
        ]
        out_shape += [
            jax.ShapeDtypeStruct((n_seq, DEPTH, N_DIR, H_MLSTM, dh, dh), F32),
            jax.ShapeDtypeStruct((n_seq, DEPTH, N_DIR, H_MLSTM, dh), F32),
            jax.ShapeDtypeStruct((n_seq, DEPTH, N_DIR, H_MLSTM), F32),
        ]
    scratch = [
        pltpu.VMEM((t, W_MLSTM), BF16),
        pltpu.VMEM((t, W_MLSTM), F32),
        pltpu.VMEM((t + 16, dh), F32),
        pltpu.VMEM((t, Z_SMALL), F32),
        pltpu.VMEM((16, t), F32),
        pltpu.VMEM((N_DIR, H_MLSTM, dh, dh), F32),
        pltpu.VMEM((N_DIR, H_MLSTM, 1, dh), F32),
        pltpu.VMEM((N_DIR, H_MLSTM, 1, LANE), F32),
        pltpu.VMEM((N_DIR, t, Z_SMALL), F32),
        pltpu.VMEM((N_DIR, 16, t), F32),
    ]
    return dict(kernel=kern, grid=(n_seq,), in_specs=in_specs, args=args, out_specs=out_specs,
                out_shape=out_shape, scratch=scratch, aliases=aliases, name="mlstm_scan")


def _head_rms(o, g, n_heads, width):
    out = []
    for h in range(n_heads):
        oh = o[:, width * h:width * (h + 1)]
        out.append(oh * lax.rsqrt(jnp.mean(oh * oh, axis=-1, keepdims=True) + NORM_EPS) * g)
    return out


def _mix_kernel(x_ref, og_ref, hm_ref, rg_ref, om_ref, gg_ref, gm_ref, mod_ref, gng_ref, gnm_ref,
                wbg_ref, wbm_ref, wo_ref, lg_ref, lb_ref, o_ref, *, cond_row0, tiles_per_cond):
    r = _cond_row(pl.program_id(0), cond_row0, tiles_per_cond)
    gate = mod_ref[2, pl.ds(r, 1), :]
    o_g = jnp.concatenate(_head_rms(og_ref[...] * (DK_GLA ** -0.5), gng_ref[...], H_GLA, DV_GLA), axis=1)
    a_g = (o_g * rg_ref[...].astype(F32)).astype(BF16)
    h_m = jnp.concatenate(_head_rms(hm_ref[...], gnm_ref[...], H_MLSTM, DH_MLSTM), axis=1)
    a_m = (om_ref[...].astype(F32) * h_m).astype(BF16)
    y_g = _dot(a_g, wbg_ref[...])
    y_m = _dot(a_m, wbm_ref[...])
    mix = gg_ref[...].astype(F32) * y_g + gm_ref[...].astype(F32) * y_m
    y = _dot(mix.astype(BF16), wo_ref[...])
    o_ref[...] = _layer_norm(ALPHA * x_ref[...] + gate * y, lg_ref[...], lb_ref[...])


def _mix_call(x, og, hm, zb, ada4, gn_gla, gn_ml, wbg, wbm, wo, lng, lnb, *, layer, cond_row0, rows_per_cond):
    m = x.shape[0]
    tm = MIX_TM
    kern = functools.partial(_mix_kernel, cond_row0=cond_row0, tiles_per_cond=rows_per_cond // tm)
    wspec = pl.BlockSpec((None, D_MODEL, D_MODEL), lambda i: (layer, 0, 0), pipeline_mode=pl.Buffered(1))
    return dict(
        kernel=kern, grid=(m // tm,),
        in_specs=[
            pl.BlockSpec((tm, D_MODEL), lambda i: (i, 0)),
            pl.BlockSpec((tm, V_GLA), lambda i: (i, 0)),
            pl.BlockSpec((tm, W_MLSTM), lambda i: (i, 0)),
            pl.BlockSpec((tm, V_GLA), lambda i: (i, SLAB_R_GLA)),
            pl.BlockSpec((tm, W_MLSTM), lambda i: (i, SLAB_O_MLSTM)),
            pl.BlockSpec((tm, D_MODEL), lambda i: (i, SLAB_G_GLA)),
            pl.BlockSpec((tm, D_MODEL), lambda i: (i, SLAB_G_MLSTM)),
            pl.BlockSpec((None, 3, 16, D_MODEL), lambda i: (layer, 1, 0, 0)),
            pl.BlockSpec((None, 1, DV_GLA), lambda i: (layer, 0, 0)),
            pl.BlockSpec((None, 1, DH_MLSTM), lambda i: (layer, 0, 0)),
            wspec, wspec, wspec,
            pl.BlockSpec((None, 1, D_MODEL), lambda i: (3 * layer + 1, 0, 0)),
            pl.BlockSpec((None, 1, D_MODEL), lambda i: (3 * layer + 1, 0, 0)),
        ],
        args=[x, og, hm, zb, zb, zb, zb, ada4, gn_gla, gn_ml, wbg, wbm, wo, lng, lnb],
        out_specs=[pl.BlockSpec((tm, D_MODEL), lambda i: (i, 0))],
        out_shape=[jax.ShapeDtypeStruct((m, D_MODEL), F32)],
        scratch=[], aliases={}, name="mixer_out_ln")


def _grid_pos_embed(t_len):
    rows = t_len // GRID_W
    r = jnp.repeat(jnp.arange(rows), GRID_W).astype(F32)
    col = jnp.tile(jnp.arange(GRID_W), rows).astype(F32)
    nf = D_MODEL // 4
    omega = 1.0 / (10000.0 ** (jnp.arange(nf, dtype=F32) / nf))
    er = r[:, None] * omega
    ec = col[:, None] * omega
    return jnp.concatenate([jnp.sin(er), jnp.cos(er), jnp.sin(ec), jnp.cos(ec)], axis=-1)


def _pack_ffn(w_gate, w_up, w_down):
    return w_gate.astype(BF16), w_up.astype(BF16), w_down.astype(BF16)


def _pack_w_in(w_in):
    o = np.cumsum((0, QK_GLA, QK_GLA, V_GLA, V_GLA, GLA_RANK, GLA_RANK, 2 * W_MLSTM, W_MLSTM, W_MLSTM,
                   H_MLSTM, H_MLSTM, H_MLSTM, H_MLSTM, D_MODEL, D_MODEL))
    o = [int(v) for v in o]
    assert o[4] == W_SPLIT[0] * PROJ_TN
    big = (w_in, w_in[:, :, o[6]:o[9]].astype(BF16), w_in[:, :, o[13]:o[15]].astype(BF16))
    small = jnp.concatenate([w_in[:, :, o[4]:o[6]], w_in[:, :, o[9]:o[13]]], axis=-1)
    small_p = jnp.pad(small, ((0, 0), (0, 0), (0, Z_SMALL - small.shape[-1]))).astype(BF16)
    return big, small_p


def kernel(x_prompt, x_sample, c, state_gla_s, state_mlstm_c, state_mlstm_n, state_mlstm_m, c_ctx,
           w_ada, b_ada, ffn1_w_gate, ffn1_w_up, ffn1_w_down, w_in, w_decay, b_decay, w_conv, b_conv,
           f_bias, gla_norm_g, mlstm_norm_g, w_br_gla, w_br_mlstm, w_out,
           ffn2_w_gate, ffn2_w_up, ffn2_w_down, ln_g, ln_b):
    bp, tp, _ = x_prompt.shape
    bs, ts, _ = x_sample.shape
    assert bs + 1 <= 16 and tp % FIN_ROWS == 0 and ts % FIN_ROWS == 0

    cond16 = jnp.zeros((16, D_MODEL), F32).at[0].set(c_ctx).at[1:1 + bs].set(c)
    ada4 = _ada_call(cond16, w_ada, b_ada)

    ffn1 = _pack_ffn(ffn1_w_gate, ffn1_w_up, ffn1_w_down)
    ffn2 = _pack_ffn(ffn2_w_gate, ffn2_w_up, ffn2_w_down)
    w_big, w_small = _pack_w_in(w_in)
    wdec = jnp.zeros((DEPTH, N_DIR, Z_SMALL, QK_GLA), F32)
    wdec = wdec.at[:, 0, 0:GLA_RANK].set(w_decay[:, 0]).at[:, 1, GLA_RANK:2 * GLA_RANK].set(w_decay[:, 1])
    bdec = b_decay.reshape(DEPTH, N_DIR, 1, QK_GLA)
    bcol = jnp.zeros((DEPTH, 1, Z_SMALL), F32)
    brow = jnp.zeros((DEPTH, 16, 1), F32)
    for d in range(N_DIR):
        bcol = bcol.at[:, 0, COL_F[d]:COL_F[d] + H_MLSTM].set(f_bias[:, d])
        brow = brow.at[:, ROW_F[d]:ROW_F[d] + H_MLSTM, 0].set(f_bias[:, d])
    bconv = b_conv.reshape(DEPTH, 1, 2 * W_MLSTM)
    gn_gla = gla_norm_g.reshape(DEPTH, 1, DV_GLA)
    gn_ml = mlstm_norm_g.reshape(DEPTH, 1, DH_MLSTM)
    wbg, wbm, wo = w_br_gla.astype(BF16), w_br_mlstm.astype(BF16), w_out.astype(BF16)
    lng = ln_g.reshape(DEPTH * 3, 1, D_MODEL)
    lnb = ln_b.reshape(DEPTH * 3, 1, D_MODEL)
    pos = _grid_pos_embed(ts)

    m_ctx, m_dec = bp * tp, bs * ts
    ctx = dict(cond_row0=0, rows_per_cond=m_ctx)
    dec = dict(cond_row0=1, rows_per_cond=ts)
    dec_states = (state_gla_s, state_mlstm_c, state_mlstm_n, state_mlstm_m)
    fuse = (m_dec == bp * FUSE_TM and ts % FUSE_TM == 0
            and tp // GLA_CHUNK <= MAX_UNROLLED_CHUNKS and tp // MLSTM_CHUNK <= MAX_UNROLLED_CHUNKS)

    def ffn(x, pos_embed, weights, layer, sub, where, **kw):
        return _ffn_call(x, pos_embed, ada4, *weights, lng, lnb, layer=layer, sub=sub, **where, **kw)

    def proj(x, layer, where, **kw):
        return _proj_call(x, ada4, w_big, w_small, layer=layer, **where, **kw)

    def gla(z, layer, state, carried, n_seq, seq_len):
        return _gla_call(z[0], z[1], wdec, bdec, state, carried, layer=layer, n_seq=n_seq, seq_len=seq_len,
                         emit_state=state is None)

    def mlstm(z, layer, states, carried, n_seq, seq_len, reduce_on_mxu):
        return _mlstm_call(z[0], z[1], z[2], w_conv, bconv, bcol, brow, states, carried, layer=layer,
                           n_seq=n_seq, seq_len=seq_len, emit_state=states is None,
                           reduce_on_mxu=reduce_on_mxu)

    def mix(x, og, hm, z, layer, where):
        return _run(_mix_call(x, og, hm, z[0], ada4, gn_gla, gn_ml, wbg, wbm, wo, lng, lnb,
                              layer=layer, **where))[0]

    xc = x_prompt.reshape(m_ctx, D_MODEL)
    xd = x_sample.reshape(m_dec, D_MODEL)
    carried = None
    for l in range(DEPTH):
        xc = _run(ffn(xc, None, ffn1, l, 0, ctx))[0]
        zc = _run(proj(xc, l, ctx))
        ml_c = mlstm(zc, l, None, None if carried is None else carried[1:], bp, tp, reduce_on_mxu=not fuse)
        gl_c = gla(zc, l, None, None if carried is None else carried[0], bp, tp)
        pos_l = pos if l == 0 else None
        if fuse:
            ml_out, (xd,) = _run_fused(ml_c, ffn(xd, pos_l, ffn1, l, 0, dec, tm=FUSE_TM), "mlstm_scan_ffn")
            gl_out, zd = _run_fused(gl_c, proj(xd, l, dec, tm=FUSE_TM), "gla_scan_in_proj")
        else:
            ml_out, gl_out = _run(ml_c), _run(gl_c)
            xd = _run(ffn(xd, pos_l, ffn1, l, 0, dec))[0]
            zd = _run(proj(xd, l, dec))
        carried = (gl_out[1], ml_out[1], ml_out[2], ml_out[3])
        xc = mix(xc, gl_out[0], ml_out[0], zc, l, ctx)
        xc = _run(ffn(xc, None, ffn2, l, 2, ctx))[0]

        og_d = _run(gla(zd, l, dec_states[0], None, bs, ts))[0]
        hm_d = _run(mlstm(zd, l, dec_states[1:], None, bs, ts, reduce_on_mxu=True))[0]
        xd = mix(xd, og_d, hm_d, zd, l, dec)
        xd = _run(ffn(xd, None, ffn2, l, 2, dec))[0]

    return (xc.reshape(bp, tp, D_MODEL), xd.reshape(bs, ts, D_MODEL)) + tuple(carried)
```

```python
import functools

import jax
import jax.numpy as jnp
import numpy as np
from jax import lax
from jax.experimental import pallas as pl
from jax.experimental.pallas import tpu as pltpu

F32 = jnp.float32
BF16 = jnp.bfloat16

D_MODEL = 1024
DEPTH = 2
GRID_W = 64
N_DIR = 2
H_GLA = 4
DK_GLA = 128
DV_GLA = 256
GLA_RANK = 16
GLA_TAU = 16.0
H_MLSTM = 4
DH_MLSTM = 256
D_FF = 2816
N_MOD = 9
LN_EPS = 1e-5
NORM_EPS = 1e-6
QK_GLA = H_GLA * DK_GLA
V_GLA = H_GLA * DV_GLA
W_MLSTM = H_MLSTM * DH_MLSTM
ALPHA = (2.0 * DEPTH) ** 0.25

Z_BIG = 9 * 1024
Z_SMALL = 128
SLAB_V_GLA, SLAB_R_GLA, SLAB_Q_MLSTM, SLAB_K_MLSTM, SLAB_V_MLSTM = 1, 2, 3, 4, 5
SLAB_O_MLSTM, SLAB_G_GLA, SLAB_G_MLSTM = 6, 7, 8
W_SPLIT = (3, 7)
COL_I = (32, 40)
COL_F = (36, 44)
ROW_I = (0, 8)
ROW_F = (4, 12)

LANE = 128
SUBLANE = 8
VMEM_LIMIT = 56 * 1024 * 1024

FFN_TM = 1024
FFN_SUB = 512
FFN_CHUNKS = ((0, 1536), (1536, 1280))
PROJ_TM = 512
PROJ_TN = 1024
MIX_TM = 512
FUSE_TM = 256
GLA_CHUNK = 64
GLA_SUB = 16
MLSTM_CHUNK = 128
FIN_ROWS = 256
MAX_UNROLLED_CHUNKS = 4


def _dot(a, b):
    return jnp.dot(a, b, preferred_element_type=F32)


def _dot_nt(a, b):
    return lax.dot_general(a, b, (((1,), (1,)), ((), ())), preferred_element_type=F32)


def _dot_tn(a, b):
    return lax.dot_general(a, b, (((0,), (0,)), ((), ())), preferred_element_type=F32)


def _split3(x):
    hi = x.astype(BF16)
    r = x - hi.astype(F32)
    mid = r.astype(BF16)
    lo = (r - mid.astype(F32)).astype(BF16)
    return hi, mid, lo


def _split2(x):
    hi = x.astype(BF16)
    return hi, (x - hi.astype(F32)).astype(BF16)


def _dot_acc(x, w):
    xh, xm, _ = _split3(x)
    wh, wl = _split2(w)
    return _dot(xh, wh) + _dot(xh, wl) + _dot(xm, wh)


def _silu(x):
    return x * jax.nn.sigmoid(x)


def _log_sigmoid(x):
    return jnp.minimum(x, 0.0) - jnp.log(1.0 + jnp.exp(-jnp.abs(x)))


def _layer_norm(y, g, b):
    mu = jnp.mean(y, axis=-1, keepdims=True)
    d = y - mu
    var = jnp.mean(d * d, axis=-1, keepdims=True)
    return d * lax.rsqrt(var + LN_EPS) * g + b


def _aligned(x, m):
    return x if isinstance(x, int) else pl.multiple_of(x, m)


def _cparams(n_axes):
    return pltpu.CompilerParams(
        dimension_semantics=("arbitrary",) * n_axes, vmem_limit_bytes=VMEM_LIMIT)


def _cond_row(i, cond_row0, tiles_per_cond):
    return cond_row0 + lax.div(i, jnp.int32(tiles_per_cond))


def _run(call):
    return pl.pallas_call(
        call["kernel"],
        grid=call["grid"],
        in_specs=call["in_specs"],
        out_specs=call["out_specs"],
        out_shape=call["out_shape"],
        scratch_shapes=call["scratch"],
        input_output_aliases=call["aliases"],
        compiler_params=_cparams(len(call["grid"])),
        name=call["name"],
    )(*call["args"])


def _run_fused(scan, mm, name):
    assert scan["grid"] == mm["grid"] and not mm["aliases"]
    counts = [len(scan["args"]), len(mm["args"]), len(scan["out_shape"]), len(mm["out_shape"]),
              len(scan["scratch"]), len(mm["scratch"])]

    def kern(*refs):
        parts, i = [], 0
        for n in counts:
            parts.append(tuple(refs[i:i + n]))
            i += n
        in_a, in_b, out_a, out_b, scr_a, scr_b = parts
        scan["kernel"](*(in_a + out_a + scr_a), hooks=mm["pieces"](in_b + out_b + scr_b))

    outs = pl.pallas_call(
        kern,
        grid=scan["grid"],
        in_specs=scan["in_specs"] + mm["in_specs"],
        out_specs=scan["out_specs"] + mm["out_specs"],
        out_shape=scan["out_shape"] + mm["out_shape"],
        scratch_shapes=scan["scratch"] + mm["scratch"],
        input_output_aliases=scan["aliases"],
        compiler_params=_cparams(len(scan["grid"])),
        name=name,
    )(*scan["args"], *mm["args"])
    n_a = counts[2]
    return outs[:n_a], outs[n_a:]


def _hook_runner(hooks):
    pending = list(hooks)

    def hook(n=1):
        for _ in range(n):
            if pending:
                pending.pop(0)()

    def flush():
        while pending:
            pending.pop(0)()

    return hook, flush


def _ada_kernel(c_ref, w_ref, b_ref, o_ref):
    o_ref[...] = _dot_acc(_silu(c_ref[...]), w_ref[...]) + b_ref[...]


def _ada_call(cond16, w_ada, b_ada):
    return pl.pallas_call(
        _ada_kernel,
        grid=(DEPTH, N_MOD),
        in_specs=[
            pl.BlockSpec((16, D_MODEL), lambda l, j: (0, 0)),
            pl.BlockSpec((None, D_MODEL, D_MODEL), lambda l, j: (l, 0, j)),
            pl.BlockSpec((None, None, 1, D_MODEL), lambda l, j: (l, j, 0, 0)),
        ],
        out_specs=pl.BlockSpec((None, None, 16, D_MODEL), lambda l, j: (l, j, 0, 0)),
        out_shape=jax.ShapeDtypeStruct((DEPTH, N_MOD, 16, D_MODEL), F32),
        compiler_params=_cparams(2),
        name="ada_mod",
    )(cond16, w_ada, b_ada.reshape(DEPTH, N_MOD, 1, D_MODEL))


def _ffn_pieces(refs, *, has_pos, cond_row0, tiles_per_cond, tm, sub):
    if has_pos:
        x_ref, pos_ref, mod_ref, wg_ref, wu_ref, wd_ref, lg_ref, lb_ref, o_ref = refs
    else:
        x_ref, mod_ref, wg_ref, wu_ref, wd_ref, lg_ref, lb_ref, o_ref = refs
        pos_ref = None
    st = {}
    pieces = []

    def start(s):
        rows = pl.ds(s * sub, sub)
        if s == 0:
            r = _cond_row(pl.program_id(0), cond_row0, tiles_per_cond)
            st["sh"] = mod_ref[0, pl.ds(r, 1), :]
            st["sc"] = mod_ref[1, pl.ds(r, 1), :]
            st["gate"] = mod_ref[2, pl.ds(r, 1), :]
        x = x_ref[rows, :]
        if has_pos:
            x = x + pos_ref[rows, :]
        st["x"] = x
        st["hm"] = (x * (1.0 + st["sc"]) + st["sh"]).astype(BF16)
        st["acc"] = None

    def gate_mm(c0, cw):
        st["g"] = _dot(st["hm"], wg_ref[:, c0:c0 + cw])

    def up_mm(c0, cw):
        st["u"] = _dot(st["hm"], wu_ref[:, c0:c0 + cw])

    def down_mm(c0, cw):
        part = _dot((_silu(st["g"]) * st["u"]).astype(BF16), wd_ref[c0:c0 + cw, :])
        st["acc"] = part if st["acc"] is None else st["acc"] + part

    def finish(s):
        y = ALPHA * st["x"] + (0.5 * st["gate"]) * st["acc"]
        o_ref[pl.ds(s * sub, sub), :] = _layer_norm(y, lg_ref[...], lb_ref[...])

    for s in range(tm // sub):
        pieces.append(functools.partial(start, s))
        for c0, cw in FFN_CHUNKS:
            pieces += [functools.partial(gate_mm, c0, cw), functools.partial(up_mm, c0, cw),
                       functools.partial(down_mm, c0, cw)]
        pieces.append(functools.partial(finish, s))
    return pieces


def _ffn_call(x, pos, ada4, wg, wu, wd, lng, lnb, *, layer, sub, cond_row0, rows_per_cond, tm=FFN_TM):
    m = x.shape[0]
    has_pos = pos is not None
    static = dict(has_pos=has_pos, cond_row0=cond_row0, tiles_per_cond=rows_per_cond // tm,
                  tm=tm, sub=min(tm, FFN_SUB))
    pieces = functools.partial(_ffn_pieces, **static)

    def kern(*refs):
        for p in pieces(refs):
            p()

    once = pl.Buffered(1)
    in_specs = [pl.BlockSpec((tm, D_MODEL), lambda i: (i, 0))]
    args = [x]
    if has_pos:
        n_pos = pos.shape[0] // tm
        if n_pos == 1:
            in_specs.append(pl.BlockSpec((tm, D_MODEL), lambda i: (0, 0), pipeline_mode=once))
        else:
            in_specs.append(pl.BlockSpec((tm, D_MODEL), lambda i: (lax.rem(i, jnp.int32(n_pos)), 0)))
        args.append(pos)
    in_specs += [
        pl.BlockSpec((None, 3, 16, D_MODEL), lambda i: (layer, sub, 0, 0)),
        pl.BlockSpec((None, D_MODEL, D_FF), lambda i: (layer, 0, 0), pipeline_mode=once),
        pl.BlockSpec((None, D_MODEL, D_FF), lambda i: (layer, 0, 0), pipeline_mode=once),
        pl.BlockSpec((None, D_FF, D_MODEL), lambda i: (layer, 0, 0), pipeline_mode=once),
        pl.BlockSpec((None, 1, D_MODEL), lambda i: (3 * layer + sub, 0, 0)),
        pl.BlockSpec((None, 1, D_MODEL), lambda i: (3 * layer + sub, 0, 0)),
    ]
    args += [ada4, wg, wu, wd, lng, lnb]
    return dict(kernel=kern, pieces=pieces, grid=(m // tm,), in_specs=in_specs, args=args,
                out_specs=[pl.BlockSpec((tm, D_MODEL), lambda i: (i, 0))],
                out_shape=[jax.ShapeDtypeStruct((m, D_MODEL), F32)],
                scratch=[], aliases={}, name="ffn_ln")


def _proj_pieces(refs, *, cond_row0, tiles_per_cond):
    x_ref, mod_ref, wa_ref, wb_ref, wc_ref, ws_ref, zb_ref, zs_ref, zst_ref = refs
    st = {}

    def weight(n):
        for ref, first in ((wa_ref, 0), (wb_ref, W_SPLIT[0]), (wc_ref, W_SPLIT[1])):
            local = n - first
            if 0 <= local * PROJ_TN < ref.shape[1]:
                return ref[:, local * PROJ_TN:(local + 1) * PROJ_TN]
        raise ValueError(n)

    def start():
        r = _cond_row(pl.program_id(0), cond_row0, tiles_per_cond)
        sh = mod_ref[0, pl.ds(r, 1), :]
        sc = mod_ref[1, pl.ds(r, 1), :]
        st["hm"] = (x_ref[...] * (1.0 + sc) + sh).astype(BF16)
        zs = _dot(st["hm"], ws_ref[...])
        zs_ref[...] = zs
        zst_ref[...] = zs.T[COL_I[0]:COL_I[0] + 16, :]

    def slab(n):
        cols = slice(n * PROJ_TN, (n + 1) * PROJ_TN)
        z = _dot(st["hm"], weight(n))
        if n == SLAB_R_GLA:
            z = _silu(z)
        elif n in (SLAB_O_MLSTM, SLAB_G_GLA, SLAB_G_MLSTM):
            z = jax.nn.sigmoid(z)
        zb_ref[:, cols] = z.astype(BF16)

    return [start] + [functools.partial(slab, n) for n in range(Z_BIG // PROJ_TN)]


def _proj_call(x, ada4, w_big, w_small, *, layer, cond_row0, rows_per_cond, tm=PROJ_TM):
    m = x.shape[0]
    w_a, w_b, w_c = w_big
    assert (W_SPLIT[0] * PROJ_TN + w_b.shape[-1] + w_c.shape[-1] == Z_BIG
            and w_b.shape[-1] == (W_SPLIT[1] - W_SPLIT[0]) * PROJ_TN)
    pieces = functools.partial(_proj_pieces, cond_row0=cond_row0, tiles_per_cond=rows_per_cond // tm)

    def kern(*refs):
        for p in pieces(refs):
            p()

    once = pl.Buffered(1)
    return dict(
        kernel=kern, pieces=pieces, grid=(m // tm,),
        in_specs=[
            pl.BlockSpec((tm, D_MODEL), lambda i: (i, 0)),
            pl.BlockSpec((None, 3, 16, D_MODEL), lambda i: (layer, 1, 0, 0)),
            pl.BlockSpec((None, D_MODEL, W_SPLIT[0] * PROJ_TN), lambda i: (layer, 0, 0), pipeline_mode=once),
            pl.BlockSpec((None, D_MODEL, w_b.shape[-1]), lambda i: (layer, 0, 0), pipeline_mode=once),
            pl.BlockSpec((None, D_MODEL, w_c.shape[-1]), lambda i: (layer, 0, 0), pipeline_mode=once),
            pl.BlockSpec((None, D_MODEL, Z_SMALL), lambda i: (layer, 0, 0), pipeline_mode=once),
        ],
        args=[x, ada4, w_a, w_b, w_c, w_small],
        out_specs=[
            pl.BlockSpec((tm, Z_BIG), lambda i: (i, 0)),
            pl.BlockSpec((tm, Z_SMALL), lambda i: (i, 0)),
            pl.BlockSpec((16, tm), lambda i: (0, i)),
        ],
        out_shape=[
            jax.ShapeDtypeStruct((m, Z_BIG), BF16),
            jax.ShapeDtypeStruct((m, Z_SMALL), F32),
            jax.ShapeDtypeStruct((16, m), F32),
        ],
        scratch=[], aliases={}, name="mixer_in_proj")


def _gla_kernel(*refs, seq_len, has_state, emit_state, n_carried, hooks=()):
    it = iter(refs)
    q_ref, k_ref, v_ref, zs_ref, wdec_ref, bdec_ref = (next(it) for _ in range(6))
    s0_ref = next(it) if has_state else None
    if n_carried:
        next(it)
    oacc_s = next(it)
    sout_ref = next(it) if emit_state else None
    la_s, st_s, cum_s = (next(it) for _ in range(3))
    t_len = seq_len
    c_len = GLA_CHUNK
    n_chunks = t_len // c_len
    unrolled = n_chunks <= MAX_UNROLLED_CHUNKS
    assert unrolled or not hooks
    hook, flush = _hook_runner(hooks)
    hook(4)

    for rb in range(t_len // FIN_ROWS):
        rows = pl.ds(rb * FIN_ROWS, FIN_ROWS)
        zh, zm, _ = _split3(zs_ref[rows, :])
        for d in range(N_DIR):
            wh, wl = _split2(wdec_ref[d])
            x = _dot(zh, wh) + _dot(zh, wl) + _dot(zm, wh) + bdec_ref[d]
            la_s[d, rows, :] = _log_sigmoid(x) * (1.0 / GLA_TAU)

    for d in range(N_DIR):
        for h in range(H_GLA):
            if has_state:
                st_s[d, h] = s0_ref[d, h].T
            else:
                st_s[d, h] = jnp.zeros((DV_GLA, DK_GLA), F32)
    oacc_s[...] = jnp.zeros_like(oacc_s)

    ri = lax.broadcasted_iota(jnp.int32, (c_len, c_len), 0)
    ci = lax.broadcasted_iota(jnp.int32, (c_len, c_len), 1)
    rr = lax.broadcasted_iota(jnp.int32, (c_len, DK_GLA), 0)
    odd_rows = (lax.shift_right_logical(rr, 4) & 1) == 1
    upper_rows = rr >= 2 * GLA_SUB
    same_sub = lax.shift_right_logical(ri, 4) == lax.shift_right_logical(ci, 4)
    same_half = lax.shift_right_logical(ri, 5) == lax.shift_right_logical(ci, 5)
    causal = (ri >= ci, ri <= ci)
    tri = tuple(jnp.where(c, 1.0, 0.0).astype(BF16) for c in causal)
    pm0 = tuple(same_sub & c for c in causal)
    q1rows = (odd_rows, jnp.logical_not(odd_rows))
    q2rows = (upper_rows, jnp.logical_not(upper_rows))

    def bcast(x, n):
        return jnp.broadcast_to(x, (n, DK_GLA))

    for d in range(N_DIR):
        for cb in range(n_chunks):
            rows = pl.ds(cb * c_len, c_len)
            l3 = _split3(la_s[d, rows, :])
            cum_s[d, rows, :] = _dot(tri[d], l3[0]) + _dot(tri[d], l3[1]) + _dot(tri[d], l3[2])
    hook()

    def scores(d, h, row0):
        rows = pl.ds(row0, c_len)
        ls = slice(DK_GLA * h, DK_GLA * (h + 1))
        vs = slice(DV_GLA * h, DV_GLA * (h + 1))

        def row(i):
            g = i - i % SUBLANE
            return cum_s[d, pl.ds(_aligned(row0 + g, SUBLANE), SUBLANE), ls][i - g:i - g + 1, :]

        c = cum_s[d, rows, ls]
        zero = jnp.zeros((GLA_SUB, DK_GLA), F32)
        if d == 0:
            ref0 = jnp.concatenate([zero, bcast(row(15), 16), bcast(row(31), 16), bcast(row(47), 16)], axis=0)
            ref1 = jnp.concatenate([bcast(row(15), 32), bcast(row(47), 32)], axis=0)
            ref2, cend = row(31), row(63)
        else:
            ref0 = jnp.concatenate([bcast(row(16), 16), bcast(row(32), 16), bcast(row(48), 16), zero], axis=0)
            ref1 = jnp.concatenate([bcast(row(16), 32), bcast(row(48), 32)], axis=0)
            ref2, cend = row(32), row(0)
        q = q_ref[rows, ls].astype(F32)
        k = k_ref[rows, ls].astype(F32)
        e0 = c - ref0
        p0 = _dot_nt((q * jnp.exp(e0)).astype(BF16), (k * jnp.exp(-e0)).astype(BF16))
        x1 = jnp.exp(-jnp.abs(c - ref1))
        p1 = _dot_nt(jnp.where(q1rows[d], q * x1, 0.0).astype(BF16),
                     jnp.where(q1rows[d], 0.0, k * x1).astype(BF16))
        x2 = jnp.exp(-jnp.abs(c - ref2))
        p2 = _dot_nt(jnp.where(q2rows[d], q * x2, 0.0).astype(BF16),
                     jnp.where(q2rows[d], 0.0, k * x2).astype(BF16))
        qi = (q * jnp.exp(c)).astype(BF16)
        kst = (k * jnp.exp(cend - c)).astype(BF16)
        return dict(d=d, h=h, rows=rows, vs=vs, p0=p0, p1=p1, p2=p2, qi=qi, kst=kst, dec=jnp.exp(cend))

    def body(j, carry):
        row0 = (_aligned(j * c_len, c_len), _aligned((n_chunks - 1 - j) * c_len, c_len))
        work = [scores(d, h, row0[d]) for d in range(N_DIR) for h in range(H_GLA)]
        hook()
        for w in work:
            d, h = w["d"], w["h"]
            p = jnp.where(pm0[d], w["p0"], jnp.where(same_half, w["p1"], w["p2"]))
            o = _dot(p.astype(BF16), v_ref[w["rows"], w["vs"]]) + _dot_nt(w["qi"], st_s[d, h].astype(BF16))
            oacc_s[w["rows"], w["vs"]] += o
        hook()
        for w in work:
            d, h = w["d"], w["h"]
            st_s[d, h] = st_s[d, h] * w["dec"] + _dot_tn(v_ref[w["rows"], w["vs"]], w["kst"])
        return carry

    if unrolled:
        for j in range(n_chunks):
            body(j, 0)
    else:
        lax.fori_loop(0, n_chunks, body, 0)
    flush()

    if emit_state:
        for d in range(N_DIR):
            for h in range(H_GLA):
                sout_ref[d, h] = st_s[d, h].T


def _gla_call(zb, zs, wdec, bdec, state, carried, *, layer, n_seq, seq_len, emit_state):
    has_state = state is not None
    t = seq_len
    n_carried = 0 if carried is None else 1
    kern = functools.partial(_gla_kernel, seq_len=t, has_state=has_state, emit_state=emit_state,
                             n_carried=n_carried)
    in_specs = [
        pl.BlockSpec((t, QK_GLA), lambda b: (b, 0)),
        pl.BlockSpec((t, QK_GLA), lambda b: (b, 1)),
        pl.BlockSpec((t, V_GLA), lambda b: (b, SLAB_V_GLA)),
        pl.BlockSpec((t, Z_SMALL), lambda b: (b, 0)),
        pl.BlockSpec((None, N_DIR, Z_SMALL, QK_GLA), lambda b: (layer, 0, 0, 0)),
        pl.BlockSpec((None, N_DIR, 1, QK_GLA), lambda b: (layer, 0, 0, 0)),
    ]
    args = [zb, zb, zb, zs, wdec, bdec]
    if has_state:
        in_specs.append(pl.BlockSpec((None, None, N_DIR, H_GLA, DK_GLA, DV_GLA),
                                     lambda b: (b, layer, 0, 0, 0, 0)))
        args.append(state)
    aliases = {}
    if n_carried:
        aliases[len(args)] = 1
        in_specs.append(pl.BlockSpec(memory_space=pl.ANY))
        args.append(carried)
    out_specs = [pl.BlockSpec((t, V_GLA), lambda b: (b, 0))]
    out_shape = [jax.ShapeDtypeStruct((n_seq * t, V_GLA), F32)]
    if emit_state:
        out_specs.append(pl.BlockSpec((None, None, N_DIR, H_GLA, DK_GLA, DV_GLA),
                                      lambda b: (b, layer, 0, 0, 0, 0)))
        out_shape.append(jax.ShapeDtypeStruct((n_seq, DEPTH, N_DIR, H_GLA, DK_GLA, DV_GLA), F32))
    scratch = [
        pltpu.VMEM((N_DIR, t, QK_GLA), F32),
        pltpu.VMEM((N_DIR, H_GLA, DV_GLA, DK_GLA), F32),
        pltpu.VMEM((N_DIR, t, QK_GLA), F32),
    ]
    return dict(kernel=kern, grid=(n_seq,), in_specs=in_specs, args=args, out_specs=out_specs,
                out_shape=out_shape, scratch=scratch, aliases=aliases, name="gla_scan")


def _mlstm_kernel(*refs, seq_len, has_state, emit_state, n_carried, reduce_on_mxu, hooks=()):
    it = iter(refs)
    (q_ref, k_ref, v_ref, zs_ref, zst_ref, wconv_ref, bconv_ref,
     bcol_ref, brow_ref) = (next(it) for _ in range(9))
    if has_state:
        c0_ref, n0_ref, m0_ref = (next(it) for _ in range(3))
    for _ in range(n_carried):
        next(it)
    hacc_s = next(it)
    if emit_state:
        cout_ref, nout_ref, mout_ref = (next(it) for _ in range(3))
    (qc_s, kc_s, xp_s, lf_s, lft_s, c_s, n_s, m_s, cumc_s, cumr_s) = (next(it) for _ in range(10))
    t_len = seq_len
    c_len = MLSTM_CHUNK
    n_chunks = t_len // c_len
    dh = DH_MLSTM
    unrolled = n_chunks <= MAX_UNROLLED_CHUNKS
    assert unrolled or not hooks
    hook, flush = _hook_runner(hooks)
    hook(6)

    xp_s[pl.ds(0, 8), :] = jnp.zeros((8, dh), F32)
    xp_s[pl.ds(8 + t_len, 8), :] = jnp.zeros((8, dh), F32)
    for which, src in enumerate((q_ref, k_ref)):
        for h in range(H_MLSTM):
            hs = slice(dh * h, dh * (h + 1))
            ws = slice(which * W_MLSTM + dh * h, which * W_MLSTM + dh * (h + 1))
            xp_s[pl.ds(8, t_len), :] = src[:, hs].astype(F32)
            w0 = wconv_ref[pl.ds(0, 1), ws]
            w1 = wconv_ref[pl.ds(1, 1), ws]
            w2 = wconv_ref[pl.ds(2, 1), ws]
            bias = bconv_ref[:, ws]
            for rb in range(t_len // FIN_ROWS):
                r0 = rb * FIN_ROWS
                y = (w0 * xp_s[pl.ds(r0 + 7, FIN_ROWS), :] + w1 * xp_s[pl.ds(r0 + 8, FIN_ROWS), :]
                     + w2 * xp_s[pl.ds(r0 + 9, FIN_ROWS), :] + bias)
                a = _silu(y)
                if which == 0:
                    qc_s[pl.ds(r0, FIN_ROWS), hs] = (a * (dh ** -0.5)).astype(BF16)
                else:
                    kc_s[pl.ds(r0, FIN_ROWS), hs] = a

    lf_s[...] = _log_sigmoid(zs_ref[...] + bcol_ref[...])
    zt = zst_ref[...]
    trow = lax.broadcasted_iota(jnp.int32, zt.shape, 0)
    is_f = (lax.shift_right_logical(trow, 2) & 1) == 1
    lft_s[...] = jnp.where(is_f, _log_sigmoid(zt + brow_ref[...]), zt)

    for d in range(N_DIR):
        for h in range(H_MLSTM):
            if has_state:
                c_s[d, h] = c0_ref[d, h]
                n_s[d, h] = n0_ref[d, pl.ds(h, 1), :]
                m_s[d, h] = jnp.broadcast_to(m0_ref[pl.ds(d, 1), pl.ds(h, 1)], (1, LANE))
            else:
                c_s[d, h] = jnp.zeros((dh, dh), F32)
                n_s[d, h] = jnp.zeros((1, dh), F32)
                m_s[d, h] = jnp.zeros((1, LANE), F32)
    hacc_s[...] = jnp.zeros_like(hacc_s)

    ri = lax.broadcasted_iota(jnp.int32, (c_len, c_len), 0)
    ci = lax.broadcasted_iota(jnp.int32, (c_len, c_len), 1)
    causal = (ri >= ci, ri <= ci)
    tri_c = tuple(jnp.where(c, 1.0, 0.0).astype(BF16) for c in causal)
    tri_r = (tri_c[1], tri_c[0])

    for d in range(N_DIR):
        for cb in range(n_chunks):
            rows = pl.ds(cb * c_len, c_len)
            l3 = _split3(lf_s[rows, :])
            cumc_s[d, rows, :] = _dot(tri_c[d], l3[0]) + _dot(tri_c[d], l3[1]) + _dot(tri_c[d], l3[2])
            r3 = _split3(lft_s[:, rows])
            cumr_s[d, :, rows] = _dot(r3[0], tri_r[d]) + _dot(r3[1], tri_r[d]) + _dot(r3[2], tri_r[d])

    ones_c = jnp.ones((c_len, LANE), BF16)

    def lanes2(x):
        return jnp.concatenate([x, x], axis=1)

    def matmuls(d, h, row0):
        rows = pl.ds(row0, c_len)
        hs = slice(dh * h, dh * (h + 1))
        q = qc_s[rows, hs]
        s_raw = _dot_nt(q, kc_s[rows, hs].astype(BF16))
        qc = _dot(q, c_s[d, h].astype(BF16))
        if reduce_on_mxu:
            n_rep = jnp.broadcast_to(n_s[d, h], (LANE, dh)).astype(BF16)
            qn = _dot_nt(q, n_rep)
        else:
            qn = jnp.sum(q.astype(F32) * n_s[d, h], axis=1, keepdims=True)
        return dict(d=d, h=h, rows=rows, row0=row0, hs=hs, s_raw=s_raw, qc=qc, qn=qn)

    def gates(w):
        d, h, rows, row0 = w["d"], w["h"], w["rows"], w["row0"]
        cf = COL_F[d] + h
        end = c_len - 1 if d == 0 else 0
        cum_col = cumc_s[d, rows, pl.ds(cf, 1)]
        cum_c = jnp.broadcast_to(cum_col, (c_len, LANE))
        b_c = jnp.broadcast_to(zs_ref[rows, pl.ds(COL_I[d] + h, 1)] - cum_col, (c_len, LANE))
        g_end = end - end % SUBLANE
        cum_end = cumc_s[d, pl.ds(_aligned(row0 + g_end, SUBLANE), SUBLANE), pl.ds(cf, 1)][
            end - g_end:end - g_end + 1, :]
        b_r = lft_s[pl.ds(ROW_I[d] + h, 1), rows] - cumr_s[d, pl.ds(ROW_F[d] + h, 1), rows]
        m_prev = m_s[d, h]
        b_m = jnp.where(causal[d], b_r, -jnp.inf)
        m_rel = jnp.maximum(m_prev, jnp.broadcast_to(jnp.max(b_m, axis=1, keepdims=True), (c_len, LANE)))
        w["dmat"] = jnp.exp(b_m - m_rel)
        w["inter"] = jnp.exp(m_prev - m_rel)
        w["floor"] = jnp.exp(-(cum_c + m_rel))
        log_w = cum_end + b_c
        m_new = jnp.maximum(cum_end + m_prev, jnp.max(log_w, axis=0, keepdims=True))
        w["wgt"] = jnp.exp(log_w - m_new)
        w["decay"] = jnp.exp(cum_end + m_prev - m_new)
        w["m_new"] = m_new

    def body(j, carry):
        row0 = (_aligned(j * c_len, c_len), _aligned((n_chunks - 1 - j) * c_len, c_len))
        work = [matmuls(d, h, row0[d]) for d in range(N_DIR) for h in range(H_MLSTM)]
        hook()
        for w in work:
            gates(w)
        for w in work:
            s = w["s_raw"] * w["dmat"]
            s_hi = s.astype(BF16)
            w["sv"] = _dot(s_hi, v_ref[w["rows"], w["hs"]])
            if reduce_on_mxu:
                s_lo = (s - s_hi.astype(F32)).astype(BF16)
                w["rsum"] = _dot(s_hi, ones_c) + _dot(s_lo, ones_c)
            else:
                w["rsum"] = jnp.sum(s, axis=1, keepdims=True)
        hook()
        for w in work:
            den = w["rsum"] + w["inter"] * w["qn"]
            rn = 1.0 / jnp.maximum(jnp.abs(den), w["floor"])
            hacc_s[w["rows"], w["hs"]] += w["sv"] * lanes2(rn) + w["qc"] * lanes2(w["inter"] * rn)
        for w in work:
            d, h = w["d"], w["h"]
            kw = kc_s[w["rows"], w["hs"]] * lanes2(w["wgt"])
            decay = w["decay"]
            c_s[d, h] = decay[:, :1] * c_s[d, h] + _dot_tn(kw.astype(BF16), v_ref[w["rows"], w["hs"]])
            n_s[d, h] = lanes2(decay) * n_s[d, h] + jnp.sum(kw, axis=0, keepdims=True)
            m_s[d, h] = w["m_new"]
        return carry

    if unrolled:
        for j in range(n_chunks):
            body(j, 0)
    else:
        lax.fori_loop(0, n_chunks, body, 0)
    flush()

    if emit_state:
        for d in range(N_DIR):
            for h in range(H_MLSTM):
                cout_ref[d, h] = c_s[d, h]
                nout_ref[d, pl.ds(h, 1), :] = n_s[d, h]
                mout_ref[pl.ds(d, 1), pl.ds(h, 1)] = m_s[d, h][:, :1]


def _mlstm_call(zb, zs, zst, wconv, bconv, bcol, brow, states, carried, *, layer, n_seq, seq_len,
                emit_state, reduce_on_mxu):
    has_state = states is not None
    t = seq_len
    dh = DH_MLSTM
    n_carried = 0 if carried is None else 3
    kern = functools.partial(_mlstm_kernel, seq_len=t, has_state=has_state, emit_state=emit_state,
                             n_carried=n_carried, reduce_on_mxu=reduce_on_mxu)
    in_specs = [
        pl.BlockSpec((t, W_MLSTM), lambda b: (b, SLAB_Q_MLSTM)),
        pl.BlockSpec((t, W_MLSTM), lambda b: (b, SLAB_K_MLSTM)),
        pl.BlockSpec((t, W_MLSTM), lambda b: (b, SLAB_V_MLSTM)),
        pl.BlockSpec((t, Z_SMALL), lambda b: (b, 0)),
        pl.BlockSpec((16, t), lambda b: (0, b)),
        pl.BlockSpec((None, 3, 2 * W_MLSTM), lambda b: (layer, 0, 0)),
        pl.BlockSpec((None, 1, 2 * W_MLSTM), lambda b: (layer, 0, 0)),
        pl.BlockSpec((None, 1, Z_SMALL), lambda b: (layer, 0, 0)),
        pl.BlockSpec((None, 16, 1), lambda b: (layer, 0, 0)),
    ]
    args = [zb, zb, zb, zs, zst, wconv, bconv, bcol, brow]
    if has_state:
        c0, n0, m0 = states
        in_specs += [
            pl.BlockSpec((None, None, N_DIR, H_MLSTM, dh, dh), lambda b: (b, layer, 0, 0, 0, 0)),
            pl.BlockSpec((None, None, N_DIR, H_MLSTM, dh), lambda b: (b, layer, 0, 0, 0)),
            pl.BlockSpec((None, None, N_DIR, H_MLSTM), lambda b: (b, layer, 0, 0)),
        ]
        args += [c0, n0, m0]
    aliases = {}
    for k in range(n_carried):
        aliases[len(args)] = 1 + k
        in_specs.append(pl.BlockSpec(memory_space=pl.ANY))
        args.append(carried[k])
    out_specs = [pl.BlockSpec((t, W_MLSTM), lambda b: (b, 0))]
    out_shape = [jax.ShapeDtypeStruct((n_seq * t, W_MLSTM), F32)]
    if emit_state:
        out_specs += [
            pl.BlockSpec((None, None, N_DIR, H_MLSTM, dh, dh), lambda b: (b, layer, 0, 0, 0, 0)),
            pl.BlockSpec((None, None, N_DIR, H_MLSTM, dh), lambda b: (b, layer, 0, 0, 0)),
            pl.BlockSpec((None, None, N_DIR, H_MLSTM), lambda b: (b, layer, 0, 0)),
        ]
        out_shape += [
            jax.ShapeDtypeStruct((n_seq, DEPTH, N_DIR, H_MLSTM, dh, dh), F32),
            jax.ShapeDtypeStruct((n_seq, DEPTH, N_DIR, H_MLSTM, dh), F32),
            jax.ShapeDtypeStruct((n_seq, DEPTH, N_DIR, H_MLSTM), F32),
        ]
    scratch = [
        pltpu.VMEM((t, W_MLSTM), BF16),
        pltpu.VMEM((t, W_MLSTM), F32),
        pltpu.VMEM((t + 16, dh), F32),
        pltpu.VMEM((t, Z_SMALL), F32),
        pltpu.VMEM((16, t), F32),
        pltpu.VMEM((N_DIR, H_MLSTM, dh, dh), F32),
        pltpu.VMEM((N_DIR, H_MLSTM, 1, dh), F32),
        pltpu.VMEM((N_DIR, H_MLSTM, 1, LANE), F32),
        pltpu.VMEM((N_DIR, t, Z_SMALL), F32),
        pltpu.VMEM((N_DIR, 16, t), F32),
    ]
    return dict(kernel=kern, grid=(n_seq,), in_specs=in_specs, args=args, out_specs=out_specs,
                out_shape=out_shape, scratch=scratch, aliases=aliases, name="mlstm_scan")


def _head_rms(o, g, n_heads, width):
    out = []
    for h in range(n_heads):
        oh = o[:, width * h:width * (h + 1)]
        out.append(oh * lax.rsqrt(jnp.mean(oh * oh, axis=-1, keepdims=True) + NORM_EPS) * g)
    return out


def _mix_kernel(x_ref, og_ref, hm_ref, rg_ref, om_ref, gg_ref, gm_ref, mod_ref, gng_ref, gnm_ref,
                wbg_ref, wbm_ref, wo_ref, lg_ref, lb_ref, o_ref, *, cond_row0, tiles_per_cond):
    r = _cond_row(pl.program_id(0), cond_row0, tiles_per_cond)
    gate = mod_ref[2, pl.ds(r, 1), :]
    o_g = jnp.concatenate(_head_rms(og_ref[...] * (DK_GLA ** -0.5), gng_ref[...], H_GLA, DV_GLA), axis=1)
    a_g = (o_g * rg_ref[...].astype(F32)).astype(BF16)
    h_m = jnp.concatenate(_head_rms(hm_ref[...], gnm_ref[...], H_MLSTM, DH_MLSTM), axis=1)
    a_m = (om_ref[...].astype(F32) * h_m).astype(BF16)
    y_g = _dot(a_g, wbg_ref[...])
    y_m = _dot(a_m, wbm_ref[...])
    mix = gg_ref[...].astype(F32) * y_g + gm_ref[...].astype(F32) * y_m
    y = _dot(mix.astype(BF16), wo_ref[...])
    o_ref[...] = _layer_norm(ALPHA * x_ref[...] + gate * y, lg_ref[...], lb_ref[...])


def _mix_call(x, og, hm, zb, ada4, gn_gla, gn_ml, wbg, wbm, wo, lng, lnb, *, layer, cond_row0, rows_per_cond):
    m = x.shape[0]
    tm = MIX_TM
    kern = functools.partial(_mix_kernel, cond_row0=cond_row0, tiles_per_cond=rows_per_cond // tm)
    wspec = pl.BlockSpec((None, D_MODEL, D_MODEL), lambda i: (layer, 0, 0), pipeline_mode=pl.Buffered(1))
    return dict(
        kernel=kern, grid=(m // tm,),
        in_specs=[
            pl.BlockSpec((tm, D_MODEL), lambda i: (i, 0)),
            pl.BlockSpec((tm, V_GLA), lambda i: (i, 0)),
            pl.BlockSpec((tm, W_MLSTM), lambda i: (i, 0)),
            pl.BlockSpec((tm, V_GLA), lambda i: (i, SLAB_R_GLA)),
            pl.BlockSpec((tm, W_MLSTM), lambda i: (i, SLAB_O_MLSTM)),
            pl.BlockSpec((tm, D_MODEL), lambda i: (i, SLAB_G_GLA)),
            pl.BlockSpec((tm, D_MODEL), lambda i: (i, SLAB_G_MLSTM)),
            pl.BlockSpec((None, 3, 16, D_MODEL), lambda i: (layer, 1, 0, 0)),
            pl.BlockSpec((None, 1, DV_GLA), lambda i: (layer, 0, 0)),
            pl.BlockSpec((None, 1, DH_MLSTM), lambda i: (layer, 0, 0)),
            wspec, wspec, wspec,
            pl.BlockSpec((None, 1, D_MODEL), lambda i: (3 * layer + 1, 0, 0)),
            pl.BlockSpec((None, 1, D_MODEL), lambda i: (3 * layer + 1, 0, 0)),
        ],
        args=[x, og, hm, zb, zb, zb, zb, ada4, gn_gla, gn_ml, wbg, wbm, wo, lng, lnb],
        out_specs=[pl.BlockSpec((tm, D_MODEL), lambda i: (i, 0))],
        out_shape=[jax.ShapeDtypeStruct((m, D_MODEL), F32)],
        scratch=[], aliases={}, name="mixer_out_ln")


def _grid_pos_embed(t_len):
    rows = t_len // GRID_W
    r = jnp.repeat(jnp.arange(rows), GRID_W).astype(F32)
    col = jnp.tile(jnp.arange(GRID_W), rows).astype(F32)
    nf = D_MODEL // 4
    omega = 1.0 / (10000.0 ** (jnp.arange(nf, dtype=F32) / nf))
    er = r[:, None] * omega
    ec = col[:, None] * omega
    return jnp.concatenate([jnp.sin(er), jnp.cos(er), jnp.sin(ec), jnp.cos(ec)], axis=-1)


def _pack_ffn(w_gate, w_up, w_down):
    return w_gate.astype(BF16), w_up.astype(BF16), w_down.astype(BF16)


def _pack_w_in(w_in):
    o = np.cumsum((0, QK_GLA, QK_GLA, V_GLA, V_GLA, GLA_RANK, GLA_RANK, 2 * W_MLSTM, W_MLSTM, W_MLSTM,
                   H_MLSTM, H_MLSTM, H_MLSTM, H_MLSTM, D_MODEL, D_MODEL))
    o = [int(v) for v in o]
    w16 = w_in.astype(BF16)
    big = (w16[:, :, o[0]:o[4]], w16[:, :, o[6]:o[9]], w16[:, :, o[13]:o[15]])
    small = jnp.concatenate([w16[:, :, o[4]:o[6]], w16[:, :, o[9]:o[13]]], axis=-1)
    small_p = jnp.pad(small, ((0, 0), (0, 0), (0, Z_SMALL - small.shape[-1])))
    return big, small_p


def kernel(x_prompt, x_sample, c, state_gla_s, state_mlstm_c, state_mlstm_n, state_mlstm_m, c_ctx,
           w_ada, b_ada, ffn1_w_gate, ffn1_w_up, ffn1_w_down, w_in, w_decay, b_decay, w_conv, b_conv,
           f_bias, gla_norm_g, mlstm_norm_g, w_br_gla, w_br_mlstm, w_out,
           ffn2_w_gate, ffn2_w_up, ffn2_w_down, ln_g, ln_b):
    bp, tp, _ = x_prompt.shape
    bs, ts, _ = x_sample.shape
    assert bs + 1 <= 16 and tp % FIN_ROWS == 0 and ts % FIN_ROWS == 0

    cond16 = jnp.zeros((16, D_MODEL), F32).at[0].set(c_ctx).at[1:1 + bs].set(c)
    ada4 = _ada_call(cond16, w_ada, b_ada)

    ffn1 = _pack_ffn(ffn1_w_gate, ffn1_w_up, ffn1_w_down)
    ffn2 = _pack_ffn(ffn2_w_gate, ffn2_w_up, ffn2_w_down)
    w_big, w_small = _pack_w_in(w_in)
    wdec = jnp.zeros((DEPTH, N_DIR, Z_SMALL, QK_GLA), F32)
    wdec = wdec.at[:, 0, 0:GLA_RANK].set(w_decay[:, 0]).at[:, 1, GLA_RANK:2 * GLA_RANK].set(w_decay[:, 1])
    bdec = b_decay.reshape(DEPTH, N_DIR, 1, QK_GLA)
    bcol = jnp.zeros((DEPTH, 1, Z_SMALL), F32)
    brow = jnp.zeros((DEPTH, 16, 1), F32)
    for d in range(N_DIR):
        bcol = bcol.at[:, 0, COL_F[d]:COL_F[d] + H_MLSTM].set(f_bias[:, d])
        brow = brow.at[:, ROW_F[d]:ROW_F[d] + H_MLSTM, 0].set(f_bias[:, d])
    bconv = b_conv.reshape(DEPTH, 1, 2 * W_MLSTM)
    gn_gla = gla_norm_g.reshape(DEPTH, 1, DV_GLA)
    gn_ml = mlstm_norm_g.reshape(DEPTH, 1, DH_MLSTM)
    wbg, wbm, wo = w_br_gla.astype(BF16), w_br_mlstm.astype(BF16), w_out.astype(BF16)
    lng = ln_g.reshape(DEPTH * 3, 1, D_MODEL)
    lnb = ln_b.reshape(DEPTH * 3, 1, D_MODEL)
    pos = _grid_pos_embed(ts)

    m_ctx, m_dec = bp * tp, bs * ts
    ctx = dict(cond_row0=0, rows_per_cond=m_ctx)
    dec = dict(cond_row0=1, rows_per_cond=ts)
    dec_states = (state_gla_s, state_mlstm_c, state_mlstm_n, state_mlstm_m)
    fuse = (m_dec == bp * FUSE_TM and ts % FUSE_TM == 0
            and tp // GLA_CHUNK <= MAX_UNROLLED_CHUNKS and tp // MLSTM_CHUNK <= MAX_UNROLLED_CHUNKS)

    def ffn(x, pos_embed, weights, layer, sub, where, **kw):
        return _ffn_call(x, pos_embed, ada4, *weights, lng, lnb, layer=layer, sub=sub, **where, **kw)

    def proj(x, layer, where, **kw):
        return _proj_call(x, ada4, w_big, w_small, layer=layer, **where, **kw)

    def gla(z, layer, state, carried, n_seq, seq_len):
        return _gla_call(z[0], z[1], wdec, bdec, state, carried, layer=layer, n_seq=n_seq, seq_len=seq_len,
                         emit_state=state is None)

    def mlstm(z, layer, states, carried, n_seq, seq_len, reduce_on_mxu):
        return _mlstm_call(z[0], z[1], z[2], w_conv, bconv, bcol, brow, states, carried, layer=layer,
                           n_seq=n_seq, seq_len=seq_len, emit_state=states is None,
                           reduce_on_mxu=reduce_on_mxu)

    def mix(x, og, hm, z, layer, where):
        return _run(_mix_call(x, og, hm, z[0], ada4, gn_gla, gn_ml, wbg, wbm, wo, lng, lnb,
                              layer=layer, **where))[0]

    xc = x_prompt.reshape(m_ctx, D_MODEL)
    xd = x_sample.reshape(m_dec, D_MODEL)
    carried = None
    for l in range(DEPTH):
        xc = _run(ffn(xc, None, ffn1, l, 0, ctx))[0]
        zc = _run(proj(xc, l, ctx))
        ml_c = mlstm(zc, l, None, None if carried is None else carried[1:], bp, tp, reduce_on_mxu=not fuse)
        gl_c = gla(zc, l, None, None if carried is None else carried[0], bp, tp)
        pos_l = pos if l == 0 else None
        if fuse:
            ml_out, (xd,) = _run_fused(ml_c, ffn(xd, pos_l, ffn1, l, 0, dec, tm=FUSE_TM), "mlstm_scan_ffn")
            gl_out, zd = _run_fused(gl_c, proj(xd, l, dec, tm=FUSE_TM), "gla_scan_in_proj")
        else:
            ml_out, gl_out = _run(ml_c), _run(gl_c)
            xd = _run(ffn(xd, pos_l, ffn1, l, 0, dec))[0]
            zd = _run(proj(xd, l, dec))
        carried = (gl_out[1], ml_out[1], ml_out[2], ml_out[3])
        xc = mix(xc, gl_out[0], ml_out[0], zc, l, ctx)
        xc = _run(ffn(xc, None, ffn2, l, 2, ctx))[0]

        og_d = _run(gla(zd, l, dec_states[0], None, bs, ts))[0]
        hm_d = _run(mlstm(zd, l, dec_states[1:], None, bs, ts, reduce_on_mxu=True))[0]
        xd = mix(xd, og_d, hm_d, zd, l, dec)
        xd = _run(ffn(xd, None, ffn2, l, 2, dec))[0]

    return (xc.reshape(bp, tp, D_MODEL), xd.reshape(bs, ts, D_MODEL)) + tuple(carried)
```

```python
import functools

import jax
import jax.numpy as jnp
import numpy as np
from jax import lax
from jax.experimental import pallas as pl
from jax.experimental.pallas import tpu as pltpu

F32 = jnp.float32
BF16 = jnp.bfloat16

D_MODEL = 1024
DEPTH = 2
GRID_W = 64
N_DIR = 2
H_GLA = 4
DK_GLA = 128
DV_GLA = 256
GLA_RANK = 16
GLA_TAU = 16.0
H_MLSTM = 4
DH_MLSTM = 256
D_FF = 2816
N_MOD = 9
LN_EPS = 1e-5
NORM_EPS = 1e-6
QK_GLA = H_GLA * DK_GLA
V_GLA = H_GLA * DV_GLA
W_MLSTM = H_MLSTM * DH_MLSTM
ALPHA = (2.0 * DEPTH) ** 0.25

Z_BIG = 9 * 1024
Z_SMALL = 128
SLAB_V_GLA, SLAB_R_GLA, SLAB_Q_MLSTM, SLAB_K_MLSTM, SLAB_V_MLSTM = 1, 2, 3, 4, 5
SLAB_O_MLSTM, SLAB_G_GLA, SLAB_G_MLSTM = 6, 7, 8
W_SPLIT = (3, 7)
COL_I = (32, 40)
COL_F = (36, 44)
ROW_I = (0, 8)
ROW_F = (4, 12)

LANE = 128
SUBLANE = 8
VMEM_LIMIT = 56 * 1024 * 1024

FFN_TM = 1024
FFN_SUB = 512
FFN_CHUNKS = ((0, 1536), (1536, 1280))
PROJ_TM = 512
PROJ_TN = 1024
MIX_TM = 512
MIX_SUB = 256
FUSE_TM = 256
GLA_CHUNK = 64
GLA_SUB = 16
MLSTM_CHUNK = 128
FIN_ROWS = 256
MAX_UNROLLED_CHUNKS = 4


def _dot(a, b):
    return jnp.dot(a, b, preferred_element_type=F32)


def _dot_nt(a, b):
    return lax.dot_general(a, b, (((1,), (1,)), ((), ())), preferred_element_type=F32)


def _dot_tn(a, b):
    return lax.dot_general(a, b, (((0,), (0,)), ((), ())), preferred_element_type=F32)


def _split3(x):
    hi = x.astype(BF16)
    r = x - hi.astype(F32)
    mid = r.astype(BF16)
    lo = (r - mid.astype(F32)).astype(BF16)
    return hi, mid, lo


def _split2(x):
    hi = x.astype(BF16)
    return hi, (x - hi.astype(F32)).astype(BF16)


def _dot_acc(x, w):
    xh, xm, _ = _split3(x)
    wh, wl = _split2(w)
    return _dot(xh, wh) + _dot(xh, wl) + _dot(xm, wh)


def _silu(x):
    return x * jax.nn.sigmoid(x)


def _log_sigmoid(x):
    return jnp.minimum(x, 0.0) - jnp.log(1.0 + jnp.exp(-jnp.abs(x)))


def _layer_norm(y, g, b):
    mu = jnp.mean(y, axis=-1, keepdims=True)
    d = y - mu
    var = jnp.mean(d * d, axis=-1, keepdims=True)
    return d * lax.rsqrt(var + LN_EPS) * g + b


def _aligned(x, m):
    return x if isinstance(x, int) else pl.multiple_of(x, m)


def _cparams(n_axes):
    return pltpu.CompilerParams(
        dimension_semantics=("arbitrary",) * n_axes, vmem_limit_bytes=VMEM_LIMIT)


def _cond_row(i, cond_row0, tiles_per_cond):
    return cond_row0 + lax.div(i, jnp.int32(tiles_per_cond))


def _run(call):
    return pl.pallas_call(
        call["kernel"],
        grid=call["grid"],
        in_specs=call["in_specs"],
        out_specs=call["out_specs"],
        out_shape=call["out_shape"],
        scratch_shapes=call["scratch"],
        input_output_aliases=call["aliases"],
        compiler_params=_cparams(len(call["grid"])),
        name=call["name"],
    )(*call["args"])


def _run_fused(scan, mm, name):
    assert scan["grid"] == mm["grid"] and not mm["aliases"]
    counts = [len(scan["args"]), len(mm["args"]), len(scan["out_shape"]), len(mm["out_shape"]),
              len(scan["scratch"]), len(mm["scratch"])]

    def kern(*refs):
        parts, i = [], 0
        for n in counts:
            parts.append(tuple(refs[i:i + n]))
            i += n
        in_a, in_b, out_a, out_b, scr_a, scr_b = parts
        scan["kernel"](*(in_a + out_a + scr_a), hooks=mm["pieces"](in_b + out_b + scr_b))

    outs = pl.pallas_call(
        kern,
        grid=scan["grid"],
        in_specs=scan["in_specs"] + mm["in_specs"],
        out_specs=scan["out_specs"] + mm["out_specs"],
        out_shape=scan["out_shape"] + mm["out_shape"],
        scratch_shapes=scan["scratch"] + mm["scratch"],
        input_output_aliases=scan["aliases"],
        compiler_params=_cparams(len(scan["grid"])),
        name=name,
    )(*scan["args"], *mm["args"])
    n_a = counts[2]
    return outs[:n_a], outs[n_a:]


def _hook_runner(hooks):
    pending = list(hooks)

    def hook(n=1):
        for _ in range(n):
            if pending:
                pending.pop(0)()

    def flush():
        while pending:
            pending.pop(0)()

    return hook, flush


def _ada_kernel(c_ref, w_ref, b_ref, o_ref):
    o_ref[...] = _dot_acc(_silu(c_ref[...]), w_ref[...]) + b_ref[...]


def _ada_call(cond16, w_ada, b_ada):
    return pl.pallas_call(
        _ada_kernel,
        grid=(DEPTH, N_MOD),
        in_specs=[
            pl.BlockSpec((16, D_MODEL), lambda l, j: (0, 0)),
            pl.BlockSpec((None, D_MODEL, D_MODEL), lambda l, j: (l, 0, j)),
            pl.BlockSpec((None, None, 1, D_MODEL), lambda l, j: (l, j, 0, 0)),
        ],
        out_specs=pl.BlockSpec((None, None, 16, D_MODEL), lambda l, j: (l, j, 0, 0)),
        out_shape=jax.ShapeDtypeStruct((DEPTH, N_MOD, 16, D_MODEL), F32),
        compiler_params=_cparams(2),
        name="ada_mod",
    )(cond16, w_ada, b_ada.reshape(DEPTH, N_MOD, 1, D_MODEL))


def _ffn_pieces(refs, *, has_pos, cond_row0, tiles_per_cond, tm, sub):
    if has_pos:
        x_ref, pos_ref, mod_ref, wg_ref, wu_ref, wd_ref, lg_ref, lb_ref, o_ref = refs
    else:
        x_ref, mod_ref, wg_ref, wu_ref, wd_ref, lg_ref, lb_ref, o_ref = refs
        pos_ref = None
    st = {}
    pieces = []

    def start(s):
        rows = pl.ds(s * sub, sub)
        if s == 0:
            r = _cond_row(pl.program_id(0), cond_row0, tiles_per_cond)
            st["sh"] = mod_ref[0, pl.ds(r, 1), :]
            st["sc"] = mod_ref[1, pl.ds(r, 1), :]
            st["gate"] = mod_ref[2, pl.ds(r, 1), :]
        x = x_ref[rows, :]
        if has_pos:
            x = x + pos_ref[rows, :]
        st["x"] = x
        st["hm"] = (x * (1.0 + st["sc"]) + st["sh"]).astype(BF16)
        st["acc"] = None

    def gate_mm(c0, cw):
        st["g"] = _dot(st["hm"], wg_ref[:, c0:c0 + cw])

    def up_mm(c0, cw):
        st["u"] = _dot(st["hm"], wu_ref[:, c0:c0 + cw])

    def down_mm(c0, cw):
        part = _dot((_silu(st["g"]) * st["u"]).astype(BF16), wd_ref[c0:c0 + cw, :])
        st["acc"] = part if st["acc"] is None else st["acc"] + part

    def finish(s):
        y = ALPHA * st["x"] + (0.5 * st["gate"]) * st["acc"]
        o_ref[pl.ds(s * sub, sub), :] = _layer_norm(y, lg_ref[...], lb_ref[...])

    for s in range(tm // sub):
        pieces.append(functools.partial(start, s))
        for c0, cw in FFN_CHUNKS:
            pieces += [functools.partial(gate_mm, c0, cw), functools.partial(up_mm, c0, cw),
                       functools.partial(down_mm, c0, cw)]
        pieces.append(functools.partial(finish, s))
    return pieces


def _ffn_call(x, pos, ada4, wg, wu, wd, lng, lnb, *, layer, sub, cond_row0, rows_per_cond, tm=FFN_TM):
    m = x.shape[0]
    has_pos = pos is not None
    static = dict(has_pos=has_pos, cond_row0=cond_row0, tiles_per_cond=rows_per_cond // tm,
                  tm=tm, sub=min(tm, FFN_SUB))
    pieces = functools.partial(_ffn_pieces, **static)

    def kern(*refs):
        for p in pieces(refs):
            p()

    once = pl.Buffered(1)
    in_specs = [pl.BlockSpec((tm, D_MODEL), lambda i: (i, 0))]
    args = [x]
    if has_pos:
        n_pos = pos.shape[0] // tm
        if n_pos == 1:
            in_specs.append(pl.BlockSpec((tm, D_MODEL), lambda i: (0, 0), pipeline_mode=once))
        else:
            in_specs.append(pl.BlockSpec((tm, D_MODEL), lambda i: (lax.rem(i, jnp.int32(n_pos)), 0)))
        args.append(pos)
    in_specs += [
        pl.BlockSpec((None, 3, 16, D_MODEL), lambda i: (layer, sub, 0, 0)),
        pl.BlockSpec((None, D_MODEL, D_FF), lambda i: (layer, 0, 0), pipeline_mode=once),
        pl.BlockSpec((None, D_MODEL, D_FF), lambda i: (layer, 0, 0), pipeline_mode=once),
        pl.BlockSpec((None, D_FF, D_MODEL), lambda i: (layer, 0, 0), pipeline_mode=once),
        pl.BlockSpec((None, 1, D_MODEL), lambda i: (3 * layer + sub, 0, 0)),
        pl.BlockSpec((None, 1, D_MODEL), lambda i: (3 * layer + sub, 0, 0)),
    ]
    args += [ada4, wg, wu, wd, lng, lnb]
    return dict(kernel=kern, pieces=pieces, grid=(m // tm,), in_specs=in_specs, args=args,
                out_specs=[pl.BlockSpec((tm, D_MODEL), lambda i: (i, 0))],
                out_shape=[jax.ShapeDtypeStruct((m, D_MODEL), F32)],
                scratch=[], aliases={}, name="ffn_ln")


def _proj_pieces(refs, *, cond_row0, tiles_per_cond):
    x_ref, mod_ref, wa_ref, wb_ref, wc_ref, ws_ref, zb_ref, zs_ref, zst_ref = refs
    st = {}

    def weight(n):
        for ref, first in ((wa_ref, 0), (wb_ref, W_SPLIT[0]), (wc_ref, W_SPLIT[1])):
            local = n - first
            if 0 <= local * PROJ_TN < ref.shape[1]:
                return ref[:, local * PROJ_TN:(local + 1) * PROJ_TN]
        raise ValueError(n)

    def start():
        r = _cond_row(pl.program_id(0), cond_row0, tiles_per_cond)
        sh = mod_ref[0, pl.ds(r, 1), :]
        sc = mod_ref[1, pl.ds(r, 1), :]
        st["hm"] = (x_ref[...] * (1.0 + sc) + sh).astype(BF16)
        zs = _dot(st["hm"], ws_ref[...])
        zs_ref[...] = zs
        zst_ref[...] = zs.T[COL_I[0]:COL_I[0] + 16, :]

    def slab(n):
        cols = slice(n * PROJ_TN, (n + 1) * PROJ_TN)
        z = _dot(st["hm"], weight(n))
        if n == SLAB_R_GLA:
            z = _silu(z)
        elif n in (SLAB_O_MLSTM, SLAB_G_GLA, SLAB_G_MLSTM):
            z = jax.nn.sigmoid(z)
        zb_ref[:, cols] = z.astype(BF16)

    return [start] + [functools.partial(slab, n) for n in range(Z_BIG // PROJ_TN)]


def _proj_call(x, ada4, w_big, w_small, *, layer, cond_row0, rows_per_cond, tm=PROJ_TM):
    m = x.shape[0]
    w_a, w_b, w_c = w_big
    assert (W_SPLIT[0] * PROJ_TN + w_b.shape[-1] + w_c.shape[-1] == Z_BIG
            and w_b.shape[-1] == (W_SPLIT[1] - W_SPLIT[0]) * PROJ_TN)
    pieces = functools.partial(_proj_pieces, cond_row0=cond_row0, tiles_per_cond=rows_per_cond // tm)

    def kern(*refs):
        for p in pieces(refs):
            p()

    once = pl.Buffered(1)
    return dict(
        kernel=kern, pieces=pieces, grid=(m // tm,),
        in_specs=[
            pl.BlockSpec((tm, D_MODEL), lambda i: (i, 0)),
            pl.BlockSpec((None, 3, 16, D_MODEL), lambda i: (layer, 1, 0, 0)),
            pl.BlockSpec((None, D_MODEL, W_SPLIT[0] * PROJ_TN), lambda i: (layer, 0, 0), pipeline_mode=once),
            pl.BlockSpec((None, D_MODEL, w_b.shape[-1]), lambda i: (layer, 0, 0), pipeline_mode=once),
            pl.BlockSpec((None, D_MODEL, w_c.shape[-1]), lambda i: (layer, 0, 0), pipeline_mode=once),
            pl.BlockSpec((None, D_MODEL, Z_SMALL), lambda i: (layer, 0, 0), pipeline_mode=once),
        ],
        args=[x, ada4, w_a, w_b, w_c, w_small],
        out_specs=[
            pl.BlockSpec((tm, Z_BIG), lambda i: (i, 0)),
            pl.BlockSpec((tm, Z_SMALL), lambda i: (i, 0)),
            pl.BlockSpec((16, tm), lambda i: (0, i)),
        ],
        out_shape=[
            jax.ShapeDtypeStruct((m, Z_BIG), BF16),
            jax.ShapeDtypeStruct((m, Z_SMALL), F32),
            jax.ShapeDtypeStruct((16, m), F32),
        ],
        scratch=[], aliases={}, name="mixer_in_proj")


def _gla_kernel(*refs, seq_len, has_state, emit_state, n_carried, hooks=()):
    it = iter(refs)
    q_ref, k_ref, v_ref, zs_ref, wdec_ref, bdec_ref = (next(it) for _ in range(6))
    s0_ref = next(it) if has_state else None
    if n_carried:
        next(it)
    oacc_s = next(it)
    sout_ref = next(it) if emit_state else None
    la_s, st_s, cum_s = (next(it) for _ in range(3))
    t_len = seq_len
    c_len = GLA_CHUNK
    n_chunks = t_len // c_len
    unrolled = n_chunks <= MAX_UNROLLED_CHUNKS
    assert unrolled or not hooks
    hook, flush = _hook_runner(hooks)
    hook(4)

    for rb in range(t_len // FIN_ROWS):
        rows = pl.ds(rb * FIN_ROWS, FIN_ROWS)
        zh, zm, _ = _split3(zs_ref[rows, :])
        for d in range(N_DIR):
            wh, wl = _split2(wdec_ref[d])
            x = _dot(zh, wh) + _dot(zh, wl) + _dot(zm, wh) + bdec_ref[d]
            la_s[d, rows, :] = _log_sigmoid(x) * (1.0 / GLA_TAU)

    for d in range(N_DIR):
        for h in range(H_GLA):
            if has_state:
                st_s[d, h] = s0_ref[d, h].T
            else:
                st_s[d, h] = jnp.zeros((DV_GLA, DK_GLA), F32)
    oacc_s[...] = jnp.zeros_like(oacc_s)

    ri = lax.broadcasted_iota(jnp.int32, (c_len, c_len), 0)
    ci = lax.broadcasted_iota(jnp.int32, (c_len, c_len), 1)
    rr = lax.broadcasted_iota(jnp.int32, (c_len, DK_GLA), 0)
    odd_rows = (lax.shift_right_logical(rr, 4) & 1) == 1
    upper_rows = rr >= 2 * GLA_SUB
    same_sub = lax.shift_right_logical(ri, 4) == lax.shift_right_logical(ci, 4)
    same_half = lax.shift_right_logical(ri, 5) == lax.shift_right_logical(ci, 5)
    causal = (ri >= ci, ri <= ci)
    tri = tuple(jnp.where(c, 1.0, 0.0).astype(BF16) for c in causal)
    pm0 = tuple(same_sub & c for c in causal)
    q1rows = (odd_rows, jnp.logical_not(odd_rows))
    q2rows = (upper_rows, jnp.logical_not(upper_rows))

    def bcast(x, n):
        return jnp.broadcast_to(x, (n, DK_GLA))

    for d in range(N_DIR):
        for cb in range(n_chunks):
            rows = pl.ds(cb * c_len, c_len)
            l_hi, l_lo = _split2(la_s[d, rows, :])
            cum_s[d, rows, :] = _dot(tri[d], l_hi) + _dot(tri[d], l_lo)
    hook()

    def scores(d, h, row0):
        rows = pl.ds(row0, c_len)
        ls = slice(DK_GLA * h, DK_GLA * (h + 1))
        vs = slice(DV_GLA * h, DV_GLA * (h + 1))

        def row(i):
            g = i - i % SUBLANE
            return cum_s[d, pl.ds(_aligned(row0 + g, SUBLANE), SUBLANE), ls][i - g:i - g + 1, :]

        c = cum_s[d, rows, ls]
        zero = jnp.zeros((GLA_SUB, DK_GLA), F32)
        if d == 0:
            ref0 = jnp.concatenate([zero, bcast(row(15), 16), bcast(row(31), 16), bcast(row(47), 16)], axis=0)
            ref1 = jnp.concatenate([bcast(row(15), 32), bcast(row(47), 32)], axis=0)
            ref2, cend = row(31), row(63)
        else:
            ref0 = jnp.concatenate([bcast(row(16), 16), bcast(row(32), 16), bcast(row(48), 16), zero], axis=0)
            ref1 = jnp.concatenate([bcast(row(16), 32), bcast(row(48), 32)], axis=0)
            ref2, cend = row(32), row(0)
        q = q_ref[rows, ls].astype(F32)
        k = k_ref[rows, ls].astype(F32)
        e0 = c - ref0
        p0 = _dot_nt((q * jnp.exp(e0)).astype(BF16), (k * jnp.exp(-e0)).astype(BF16))
        x1 = jnp.exp(-jnp.abs(c - ref1))
        p1 = _dot_nt(jnp.where(q1rows[d], q * x1, 0.0).astype(BF16),
                     jnp.where(q1rows[d], 0.0, k * x1).astype(BF16))
        x2 = jnp.exp(-jnp.abs(c - ref2))
        p2 = _dot_nt(jnp.where(q2rows[d], q * x2, 0.0).astype(BF16),
                     jnp.where(q2rows[d], 0.0, k * x2).astype(BF16))
        qi = (q * jnp.exp(c)).astype(BF16)
        kst = (k * jnp.exp(cend - c)).astype(BF16)
        return dict(d=d, h=h, rows=rows, vs=vs, p0=p0, p1=p1, p2=p2, qi=qi, kst=kst, dec=jnp.exp(cend))

    def chunk_scores(j):
        row0 = (_aligned(j * c_len, c_len), _aligned((n_chunks - 1 - j) * c_len, c_len))
        return [scores(d, h, row0[d]) for d in range(N_DIR) for h in range(H_GLA)]

    def chunk_update(work):
        for w in work:
            d, h = w["d"], w["h"]
            p = jnp.where(pm0[d], w["p0"], jnp.where(same_half, w["p1"], w["p2"]))
            o = _dot(p.astype(BF16), v_ref[w["rows"], w["vs"]]) + _dot_nt(w["qi"], st_s[d, h].astype(BF16))
            oacc_s[w["rows"], w["vs"]] += o
        hook()
        for w in work:
            d, h = w["d"], w["h"]
            st_s[d, h] = st_s[d, h] * w["dec"] + _dot_tn(v_ref[w["rows"], w["vs"]], w["kst"])

    def body(j, carry):
        work = chunk_scores(j)
        hook()
        chunk_update(work)
        return carry

    def body2(jj, carry):
        work = [chunk_scores(2 * jj), chunk_scores(2 * jj + 1)]
        chunk_update(work[0])
        chunk_update(work[1])
        return carry

    if unrolled:
        for j in range(n_chunks):
            body(j, 0)
    else:
        lax.fori_loop(0, n_chunks // 2, body2, 0)
    flush()

    if emit_state:
        for d in range(N_DIR):
            for h in range(H_GLA):
                sout_ref[d, h] = st_s[d, h].T


def _gla_call(zb, zs, wdec, bdec, state, carried, *, layer, n_seq, seq_len, emit_state):
    has_state = state is not None
    t = seq_len
    n_carried = 0 if carried is None else 1
    kern = functools.partial(_gla_kernel, seq_len=t, has_state=has_state, emit_state=emit_state,
                             n_carried=n_carried)
    in_specs = [
        pl.BlockSpec((t, QK_GLA), lambda b: (b, 0)),
        pl.BlockSpec((t, QK_GLA), lambda b: (b, 1)),
        pl.BlockSpec((t, V_GLA), lambda b: (b, SLAB_V_GLA)),
        pl.BlockSpec((t, Z_SMALL), lambda b: (b, 0)),
        pl.BlockSpec((None, N_DIR, Z_SMALL, QK_GLA), lambda b: (layer, 0, 0, 0)),
        pl.BlockSpec((None, N_DIR, 1, QK_GLA), lambda b: (layer, 0, 0, 0)),
    ]
    args = [zb, zb, zb, zs, wdec, bdec]
    if has_state:
        in_specs.append(pl.BlockSpec((None, None, N_DIR, H_GLA, DK_GLA, DV_GLA),
                                     lambda b: (b, layer, 0, 0, 0, 0)))
        args.append(state)
    aliases = {}
    if n_carried:
        aliases[len(args)] = 1
        in_specs.append(pl.BlockSpec(memory_space=pl.ANY))
        args.append(carried)
    out_specs = [pl.BlockSpec((t, V_GLA), lambda b: (b, 0))]
    out_shape = [jax.ShapeDtypeStruct((n_seq * t, V_GLA), F32)]
    if emit_state:
        out_specs.append(pl.BlockSpec((None, None, N_DIR, H_GLA, DK_GLA, DV_GLA),
                                      lambda b: (b, layer, 0, 0, 0, 0)))
        out_shape.append(jax.ShapeDtypeStruct((n_seq, DEPTH, N_DIR, H_GLA, DK_GLA, DV_GLA), F32))
    scratch = [
        pltpu.VMEM((N_DIR, t, QK_GLA), F32),
        pltpu.VMEM((N_DIR, H_GLA, DV_GLA, DK_GLA), F32),
        pltpu.VMEM((N_DIR, t, QK_GLA), F32),
    ]
    return dict(kernel=kern, grid=(n_seq,), in_specs=in_specs, args=args, out_specs=out_specs,
                out_shape=out_shape, scratch=scratch, aliases=aliases, name="gla_scan")


def _mlstm_kernel(*refs, seq_len, has_state, emit_state, n_carried, reduce_on_mxu, hooks=()):
    it = iter(refs)
    (q_ref, k_ref, v_ref, zs_ref, zst_ref, wconv_ref, bconv_ref,
     bcol_ref, brow_ref) = (next(it) for _ in range(9))
    if has_state:
        c0_ref, n0_ref, m0_ref = (next(it) for _ in range(3))
    for _ in range(n_carried):
        next(it)
    hacc_s = next(it)
    if emit_state:
        cout_ref, nout_ref, mout_ref = (next(it) for _ in range(3))
    (qc_s, kc_s, xp_s, lf_s, lft_s, c_s, n_s, m_s, cumc_s, cumr_s) = (next(it) for _ in range(10))
    t_len = seq_len
    c_len = MLSTM_CHUNK
    n_chunks = t_len // c_len
    dh = DH_MLSTM
    unrolled = n_chunks <= MAX_UNROLLED_CHUNKS
    assert unrolled or not hooks
    hook, flush = _hook_runner(hooks)
    hook(6)

    xp_s[pl.ds(0, 8), :] = jnp.zeros((8, dh), F32)
    xp_s[pl.ds(8 + t_len, 8), :] = jnp.zeros((8, dh), F32)
    for which, src in enumerate((q_ref, k_ref)):
        for h in range(H_MLSTM):
            hs = slice(dh * h, dh * (h + 1))
            ws = slice(which * W_MLSTM + dh * h, which * W_MLSTM + dh * (h + 1))
            xp_s[pl.ds(8, t_len), :] = src[:, hs].astype(F32)
            w0 = wconv_ref[pl.ds(0, 1), ws]
            w1 = wconv_ref[pl.ds(1, 1), ws]
            w2 = wconv_ref[pl.ds(2, 1), ws]
            bias = bconv_ref[:, ws]
            for rb in range(t_len // FIN_ROWS):
                r0 = rb * FIN_ROWS
                y = (w0 * xp_s[pl.ds(r0 + 7, FIN_ROWS), :] + w1 * xp_s[pl.ds(r0 + 8, FIN_ROWS), :]
                     + w2 * xp_s[pl.ds(r0 + 9, FIN_ROWS), :] + bias)
                a = _silu(y)
                if which == 0:
                    qc_s[pl.ds(r0, FIN_ROWS), hs] = (a * (dh ** -0.5)).astype(BF16)
                else:
                    kc_s[pl.ds(r0, FIN_ROWS), hs] = a

    lf_s[...] = _log_sigmoid(zs_ref[...] + bcol_ref[...])
    zt = zst_ref[...]
    trow = lax.broadcasted_iota(jnp.int32, zt.shape, 0)
    is_f = (lax.shift_right_logical(trow, 2) & 1) == 1
    lft_s[...] = jnp.where(is_f, _log_sigmoid(zt + brow_ref[...]), zt)

    for d in range(N_DIR):
        for h in range(H_MLSTM):
            if has_state:
                c_s[d, h] = c0_ref[d, h]
                n_s[d, h] = n0_ref[d, pl.ds(h, 1), :]
                m_s[d, h] = jnp.broadcast_to(m0_ref[pl.ds(d, 1), pl.ds(h, 1)], (1, LANE))
            else:
                c_s[d, h] = jnp.zeros((dh, dh), F32)
                n_s[d, h] = jnp.zeros((1, dh), F32)
                m_s[d, h] = jnp.zeros((1, LANE), F32)
    hacc_s[...] = jnp.zeros_like(hacc_s)

    ri = lax.broadcasted_iota(jnp.int32, (c_len, c_len), 0)
    ci = lax.broadcasted_iota(jnp.int32, (c_len, c_len), 1)
    causal = (ri >= ci, ri <= ci)
    tri_c = tuple(jnp.where(c, 1.0, 0.0).astype(BF16) for c in causal)
    tri_r = (tri_c[1], tri_c[0])

    for d in range(N_DIR):
        for cb in range(n_chunks):
            rows = pl.ds(cb * c_len, c_len)
            l3 = _split3(lf_s[rows, :])
            cumc_s[d, rows, :] = _dot(tri_c[d], l3[0]) + _dot(tri_c[d], l3[1]) + _dot(tri_c[d], l3[2])
            r3 = _split3(lft_s[:, rows])
            cumr_s[d, :, rows] = _dot(r3[0], tri_r[d]) + _dot(r3[1], tri_r[d]) + _dot(r3[2], tri_r[d])

    ones_c = jnp.ones((c_len, LANE), BF16)

    def lanes2(x):
        return jnp.concatenate([x, x], axis=1)

    def matmuls(d, h, row0):
        rows = pl.ds(row0, c_len)
        hs = slice(dh * h, dh * (h + 1))
        q = qc_s[rows, hs]
        s_raw = _dot_nt(q, kc_s[rows, hs].astype(BF16))
        qc = _dot(q, c_s[d, h].astype(BF16))
        if reduce_on_mxu:
            n_rep = jnp.broadcast_to(n_s[d, h], (LANE, dh)).astype(BF16)
            qn = _dot_nt(q, n_rep)
        else:
            qn = jnp.sum(q.astype(F32) * n_s[d, h], axis=1, keepdims=True)
        return dict(d=d, h=h, rows=rows, row0=row0, hs=hs, s_raw=s_raw, qc=qc, qn=qn)

    def gates(w):
        d, h, rows, row0 = w["d"], w["h"], w["rows"], w["row0"]
        cf = COL_F[d] + h
        end = c_len - 1 if d == 0 else 0
        cum_col = cumc_s[d, rows, pl.ds(cf, 1)]
        cum_c = jnp.broadcast_to(cum_col, (c_len, LANE))
        b_c = jnp.broadcast_to(zs_ref[rows, pl.ds(COL_I[d] + h, 1)] - cum_col, (c_len, LANE))
        g_end = end - end % SUBLANE
        cum_end = cumc_s[d, pl.ds(_aligned(row0 + g_end, SUBLANE), SUBLANE), pl.ds(cf, 1)][
            end - g_end:end - g_end + 1, :]
        b_r = lft_s[pl.ds(ROW_I[d] + h, 1), rows] - cumr_s[d, pl.ds(ROW_F[d] + h, 1), rows]
        m_prev = m_s[d, h]
        b_m = jnp.where(causal[d], b_r, -jnp.inf)
        m_rel = jnp.maximum(m_prev, jnp.broadcast_to(jnp.max(b_m, axis=1, keepdims=True), (c_len, LANE)))
        w["dmat"] = jnp.exp(b_m - m_rel)
        w["inter"] = jnp.exp(m_prev - m_rel)
        w["floor"] = jnp.exp(-(cum_c + m_rel))
        log_w = cum_end + b_c
        m_new = jnp.maximum(cum_end + m_prev, jnp.max(log_w, axis=0, keepdims=True))
        w["wgt"] = jnp.exp(log_w - m_new)
        w["decay"] = jnp.exp(cum_end + m_prev - m_new)
        w["m_new"] = m_new

    def body(j, carry):
        row0 = (_aligned(j * c_len, c_len), _aligned((n_chunks - 1 - j) * c_len, c_len))
        work = [matmuls(d, h, row0[d]) for d in range(N_DIR) for h in range(H_MLSTM)]
        hook()
        for w in work:
            gates(w)
        for w in work:
            s = w["s_raw"] * w["dmat"]
            s_hi = s.astype(BF16)
            w["sv"] = _dot(s_hi, v_ref[w["rows"], w["hs"]])
            if reduce_on_mxu:
                s_lo = (s - s_hi.astype(F32)).astype(BF16)
                w["rsum"] = _dot(s_hi, ones_c) + _dot(s_lo, ones_c)
            else:
                w["rsum"] = jnp.sum(s, axis=1, keepdims=True)
        hook()
        for w in work:
            den = w["rsum"] + w["inter"] * w["qn"]
            rn = 1.0 / jnp.maximum(jnp.abs(den), w["floor"])
            hacc_s[w["rows"], w["hs"]] += w["sv"] * lanes2(rn) + w["qc"] * lanes2(w["inter"] * rn)
        for w in work:
            d, h = w["d"], w["h"]
            kw = kc_s[w["rows"], w["hs"]] * lanes2(w["wgt"])
            decay = w["decay"]
            c_s[d, h] = decay[:, :1] * c_s[d, h] + _dot_tn(kw.astype(BF16), v_ref[w["rows"], w["hs"]])
            n_s[d, h] = lanes2(decay) * n_s[d, h] + jnp.sum(kw, axis=0, keepdims=True)
            m_s[d, h] = w["m_new"]
        return carry

    if unrolled:
        for j in range(n_chunks):
            body(j, 0)
    else:
        lax.fori_loop(0, n_chunks, body, 0)
    flush()

    if emit_state:
        for d in range(N_DIR):
            for h in range(H_MLSTM):
                cout_ref[d, h] = c_s[d, h]
                nout_ref[d, pl.ds(h, 1), :] = n_s[d, h]
                mout_ref[pl.ds(d, 1), pl.ds(h, 1)] = m_s[d, h][:, :1]


def _mlstm_call(zb, zs, zst, wconv, bconv, bcol, brow, states, carried, *, layer, n_seq, seq_len,
                emit_state, reduce_on_mxu):
    has_state = states is not None
    t = seq_len
    dh = DH_MLSTM
    n_carried = 0 if carried is None else 3
    kern = functools.partial(_mlstm_kernel, seq_len=t, has_state=has_state, emit_state=emit_state,
                             n_carried=n_carried, reduce_on_mxu=reduce_on_mxu)
    in_specs = [
        pl.BlockSpec((t, W_MLSTM), lambda b: (b, SLAB_Q_MLSTM)),
        pl.BlockSpec((t, W_MLSTM), lambda b: (b, SLAB_K_MLSTM)),
        pl.BlockSpec((t, W_MLSTM), lambda b: (b, SLAB_V_MLSTM)),
        pl.BlockSpec((t, Z_SMALL), lambda b: (b, 0)),
        pl.BlockSpec((16, t), lambda b: (0, b)),
        pl.BlockSpec((None, 3, 2 * W_MLSTM), lambda b: (layer, 0, 0)),
        pl.BlockSpec((None, 1, 2 * W_MLSTM), lambda b: (layer, 0, 0)),
        pl.BlockSpec((None, 1, Z_SMALL), lambda b: (layer, 0, 0)),
        pl.BlockSpec((None, 16, 1), lambda b: (layer, 0, 0)),
    ]
    args = [zb, zb, zb, zs, zst, wconv, bconv, bcol, brow]
    if has_state:
        c0, n0, m0 = states
        in_specs += [
            pl.BlockSpec((None, None, N_DIR, H_MLSTM, dh, dh), lambda b: (b, layer, 0, 0, 0, 0)),
            pl.BlockSpec((None, None, N_DIR, H_MLSTM, dh), lambda b: (b, layer, 0, 0, 0)),
            pl.BlockSpec((None, None, N_DIR, H_MLSTM), lambda b: (b, layer, 0, 0)),
        ]
        args += [c0, n0, m0]
    aliases = {}
    for k in range(n_carried):
        aliases[len(args)] = 1 + k
        in_specs.append(pl.BlockSpec(memory_space=pl.ANY))
        args.append(carried[k])
    out_specs = [pl.BlockSpec((t, W_MLSTM), lambda b: (b, 0))]
    out_shape = [jax.ShapeDtypeStruct((n_seq * t, W_MLSTM), F32)]
    if emit_state:
        out_specs += [
            pl.BlockSpec((None, None, N_DIR, H_MLSTM, dh, dh), lambda b: (b, layer, 0, 0, 0, 0)),
            pl.BlockSpec((None, None, N_DIR, H_MLSTM, dh), lambda b: (b, layer, 0, 0, 0)),
            pl.BlockSpec((None, None, N_DIR, H_MLSTM), lambda b: (b, layer, 0, 0)),
        ]
        out_shape += [
            jax.ShapeDtypeStruct((n_seq, DEPTH, N_DIR, H_MLSTM, dh, dh), F32),
            jax.ShapeDtypeStruct((n_seq, DEPTH, N_DIR, H_MLSTM, dh), F32),
            jax.ShapeDtypeStruct((n_seq, DEPTH, N_DIR, H_MLSTM), F32),
        ]
    scratch = [
        pltpu.VMEM((t, W_MLSTM), BF16),
        pltpu.VMEM((t, W_MLSTM), F32),
        pltpu.VMEM((t + 16, dh), F32),
        pltpu.VMEM((t, Z_SMALL), F32),
        pltpu.VMEM((16, t), F32),
        pltpu.VMEM((N_DIR, H_MLSTM, dh, dh), F32),
        pltpu.VMEM((N_DIR, H_MLSTM, 1, dh), F32),
        pltpu.VMEM((N_DIR, H_MLSTM, 1, LANE), F32),
        pltpu.VMEM((N_DIR, t, Z_SMALL), F32),
        pltpu.VMEM((N_DIR, 16, t), F32),
    ]
    return dict(kernel=kern, grid=(n_seq,), in_specs=in_specs, args=args, out_specs=out_specs,
                out_shape=out_shape, scratch=scratch, aliases=aliases, name="mlstm_scan")


def _head_rms(o, g, n_heads, width):
    out = []
    for h in range(n_heads):
        oh = o[:, width * h:width * (h + 1)]
        out.append(oh * lax.rsqrt(jnp.mean(oh * oh, axis=-1, keepdims=True) + NORM_EPS) * g)
    return out


def _mix_kernel(x_ref, og_ref, hm_ref, rg_ref, om_ref, gg_ref, gm_ref, mod_ref, gng_ref, gnm_ref,
                wbg_ref, wbm_ref, wo_ref, lg_ref, lb_ref, o_ref, *, cond_row0, tiles_per_cond):
    r = _cond_row(pl.program_id(0), cond_row0, tiles_per_cond)
    gate = mod_ref[2, pl.ds(r, 1), :]
    halves = [pl.ds(s * MIX_SUB, MIX_SUB) for s in range(MIX_TM // MIX_SUB)]
    branch = []
    for rows in halves:
        o_g = jnp.concatenate(_head_rms(og_ref[rows, :] * (DK_GLA ** -0.5), gng_ref[...], H_GLA, DV_GLA), axis=1)
        a_g = (o_g * rg_ref[rows, :].astype(F32)).astype(BF16)
        h_m = jnp.concatenate(_head_rms(hm_ref[rows, :], gnm_ref[...], H_MLSTM, DH_MLSTM), axis=1)
        a_m = (om_ref[rows, :].astype(F32) * h_m).astype(BF16)
        branch.append((_dot(a_g, wbg_ref[...]), _dot(a_m, wbm_ref[...])))
    for rows, (y_g, y_m) in zip(halves, branch):
        mix = gg_ref[rows, :].astype(F32) * y_g + gm_ref[rows, :].astype(F32) * y_m
        y = _dot(mix.astype(BF16), wo_ref[...])
        o_ref[rows, :] = _layer_norm(ALPHA * x_ref[rows, :] + gate * y, lg_ref[...], lb_ref[...])


def _mix_call(x, og, hm, zb, ada4, gn_gla, gn_ml, wbg, wbm, wo, lng, lnb, *, layer, cond_row0, rows_per_cond):
    m = x.shape[0]
    tm = MIX_TM
    kern = functools.partial(_mix_kernel, cond_row0=cond_row0, tiles_per_cond=rows_per_cond // tm)
    wspec = pl.BlockSpec((None, D_MODEL, D_MODEL), lambda i: (layer, 0, 0), pipeline_mode=pl.Buffered(1))
    return dict(
        kernel=kern, grid=(m // tm,),
        in_specs=[
            pl.BlockSpec((tm, D_MODEL), lambda i: (i, 0)),
            pl.BlockSpec((tm, V_GLA), lambda i: (i, 0)),
            pl.BlockSpec((tm, W_MLSTM), lambda i: (i, 0)),
            pl.BlockSpec((tm, V_GLA), lambda i: (i, SLAB_R_GLA)),
            pl.BlockSpec((tm, W_MLSTM), lambda i: (i, SLAB_O_MLSTM)),
            pl.BlockSpec((tm, D_MODEL), lambda i: (i, SLAB_G_GLA)),
            pl.BlockSpec((tm, D_MODEL), lambda i: (i, SLAB_G_MLSTM)),
            pl.BlockSpec((None, 3, 16, D_MODEL), lambda i: (layer, 1, 0, 0)),
            pl.BlockSpec((None, 1, DV_GLA), lambda i: (layer, 0, 0)),
            pl.BlockSpec((None, 1, DH_MLSTM), lambda i: (layer, 0, 0)),
            wspec, wspec, wspec,
            pl.BlockSpec((None, 1, D_MODEL), lambda i: (3 * layer + 1, 0, 0)),
            pl.BlockSpec((None, 1, D_MODEL), lambda i: (3 * layer + 1, 0, 0)),
        ],
        args=[x, og, hm, zb, zb, zb, zb, ada4, gn_gla, gn_ml, wbg, wbm, wo, lng, lnb],
        out_specs=[pl.BlockSpec((tm, D_MODEL), lambda i: (i, 0))],
        out_shape=[jax.ShapeDtypeStruct((m, D_MODEL), F32)],
        scratch=[], aliases={}, name="mixer_out_ln")


def _grid_pos_embed(t_len):
    rows = t_len // GRID_W
    r = jnp.repeat(jnp.arange(rows), GRID_W).astype(F32)
    col = jnp.tile(jnp.arange(GRID_W), rows).astype(F32)
    nf = D_MODEL // 4
    omega = 1.0 / (10000.0 ** (jnp.arange(nf, dtype=F32) / nf))
    er = r[:, None] * omega
    ec = col[:, None] * omega
    return jnp.concatenate([jnp.sin(er), jnp.cos(er), jnp.sin(ec), jnp.cos(ec)], axis=-1)


def _pack_ffn(w_gate, w_up, w_down):
    return w_gate.astype(BF16), w_up.astype(BF16), w_down.astype(BF16)


def _pack_w_in(w_in):
    o = np.cumsum((0, QK_GLA, QK_GLA, V_GLA, V_GLA, GLA_RANK, GLA_RANK, 2 * W_MLSTM, W_MLSTM, W_MLSTM,
                   H_MLSTM, H_MLSTM, H_MLSTM, H_MLSTM, D_MODEL, D_MODEL))
    o = [int(v) for v in o]
    assert o[4] == W_SPLIT[0] * PROJ_TN
    w16 = w_in.astype(BF16)
    big = (w16, w16[:, :, o[6]:o[9]], w16[:, :, o[13]:o[15]])
    small = jnp.concatenate([w16[:, :, o[4]:o[6]], w16[:, :, o[9]:o[13]]], axis=-1)
    small_p = jnp.pad(small, ((0, 0), (0, 0), (0, Z_SMALL - small.shape[-1])))
    return big, small_p


def kernel(x_prompt, x_sample, c, state_gla_s, state_mlstm_c, state_mlstm_n, state_mlstm_m, c_ctx,
           w_ada, b_ada, ffn1_w_gate, ffn1_w_up, ffn1_w_down, w_in, w_decay, b_decay, w_conv, b_conv,
           f_bias, gla_norm_g, mlstm_norm_g, w_br_gla, w_br_mlstm, w_out,
           ffn2_w_gate, ffn2_w_up, ffn2_w_down, ln_g, ln_b):
    bp, tp, _ = x_prompt.shape
    bs, ts, _ = x_sample.shape
    assert bs + 1 <= 16 and tp % FIN_ROWS == 0 and ts % FIN_ROWS == 0

    cond16 = jnp.zeros((16, D_MODEL), F32).at[0].set(c_ctx).at[1:1 + bs].set(c)
    ada4 = _ada_call(cond16, w_ada, b_ada)

    ffn1 = _pack_ffn(ffn1_w_gate, ffn1_w_up, ffn1_w_down)
    ffn2 = _pack_ffn(ffn2_w_gate, ffn2_w_up, ffn2_w_down)
    w_big, w_small = _pack_w_in(w_in)
    wdec = jnp.zeros((DEPTH, N_DIR, Z_SMALL, QK_GLA), F32)
    wdec = wdec.at[:, 0, 0:GLA_RANK].set(w_decay[:, 0]).at[:, 1, GLA_RANK:2 * GLA_RANK].set(w_decay[:, 1])
    bdec = b_decay.reshape(DEPTH, N_DIR, 1, QK_GLA)
    bcol = jnp.zeros((DEPTH, 1, Z_SMALL), F32)
    brow = jnp.zeros((DEPTH, 16, 1), F32)
    for d in range(N_DIR):
        bcol = bcol.at[:, 0, COL_F[d]:COL_F[d] + H_MLSTM].set(f_bias[:, d])
        brow = brow.at[:, ROW_F[d]:ROW_F[d] + H_MLSTM, 0].set(f_bias[:, d])
    bconv = b_conv.reshape(DEPTH, 1, 2 * W_MLSTM)
    gn_gla = gla_norm_g.reshape(DEPTH, 1, DV_GLA)
    gn_ml = mlstm_norm_g.reshape(DEPTH, 1, DH_MLSTM)
    wbg, wbm, wo = w_br_gla.astype(BF16), w_br_mlstm.astype(BF16), w_out.astype(BF16)
    lng = ln_g.reshape(DEPTH * 3, 1, D_MODEL)
    lnb = ln_b.reshape(DEPTH * 3, 1, D_MODEL)
    pos = _grid_pos_embed(ts)

    m_ctx, m_dec = bp * tp, bs * ts
    ctx = dict(cond_row0=0, rows_per_cond=m_ctx)
    dec = dict(cond_row0=1, rows_per_cond=ts)
    dec_states = (state_gla_s, state_mlstm_c, state_mlstm_n, state_mlstm_m)
    fuse = (m_dec == bp * FUSE_TM and ts % FUSE_TM == 0
            and tp // GLA_CHUNK <= MAX_UNROLLED_CHUNKS and tp // MLSTM_CHUNK <= MAX_UNROLLED_CHUNKS)

    def ffn(x, pos_embed, weights, layer, sub, where, **kw):
        return _ffn_call(x, pos_embed, ada4, *weights, lng, lnb, layer=layer, sub=sub, **where, **kw)

    def proj(x, layer, where, **kw):
        return _proj_call(x, ada4, w_big, w_small, layer=layer, **where, **kw)

    def gla(z, layer, state, carried, n_seq, seq_len):
        return _gla_call(z[0], z[1], wdec, bdec, state, carried, layer=layer, n_seq=n_seq, seq_len=seq_len,
                         emit_state=state is None)

    def mlstm(z, layer, states, carried, n_seq, seq_len, reduce_on_mxu):
        return _mlstm_call(z[0], z[1], z[2], w_conv, bconv, bcol, brow, states, carried, layer=layer,
                           n_seq=n_seq, seq_len=seq_len, emit_state=states is None,
                           reduce_on_mxu=reduce_on_mxu)

    def mix(x, og, hm, z, layer, where):
        return _run(_mix_call(x, og, hm, z[0], ada4, gn_gla, gn_ml, wbg, wbm, wo, lng, lnb,
                              layer=layer, **where))[0]

    xc = x_prompt.reshape(m_ctx, D_MODEL)
    xd = x_sample.reshape(m_dec, D_MODEL)
    carried = None
    for l in range(DEPTH):
        xc = _run(ffn(xc, None, ffn1, l, 0, ctx))[0]
        zc = _run(proj(xc, l, ctx))
        ml_c = mlstm(zc, l, None, None if carried is None else carried[1:], bp, tp, reduce_on_mxu=not fuse)
        gl_c = gla(zc, l, None, None if carried is None else carried[0], bp, tp)
        pos_l = pos if l == 0 else None
        if fuse:
            ml_out, (xd,) = _run_fused(ml_c, ffn(xd, pos_l, ffn1, l, 0, dec, tm=FUSE_TM), "mlstm_scan_ffn")
            gl_out, zd = _run_fused(gl_c, proj(xd, l, dec, tm=FUSE_TM), "gla_scan_in_proj")
        else:
            ml_out, gl_out = _run(ml_c), _run(gl_c)
            xd = _run(ffn(xd, pos_l, ffn1, l, 0, dec))[0]
            zd = _run(proj(xd, l, dec))
        carried = (gl_out[1], ml_out[1], ml_out[2], ml_out[3])
        xc = mix(xc, gl_out[0], ml_out[0], zc, l, ctx)
        xc = _run(ffn(xc, None, ffn2, l, 2, ctx))[0]

        og_d = _run(gla(zd, l, dec_states[0], None, bs, ts))[0]
        hm_d = _run(mlstm(zd, l, dec_states[1:], None, bs, ts, reduce_on_mxu=True))[0]
        xd = mix(xd, og_d, hm_d, zd, l, dec)
        xd = _run(ffn(xd, None, ffn2, l, 2, dec))[0]

    return (xc.reshape(bp, tp, D_MODEL), xd.reshape(bs, ts, D_MODEL)) + tuple(carried)
```

```python
import functools

import jax
import jax.numpy as jnp
import numpy as np
from jax import lax
from jax.experimental import pallas as pl
from jax.experimental.pallas import tpu as pltpu

F32 = jnp.float32
BF16 = jnp.bfloat16

D_MODEL = 1024
DEPTH = 2
GRID_W = 64
N_DIR = 2
H_GLA = 4
DK_GLA = 128
DV_GLA = 256
GLA_RANK = 16
GLA_TAU = 16.0
H_MLSTM = 4
DH_MLSTM = 256
D_FF = 2816
N_MOD = 9
LN_EPS = 1e-5
NORM_EPS = 1e-6
QK_GLA = H_GLA * DK_GLA
V_GLA = H_GLA * DV_GLA
W_MLSTM = H_MLSTM * DH_MLSTM
ALPHA = (2.0 * DEPTH) ** 0.25

Z_BIG = 9 * 1024
Z_SMALL = 128
SLAB_V_GLA, SLAB_R_GLA, SLAB_Q_MLSTM, SLAB_K_MLSTM, SLAB_V_MLSTM = 1, 2, 3, 4, 5
SLAB_O_MLSTM, SLAB_G_GLA, SLAB_G_MLSTM = 6, 7, 8
COL_I = (32, 40)
COL_F = (36, 44)
ROW_I = (0, 8)
ROW_F = (4, 12)

LANE = 128
SUBLANE = 8
VMEM_LIMIT = 56 * 1024 * 1024

FFN_TM = 1024
FFN_SUB = 512
FFN_CHUNKS = ((0, 1536), (1536, 1280))
PROJ_TM = 512
PROJ_TN = 1024
MIX_TM = 512
MIX_SUB = 256
FUSE_TM = 256
GLA_CHUNK = 64
GLA_SUB = 16
MLSTM_CHUNK = 128
FIN_ROWS = 256
MAX_UNROLLED_CHUNKS = 4


def _dot(a, b):
    return jnp.dot(a, b, preferred_element_type=F32)


def _dot_nt(a, b):
    return lax.dot_general(a, b, (((1,), (1,)), ((), ())), preferred_element_type=F32)


def _dot_tn(a, b):
    return lax.dot_general(a, b, (((0,), (0,)), ((), ())), preferred_element_type=F32)


def _split3(x):
    hi = x.astype(BF16)
    r = x - hi.astype(F32)
    mid = r.astype(BF16)
    lo = (r - mid.astype(F32)).astype(BF16)
    return hi, mid, lo


def _split2(x):
    hi = x.astype(BF16)
    return hi, (x - hi.astype(F32)).astype(BF16)


def _dot_acc(x, w):
    xh, xm, _ = _split3(x)
    wh, wl = _split2(w)
    return _dot(xh, wh) + _dot(xh, wl) + _dot(xm, wh)


def _silu(x):
    return x * jax.nn.sigmoid(x)


def _log_sigmoid(x):
    return jnp.minimum(x, 0.0) - jnp.log(1.0 + jnp.exp(-jnp.abs(x)))


def _layer_norm(y, g, b):
    mu = jnp.mean(y, axis=-1, keepdims=True)
    d = y - mu
    var = jnp.mean(d * d, axis=-1, keepdims=True)
    return d * lax.rsqrt(var + LN_EPS) * g + b


def _aligned(x, m):
    return x if isinstance(x, int) else pl.multiple_of(x, m)


def _cparams(n_axes):
    return pltpu.CompilerParams(
        dimension_semantics=("arbitrary",) * n_axes, vmem_limit_bytes=VMEM_LIMIT)


def _cond_row(i, cond_row0, tiles_per_cond):
    return cond_row0 + lax.div(i, jnp.int32(tiles_per_cond))


def _run(call):
    return pl.pallas_call(
        call["kernel"],
        grid=call["grid"],
        in_specs=call["in_specs"],
        out_specs=call["out_specs"],
        out_shape=call["out_shape"],
        scratch_shapes=call["scratch"],
        input_output_aliases=call["aliases"],
        compiler_params=_cparams(len(call["grid"])),
        name=call["name"],
    )(*call["args"])


def _run_fused(scan, mm, name):
    assert scan["grid"] == mm["grid"] and not mm["aliases"]
    counts = [len(scan["args"]), len(mm["args"]), len(scan["out_shape"]), len(mm["out_shape"]),
              len(scan["scratch"]), len(mm["scratch"])]

    def kern(*refs):
        parts, i = [], 0
        for n in counts:
            parts.append(tuple(refs[i:i + n]))
            i += n
        in_a, in_b, out_a, out_b, scr_a, scr_b = parts
        scan["kernel"](*(in_a + out_a + scr_a), hooks=mm["pieces"](in_b + out_b + scr_b))

    outs = pl.pallas_call(
        kern,
        grid=scan["grid"],
        in_specs=scan["in_specs"] + mm["in_specs"],
        out_specs=scan["out_specs"] + mm["out_specs"],
        out_shape=scan["out_shape"] + mm["out_shape"],
        scratch_shapes=scan["scratch"] + mm["scratch"],
        input_output_aliases=scan["aliases"],
        compiler_params=_cparams(len(scan["grid"])),
        name=name,
    )(*scan["args"], *mm["args"])
    n_a = counts[2]
    return outs[:n_a], outs[n_a:]


def _hook_runner(hooks):
    pending = list(hooks)

    def hook(n=1):
        for _ in range(n):
            if pending:
                pending.pop(0)()

    def flush():
        while pending:
            pending.pop(0)()

    return hook, flush


def _ada_kernel(c_ref, w_ref, b_ref, o_ref):
    o_ref[...] = _dot_acc(_silu(c_ref[...]), w_ref[...]) + b_ref[...]


def _ada_call(cond16, w_ada, b_ada):
    return pl.pallas_call(
        _ada_kernel,
        grid=(DEPTH, N_MOD),
        in_specs=[
            pl.BlockSpec((16, D_MODEL), lambda l, j: (0, 0)),
            pl.BlockSpec((None, D_MODEL, D_MODEL), lambda l, j: (l, 0, j)),
            pl.BlockSpec((None, None, 1, D_MODEL), lambda l, j: (l, j, 0, 0)),
        ],
        out_specs=pl.BlockSpec((None, None, 16, D_MODEL), lambda l, j: (l, j, 0, 0)),
        out_shape=jax.ShapeDtypeStruct((DEPTH, N_MOD, 16, D_MODEL), F32),
        compiler_params=_cparams(2),
        name="ada_mod",
    )(cond16, w_ada, b_ada.reshape(DEPTH, N_MOD, 1, D_MODEL))


def _ffn_pieces(refs, *, has_pos, cond_row0, tiles_per_cond, tm, sub):
    if has_pos:
        x_ref, pos_ref, mod_ref, wg_ref, wu_ref, wd_ref, lg_ref, lb_ref, o_ref = refs
    else:
        x_ref, mod_ref, wg_ref, wu_ref, wd_ref, lg_ref, lb_ref, o_ref = refs
        pos_ref = None
    st = {}
    pieces = []

    def start(s):
        rows = pl.ds(s * sub, sub)
        if s == 0:
            r = _cond_row(pl.program_id(0), cond_row0, tiles_per_cond)
            st["sh"] = mod_ref[0, pl.ds(r, 1), :]
            st["sc"] = mod_ref[1, pl.ds(r, 1), :]
            st["gate"] = mod_ref[2, pl.ds(r, 1), :]
        x = x_ref[rows, :]
        if has_pos:
            x = x + pos_ref[rows, :]
        st["x"] = x
        st["hm"] = (x * (1.0 + st["sc"]) + st["sh"]).astype(BF16)
        st["acc"] = None

    def gate_mm(c0, cw):
        st["g"] = _dot(st["hm"], wg_ref[:, c0:c0 + cw])

    def up_mm(c0, cw):
        st["u"] = _dot(st["hm"], wu_ref[:, c0:c0 + cw])

    def down_mm(c0, cw):
        part = _dot((_silu(st["g"]) * st["u"]).astype(BF16), wd_ref[c0:c0 + cw, :])
        st["acc"] = part if st["acc"] is None else st["acc"] + part

    def finish(s):
        y = ALPHA * st["x"] + (0.5 * st["gate"]) * st["acc"]
        o_ref[pl.ds(s * sub, sub), :] = _layer_norm(y, lg_ref[...], lb_ref[...])

    for s in range(tm // sub):
        pieces.append(functools.partial(start, s))
        for c0, cw in FFN_CHUNKS:
            pieces += [functools.partial(gate_mm, c0, cw), functools.partial(up_mm, c0, cw),
                       functools.partial(down_mm, c0, cw)]
        pieces.append(functools.partial(finish, s))
    return pieces


def _ffn_call(x, pos, ada4, wg, wu, wd, lng, lnb, *, layer, sub, cond_row0, rows_per_cond, tm=FFN_TM):
    m = x.shape[0]
    has_pos = pos is not None
    static = dict(has_pos=has_pos, cond_row0=cond_row0, tiles_per_cond=rows_per_cond // tm,
                  tm=tm, sub=min(tm, FFN_SUB))
    pieces = functools.partial(_ffn_pieces, **static)

    def kern(*refs):
        for p in pieces(refs):
            p()

    once = pl.Buffered(1)
    in_specs = [pl.BlockSpec((tm, D_MODEL), lambda i: (i, 0))]
    args = [x]
    if has_pos:
        n_pos = pos.shape[0] // tm
        if n_pos == 1:
            in_specs.append(pl.BlockSpec((tm, D_MODEL), lambda i: (0, 0), pipeline_mode=once))
        else:
            in_specs.append(pl.BlockSpec((tm, D_MODEL), lambda i: (lax.rem(i, jnp.int32(n_pos)), 0)))
        args.append(pos)
    in_specs += [
        pl.BlockSpec((None, 3, 16, D_MODEL), lambda i: (layer, sub, 0, 0)),
        pl.BlockSpec((None, D_MODEL, D_FF), lambda i: (layer, 0, 0), pipeline_mode=once),
        pl.BlockSpec((None, D_MODEL, D_FF), lambda i: (layer, 0, 0), pipeline_mode=once),
        pl.BlockSpec((None, D_FF, D_MODEL), lambda i: (layer, 0, 0), pipeline_mode=once),
        pl.BlockSpec((None, 1, D_MODEL), lambda i: (3 * layer + sub, 0, 0)),
        pl.BlockSpec((None, 1, D_MODEL), lambda i: (3 * layer + sub, 0, 0)),
    ]
    args += [ada4, wg, wu, wd, lng, lnb]
    return dict(kernel=kern, pieces=pieces, grid=(m // tm,), in_specs=in_specs, args=args,
                out_specs=[pl.BlockSpec((tm, D_MODEL), lambda i: (i, 0))],
                out_shape=[jax.ShapeDtypeStruct((m, D_MODEL), F32)],
                scratch=[], aliases={}, name="ffn_ln")


def _proj_pieces(refs, *, cond_row0, tiles_per_cond):
    x_ref, mod_ref, wb_ref, ws_ref, zb_ref, zs_ref, zst_ref = refs
    st = {}

    def start():
        r = _cond_row(pl.program_id(0), cond_row0, tiles_per_cond)
        sh = mod_ref[0, pl.ds(r, 1), :]
        sc = mod_ref[1, pl.ds(r, 1), :]
        st["hm"] = (x_ref[...] * (1.0 + sc) + sh).astype(BF16)
        zs = _dot(st["hm"], ws_ref[...])
        zs_ref[...] = zs
        zst_ref[...] = zs.T[COL_I[0]:COL_I[0] + 16, :]

    def slab(n):
        cols = slice(n * PROJ_TN, (n + 1) * PROJ_TN)
        z = _dot(st["hm"], wb_ref[:, cols])
        if n == SLAB_R_GLA:
            z = _silu(z)
        elif n in (SLAB_O_MLSTM, SLAB_G_GLA, SLAB_G_MLSTM):
            z = jax.nn.sigmoid(z)
        zb_ref[:, cols] = z.astype(BF16)

    return [start] + [functools.partial(slab, n) for n in range(Z_BIG // PROJ_TN)]


def _proj_call(x, ada4, w_big, w_small, *, layer, cond_row0, rows_per_cond, tm=PROJ_TM):
    m = x.shape[0]
    pieces = functools.partial(_proj_pieces, cond_row0=cond_row0, tiles_per_cond=rows_per_cond // tm)

    def kern(*refs):
        for p in pieces(refs):
            p()

    once = pl.Buffered(1)
    return dict(
        kernel=kern, pieces=pieces, grid=(m // tm,),
        in_specs=[
            pl.BlockSpec((tm, D_MODEL), lambda i: (i, 0)),
            pl.BlockSpec((None, 3, 16, D_MODEL), lambda i: (layer, 1, 0, 0)),
            pl.BlockSpec((None, D_MODEL, Z_BIG), lambda i: (layer, 0, 0), pipeline_mode=once),
            pl.BlockSpec((None, D_MODEL, Z_SMALL), lambda i: (layer, 0, 0), pipeline_mode=once),
        ],
        args=[x, ada4, w_big, w_small],
        out_specs=[
            pl.BlockSpec((tm, Z_BIG), lambda i: (i, 0)),
            pl.BlockSpec((tm, Z_SMALL), lambda i: (i, 0)),
            pl.BlockSpec((16, tm), lambda i: (0, i)),
        ],
        out_shape=[
            jax.ShapeDtypeStruct((m, Z_BIG), BF16),
            jax.ShapeDtypeStruct((m, Z_SMALL), F32),
            jax.ShapeDtypeStruct((16, m), F32),
        ],
        scratch=[], aliases={}, name="mixer_in_proj")


def _gla_kernel(*refs, seq_len, has_state, emit_state, n_carried, hooks=()):
    it = iter(refs)
    q_ref, k_ref, v_ref, zs_ref, wdec_ref, bdec_ref = (next(it) for _ in range(6))
    s0_ref = next(it) if has_state else None
    if n_carried:
        next(it)
    oacc_s = next(it)
    sout_ref = next(it) if emit_state else None
    la_s, st_s, cum_s = (next(it) for _ in range(3))
    t_len = seq_len
    c_len = GLA_CHUNK
    n_chunks = t_len // c_len
    unrolled = n_chunks <= MAX_UNROLLED_CHUNKS
    assert unrolled or not hooks
    hook, flush = _hook_runner(hooks)
    hook(4)

    for rb in range(t_len // FIN_ROWS):
        rows = pl.ds(rb * FIN_ROWS, FIN_ROWS)
        zh, zm = _split2(zs_ref[rows, :])
        for d in range(N_DIR):
            wh = wdec_ref[d].astype(BF16)
            x = _dot(zh, wh) + _dot(zm, wh) + bdec_ref[d]
            la_s[d, rows, :] = _log_sigmoid(x) * (1.0 / GLA_TAU)

    for d in range(N_DIR):
        for h in range(H_GLA):
            if has_state:
                st_s[d, h] = s0_ref[d, h].T
            else:
                st_s[d, h] = jnp.zeros((DV_GLA, DK_GLA), F32)
    oacc_s[...] = jnp.zeros_like(oacc_s)

    ri = lax.broadcasted_iota(jnp.int32, (c_len, c_len), 0)
    ci = lax.broadcasted_iota(jnp.int32, (c_len, c_len), 1)
    rr = lax.broadcasted_iota(jnp.int32, (c_len, DK_GLA), 0)
    odd_rows = (lax.shift_right_logical(rr, 4) & 1) == 1
    upper_rows = rr >= 2 * GLA_SUB
    same_sub = lax.shift_right_logical(ri, 4) == lax.shift_right_logical(ci, 4)
    same_half = lax.shift_right_logical(ri, 5) == lax.shift_right_logical(ci, 5)
    causal = (ri >= ci, ri <= ci)
    tri = tuple(jnp.where(c, 1.0, 0.0).astype(BF16) for c in causal)
    pm0 = tuple(same_sub & c for c in causal)
    q1rows = (odd_rows, jnp.logical_not(odd_rows))
    q2rows = (upper_rows, jnp.logical_not(upper_rows))

    def bcast(x, n):
        return jnp.broadcast_to(x, (n, DK_GLA))

    for d in range(N_DIR):
        for cb in range(n_chunks):
            rows = pl.ds(cb * c_len, c_len)
            l_hi, l_lo = _split2(la_s[d, rows, :])
            cum_s[d, rows, :] = _dot(tri[d], l_hi) + _dot(tri[d], l_lo)
    hook()

    def scores(d, h, row0):
        rows = pl.ds(row0, c_len)
        ls = slice(DK_GLA * h, DK_GLA * (h + 1))
        vs = slice(DV_GLA * h, DV_GLA * (h + 1))

        def row(i):
            g = i - i % SUBLANE
            return cum_s[d, pl.ds(_aligned(row0 + g, SUBLANE), SUBLANE), ls][i - g:i - g + 1, :]

        c = cum_s[d, rows, ls]
        zero = jnp.zeros((GLA_SUB, DK_GLA), F32)
        if d == 0:
            ref0 = jnp.concatenate([zero, bcast(row(15), 16), bcast(row(31), 16), bcast(row(47), 16)], axis=0)
            ref1 = jnp.concatenate([bcast(row(15), 32), bcast(row(47), 32)], axis=0)
            ref2, cend = row(31), row(63)
        else:
            ref0 = jnp.concatenate([bcast(row(16), 16), bcast(row(32), 16), bcast(row(48), 16), zero], axis=0)
            ref1 = jnp.concatenate([bcast(row(16), 32), bcast(row(48), 32)], axis=0)
            ref2, cend = row(32), row(0)
        q = q_ref[rows, ls].astype(F32)
        k = k_ref[rows, ls].astype(F32)
        e0 = c - ref0
        p0 = _dot_nt((q * jnp.exp(e0)).astype(BF16), (k * jnp.exp(-e0)).astype(BF16))
        x1 = jnp.exp(-jnp.abs(c - ref1))
        p1 = _dot_nt(jnp.where(q1rows[d], q * x1, 0.0).astype(BF16),
                     jnp.where(q1rows[d], 0.0, k * x1).astype(BF16))
        x2 = jnp.exp(-jnp.abs(c - ref2))
        p2 = _dot_nt(jnp.where(q2rows[d], q * x2, 0.0).astype(BF16),
                     jnp.where(q2rows[d], 0.0, k * x2).astype(BF16))
        qi = (q * jnp.exp(c)).astype(BF16)
        kst = (k * jnp.exp(cend - c)).astype(BF16)
        return dict(d=d, h=h, rows=rows, vs=vs, p0=p0, p1=p1, p2=p2, qi=qi, kst=kst, dec=jnp.exp(cend))

    def chunk_scores(j):
        row0 = (_aligned(j * c_len, c_len), _aligned((n_chunks - 1 - j) * c_len, c_len))
        return [scores(d, h, row0[d]) for d in range(N_DIR) for h in range(H_GLA)]

    def chunk_update(work):
        for w in work:
            d, h = w["d"], w["h"]
            p = jnp.where(pm0[d], w["p0"], jnp.where(same_half, w["p1"], w["p2"]))
            o = _dot(p.astype(BF16), v_ref[w["rows"], w["vs"]]) + _dot_nt(w["qi"], st_s[d, h].astype(BF16))
            oacc_s[w["rows"], w["vs"]] += o
        hook()
        for w in work:
            d, h = w["d"], w["h"]
            st_s[d, h] = st_s[d, h] * w["dec"] + _dot_tn(v_ref[w["rows"], w["vs"]], w["kst"])

    def body(j, carry):
        work = chunk_scores(j)
        hook()
        chunk_update(work)
        return carry

    def body2(jj, carry):
        work = [chunk_scores(2 * jj), chunk_scores(2 * jj + 1)]
        chunk_update(work[0])
        chunk_update(work[1])
        return carry

    if unrolled:
        for j in range(n_chunks):
            body(j, 0)
    else:
        lax.fori_loop(0, n_chunks // 2, body2, 0)
    flush()

    if emit_state:
        for d in range(N_DIR):
            for h in range(H_GLA):
                sout_ref[d, h] = st_s[d, h].T


def _gla_call(zb, zs, wdec, bdec, state, carried, *, layer, n_seq, seq_len, emit_state):
    has_state = state is not None
    t = seq_len
    n_carried = 0 if carried is None else 1
    kern = functools.partial(_gla_kernel, seq_len=t, has_state=has_state, emit_state=emit_state,
                             n_carried=n_carried)
    in_specs = [
        pl.BlockSpec((t, QK_GLA), lambda b: (b, 0)),
        pl.BlockSpec((t, QK_GLA), lambda b: (b, 1)),
        pl.BlockSpec((t, V_GLA), lambda b: (b, SLAB_V_GLA)),
        pl.BlockSpec((t, Z_SMALL), lambda b: (b, 0)),
        pl.BlockSpec((None, N_DIR, Z_SMALL, QK_GLA), lambda b: (layer, 0, 0, 0)),
        pl.BlockSpec((None, N_DIR, 1, QK_GLA), lambda b: (layer, 0, 0, 0)),
    ]
    args = [zb, zb, zb, zs, wdec, bdec]
    if has_state:
        in_specs.append(pl.BlockSpec((None, None, N_DIR, H_GLA, DK_GLA, DV_GLA),
                                     lambda b: (b, layer, 0, 0, 0, 0)))
        args.append(state)
    aliases = {}
    if n_carried:
        aliases[len(args)] = 1
        in_specs.append(pl.BlockSpec(memory_space=pl.ANY))
        args.append(carried)
    out_specs = [pl.BlockSpec((t, V_GLA), lambda b: (b, 0))]
    out_shape = [jax.ShapeDtypeStruct((n_seq * t, V_GLA), F32)]
    if emit_state:
        out_specs.append(pl.BlockSpec((None, None, N_DIR, H_GLA, DK_GLA, DV_GLA),
                                      lambda b: (b, layer, 0, 0, 0, 0)))
        out_shape.append(jax.ShapeDtypeStruct((n_seq, DEPTH, N_DIR, H_GLA, DK_GLA, DV_GLA), F32))
    scratch = [
        pltpu.VMEM((N_DIR, t, QK_GLA), F32),
        pltpu.VMEM((N_DIR, H_GLA, DV_GLA, DK_GLA), F32),
        pltpu.VMEM((N_DIR, t, QK_GLA), F32),
    ]
    return dict(kernel=kern, grid=(n_seq,), in_specs=in_specs, args=args, out_specs=out_specs,
                out_shape=out_shape, scratch=scratch, aliases=aliases, name="gla_scan")


def _mlstm_kernel(*refs, seq_len, has_state, emit_state, n_carried, reduce_on_mxu, hooks=()):
    it = iter(refs)
    (q_ref, k_ref, v_ref, zs_ref, zst_ref, wconv_ref, bconv_ref,
     bcol_ref, brow_ref) = (next(it) for _ in range(9))
    if has_state:
        c0_ref, n0_ref, m0_ref = (next(it) for _ in range(3))
    for _ in range(n_carried):
        next(it)
    hacc_s = next(it)
    if emit_state:
        cout_ref, nout_ref, mout_ref = (next(it) for _ in range(3))
    (qc_s, kc_s, xp_s, lf_s, lft_s, c_s, n_s, m_s, cumc_s, cumr_s) = (next(it) for _ in range(10))
    t_len = seq_len
    c_len = MLSTM_CHUNK
    n_chunks = t_len // c_len
    dh = DH_MLSTM
    unrolled = n_chunks <= MAX_UNROLLED_CHUNKS
    assert unrolled or not hooks
    hook, flush = _hook_runner(hooks)
    hook(6)

    xp_s[pl.ds(0, 8), :] = jnp.zeros((8, dh), F32)
    xp_s[pl.ds(8 + t_len, 8), :] = jnp.zeros((8, dh), F32)
    for which, src in enumerate((q_ref, k_ref)):
        for h in range(H_MLSTM):
            hs = slice(dh * h, dh * (h + 1))
            ws = slice(which * W_MLSTM + dh * h, which * W_MLSTM + dh * (h + 1))
            xp_s[pl.ds(8, t_len), :] = src[:, hs].astype(F32)
            w0 = wconv_ref[pl.ds(0, 1), ws]
            w1 = wconv_ref[pl.ds(1, 1), ws]
            w2 = wconv_ref[pl.ds(2, 1), ws]
            bias = bconv_ref[:, ws]
            for rb in range(t_len // FIN_ROWS):
                r0 = rb * FIN_ROWS
                y = (w0 * xp_s[pl.ds(r0 + 7, FIN_ROWS), :] + w1 * xp_s[pl.ds(r0 + 8, FIN_ROWS), :]
                     + w2 * xp_s[pl.ds(r0 + 9, FIN_ROWS), :] + bias)
                a = _silu(y)
                if which == 0:
                    qc_s[pl.ds(r0, FIN_ROWS), hs] = (a * (dh ** -0.5)).astype(BF16)
                else:
                    kc_s[pl.ds(r0, FIN_ROWS), hs] = a

    lf_s[...] = _log_sigmoid(zs_ref[...] + bcol_ref[...])
    zt = zst_ref[...]
    trow = lax.broadcasted_iota(jnp.int32, zt.shape, 0)
    is_f = (lax.shift_right_logical(trow, 2) & 1) == 1
    lft_s[...] = jnp.where(is_f, _log_sigmoid(zt + brow_ref[...]), zt)

    for d in range(N_DIR):
        for h in range(H_MLSTM):
            if has_state:
                c_s[d, h] = c0_ref[d, h]
                n_s[d, h] = n0_ref[d, pl.ds(h, 1), :]
                m_s[d, h] = jnp.broadcast_to(m0_ref[pl.ds(d, 1), pl.ds(h, 1)], (1, LANE))
            else:
                c_s[d, h] = jnp.zeros((dh, dh), F32)
                n_s[d, h] = jnp.zeros((1, dh), F32)
                m_s[d, h] = jnp.zeros((1, LANE), F32)
    hacc_s[...] = jnp.zeros_like(hacc_s)

    ri = lax.broadcasted_iota(jnp.int32, (c_len, c_len), 0)
    ci = lax.broadcasted_iota(jnp.int32, (c_len, c_len), 1)
    causal = (ri >= ci, ri <= ci)
    tri_c = tuple(jnp.where(c, 1.0, 0.0).astype(BF16) for c in causal)
    tri_r = (tri_c[1], tri_c[0])

    for d in range(N_DIR):
        for cb in range(n_chunks):
            rows = pl.ds(cb * c_len, c_len)
            l3 = _split3(lf_s[rows, :])
            cumc_s[d, rows, :] = _dot(tri_c[d], l3[0]) + _dot(tri_c[d], l3[1]) + _dot(tri_c[d], l3[2])
            r3 = _split3(lft_s[:, rows])
            cumr_s[d, :, rows] = _dot(r3[0], tri_r[d]) + _dot(r3[1], tri_r[d]) + _dot(r3[2], tri_r[d])

    ones_c = jnp.ones((c_len, LANE), BF16)

    def lanes2(x):
        return jnp.concatenate([x, x], axis=1)

    def matmuls(d, h, row0):
        rows = pl.ds(row0, c_len)
        hs = slice(dh * h, dh * (h + 1))
        q = qc_s[rows, hs]
        s_raw = _dot_nt(q, kc_s[rows, hs].astype(BF16))
        qc = _dot(q, c_s[d, h].astype(BF16))
        if reduce_on_mxu:
            n_rep = jnp.broadcast_to(n_s[d, h], (LANE, dh)).astype(BF16)
            qn = _dot_nt(q, n_rep)
        else:
            qn = jnp.sum(q.astype(F32) * n_s[d, h], axis=1, keepdims=True)
        return dict(d=d, h=h, rows=rows, row0=row0, hs=hs, s_raw=s_raw, qc=qc, qn=qn)

    def gates(w):
        d, h, rows, row0 = w["d"], w["h"], w["rows"], w["row0"]
        cf = COL_F[d] + h
        end = c_len - 1 if d == 0 else 0
        cum_col = cumc_s[d, rows, pl.ds(cf, 1)]
        cum_c = jnp.broadcast_to(cum_col, (c_len, LANE))
        b_c = jnp.broadcast_to(zs_ref[rows, pl.ds(COL_I[d] + h, 1)] - cum_col, (c_len, LANE))
        g_end = end - end % SUBLANE
        cum_end = cumc_s[d, pl.ds(_aligned(row0 + g_end, SUBLANE), SUBLANE), pl.ds(cf, 1)][
            end - g_end:end - g_end + 1, :]
        b_r = lft_s[pl.ds(ROW_I[d] + h, 1), rows] - cumr_s[d, pl.ds(ROW_F[d] + h, 1), rows]
        m_prev = m_s[d, h]
        b_m = jnp.where(causal[d], b_r, -jnp.inf)
        m_rel = jnp.maximum(m_prev, jnp.broadcast_to(jnp.max(b_m, axis=1, keepdims=True), (c_len, LANE)))
        w["dmat"] = jnp.exp(b_m - m_rel)
        w["inter"] = jnp.exp(m_prev - m_rel)
        w["floor"] = jnp.exp(-(cum_c + m_rel))
        log_w = cum_end + b_c
        m_new = jnp.maximum(cum_end + m_prev, jnp.max(log_w, axis=0, keepdims=True))
        w["wgt"] = jnp.exp(log_w - m_new)
        w["decay"] = jnp.exp(cum_end + m_prev - m_new)
        w["m_new"] = m_new

    def body(j, carry):
        row0 = (_aligned(j * c_len, c_len), _aligned((n_chunks - 1 - j) * c_len, c_len))
        work = [matmuls(d, h, row0[d]) for d in range(N_DIR) for h in range(H_MLSTM)]
        hook()
        for w in work:
            gates(w)
        for w in work:
            s = w["s_raw"] * w["dmat"]
            s_hi = s.astype(BF16)
            w["sv"] = _dot(s_hi, v_ref[w["rows"], w["hs"]])
            if reduce_on_mxu:
                s_lo = (s - s_hi.astype(F32)).astype(BF16)
                w["rsum"] = _dot(s_hi, ones_c) + _dot(s_lo, ones_c)
            else:
                w["rsum"] = jnp.sum(s, axis=1, keepdims=True)
        hook()
        for w in work:
            den = w["rsum"] + w["inter"] * w["qn"]
            rn = 1.0 / jnp.maximum(jnp.abs(den), w["floor"])
            hacc_s[w["rows"], w["hs"]] += w["sv"] * lanes2(rn) + w["qc"] * lanes2(w["inter"] * rn)
        for w in work:
            d, h = w["d"], w["h"]
            kw = kc_s[w["rows"], w["hs"]] * lanes2(w["wgt"])
            decay = w["decay"]
            c_s[d, h] = decay[:, :1] * c_s[d, h] + _dot_tn(kw.astype(BF16), v_ref[w["rows"], w["hs"]])
            n_s[d, h] = lanes2(decay) * n_s[d, h] + jnp.sum(kw, axis=0, keepdims=True)
            m_s[d, h] = w["m_new"]
        return carry

    if unrolled:
        for j in range(n_chunks):
            body(j, 0)
    else:
        lax.fori_loop(0, n_chunks, body, 0)
    flush()

    if emit_state:
        for d in range(N_DIR):
            for h in range(H_MLSTM):
                cout_ref[d, h] = c_s[d, h]
                nout_ref[d, pl.ds(h, 1), :] = n_s[d, h]
                mout_ref[pl.ds(d, 1), pl.ds(h, 1)] = m_s[d, h][:, :1]


def _mlstm_call(zb, zs, zst, wconv, bconv, bcol, brow, states, carried, *, layer, n_seq, seq_len,
                emit_state, reduce_on_mxu):
    has_state = states is not None
    t = seq_len
    dh = DH_MLSTM
    n_carried = 0 if carried is None else 3
    kern = functools.partial(_mlstm_kernel, seq_len=t, has_state=has_state, emit_state=emit_state,
                             n_carried=n_carried, reduce_on_mxu=reduce_on_mxu)
    in_specs = [
        pl.BlockSpec((t, W_MLSTM), lambda b: (b, SLAB_Q_MLSTM)),
        pl.BlockSpec((t, W_MLSTM), lambda b: (b, SLAB_K_MLSTM)),
        pl.BlockSpec((t, W_MLSTM), lambda b: (b, SLAB_V_MLSTM)),
        pl.BlockSpec((t, Z_SMALL), lambda b: (b, 0)),
        pl.BlockSpec((16, t), lambda b: (0, b)),
        pl.BlockSpec((None, 3, 2 * W_MLSTM), lambda b: (layer, 0, 0)),
        pl.BlockSpec((None, 1, 2 * W_MLSTM), lambda b: (layer, 0, 0)),
        pl.BlockSpec((None, 1, Z_SMALL), lambda b: (layer, 0, 0)),
        pl.BlockSpec((None, 16, 1), lambda b: (layer, 0, 0)),
    ]
    args = [zb, zb, zb, zs, zst, wconv, bconv, bcol, brow]
    if has_state:
        c0, n0, m0 = states
        in_specs += [
            pl.BlockSpec((None, None, N_DIR, H_MLSTM, dh, dh), lambda b: (b, layer, 0, 0, 0, 0)),
            pl.BlockSpec((None, None, N_DIR, H_MLSTM, dh), lambda b: (b, layer, 0, 0, 0)),
            pl.BlockSpec((None, None, N_DIR, H_MLSTM), lambda b: (b, layer, 0, 0)),
        ]
        args += [c0, n0, m0]
    aliases = {}
    for k in range(n_carried):
        aliases[len(args)] = 1 + k
        in_specs.append(pl.BlockSpec(memory_space=pl.ANY))
        args.append(carried[k])
    out_specs = [pl.BlockSpec((t, W_MLSTM), lambda b: (b, 0))]
    out_shape = [jax.ShapeDtypeStruct((n_seq * t, W_MLSTM), F32)]
    if emit_state:
        out_specs += [
            pl.BlockSpec((None, None, N_DIR, H_MLSTM, dh, dh), lambda b: (b, layer, 0, 0, 0, 0)),
            pl.BlockSpec((None, None, N_DIR, H_MLSTM, dh), lambda b: (b, layer, 0, 0, 0)),
            pl.BlockSpec((None, None, N_DIR, H_MLSTM), lambda b: (b, layer, 0, 0)),
        ]
        out_shape += [
            jax.ShapeDtypeStruct((n_seq, DEPTH, N_DIR, H_MLSTM, dh, dh), F32),
            jax.ShapeDtypeStruct((n_seq, DEPTH, N_DIR, H_MLSTM, dh), F32),
            jax.ShapeDtypeStruct((n_seq, DEPTH, N_DIR, H_MLSTM), F32),
        ]
    scratch = [
        pltpu.VMEM((t, W_MLSTM), BF16),
        pltpu.VMEM((t, W_MLSTM), F32),
        pltpu.VMEM((t + 16, dh), F32),
        pltpu.VMEM((t, Z_SMALL), F32),
        pltpu.VMEM((16, t), F32),
        pltpu.VMEM((N_DIR, H_MLSTM, dh, dh), F32),
        pltpu.VMEM((N_DIR, H_MLSTM, 1, dh), F32),
        pltpu.VMEM((N_DIR, H_MLSTM, 1, LANE), F32),
        pltpu.VMEM((N_DIR, t, Z_SMALL), F32),
        pltpu.VMEM((N_DIR, 16, t), F32),
    ]
    return dict(kernel=kern, grid=(n_seq,), in_specs=in_specs, args=args, out_specs=out_specs,
                out_shape=out_shape, scratch=scratch, aliases=aliases, name="mlstm_scan")


def _head_rms(o, g, n_heads, width):
    out = []
    for h in range(n_heads):
        oh = o[:, width * h:width * (h + 1)]
        out.append(oh * lax.rsqrt(jnp.mean(oh * oh, axis=-1, keepdims=True) + NORM_EPS) * g)
    return out


def _mix_kernel(x_ref, og_ref, hm_ref, rg_ref, om_ref, gg_ref, gm_ref, mod_ref, gng_ref, gnm_ref,
                wbg_ref, wbm_ref, wo_ref, lg_ref, lb_ref, o_ref, *, cond_row0, tiles_per_cond):
    r = _cond_row(pl.program_id(0), cond_row0, tiles_per_cond)
    gate = mod_ref[2, pl.ds(r, 1), :]
    halves = [pl.ds(s * MIX_SUB, MIX_SUB) for s in range(MIX_TM // MIX_SUB)]
    branch = []
    for rows in halves:
        o_g = jnp.concatenate(_head_rms(og_ref[rows, :] * (DK_GLA ** -0.5), gng_ref[...], H_GLA, DV_GLA), axis=1)
        a_g = (o_g * rg_ref[rows, :].astype(F32)).astype(BF16)
        h_m = jnp.concatenate(_head_rms(hm_ref[rows, :], gnm_ref[...], H_MLSTM, DH_MLSTM), axis=1)
        a_m = (om_ref[rows, :].astype(F32) * h_m).astype(BF16)
        branch.append((_dot(a_g, wbg_ref[...]), _dot(a_m, wbm_ref[...])))
    for rows, (y_g, y_m) in zip(halves, branch):
        mix = gg_ref[rows, :].astype(F32) * y_g + gm_ref[rows, :].astype(F32) * y_m
        y = _dot(mix.astype(BF16), wo_ref[...])
        o_ref[rows, :] = _layer_norm(ALPHA * x_ref[rows, :] + gate * y, lg_ref[...], lb_ref[...])


def _mix_call(x, og, hm, zb, ada4, gn_gla, gn_ml, wbg, wbm, wo, lng, lnb, *, layer, cond_row0, rows_per_cond):
    m = x.shape[0]
    tm = MIX_TM
    kern = functools.partial(_mix_kernel, cond_row0=cond_row0, tiles_per_cond=rows_per_cond // tm)
    wspec = pl.BlockSpec((None, D_MODEL, D_MODEL), lambda i: (layer, 0, 0), pipeline_mode=pl.Buffered(1))
    return dict(
        kernel=kern, grid=(m // tm,),
        in_specs=[
            pl.BlockSpec((tm, D_MODEL), lambda i: (i, 0)),
            pl.BlockSpec((tm, V_GLA), lambda i: (i, 0)),
            pl.BlockSpec((tm, W_MLSTM), lambda i: (i, 0)),
            pl.BlockSpec((tm, V_GLA), lambda i: (i, SLAB_R_GLA)),
            pl.BlockSpec((tm, W_MLSTM), lambda i: (i, SLAB_O_MLSTM)),
            pl.BlockSpec((tm, D_MODEL), lambda i: (i, SLAB_G_GLA)),
            pl.BlockSpec((tm, D_MODEL), lambda i: (i, SLAB_G_MLSTM)),
            pl.BlockSpec((None, 3, 16, D_MODEL), lambda i: (layer, 1, 0, 0)),
            pl.BlockSpec((None, 1, DV_GLA), lambda i: (layer, 0, 0)),
            pl.BlockSpec((None, 1, DH_MLSTM), lambda i: (layer, 0, 0)),
            wspec, wspec, wspec,
            pl.BlockSpec((None, 1, D_MODEL), lambda i: (3 * layer + 1, 0, 0)),
            pl.BlockSpec((None, 1, D_MODEL), lambda i: (3 * layer + 1, 0, 0)),
        ],
        args=[x, og, hm, zb, zb, zb, zb, ada4, gn_gla, gn_ml, wbg, wbm, wo, lng, lnb],
        out_specs=[pl.BlockSpec((tm, D_MODEL), lambda i: (i, 0))],
        out_shape=[jax.ShapeDtypeStruct((m, D_MODEL), F32)],
        scratch=[], aliases={}, name="mixer_out_ln")


def _grid_pos_embed(t_len):
    rows = t_len // GRID_W
    r = jnp.repeat(jnp.arange(rows), GRID_W).astype(F32)
    col = jnp.tile(jnp.arange(GRID_W), rows).astype(F32)
    nf = D_MODEL // 4
    omega = 1.0 / (10000.0 ** (jnp.arange(nf, dtype=F32) / nf))
    er = r[:, None] * omega
    ec = col[:, None] * omega
    return jnp.concatenate([jnp.sin(er), jnp.cos(er), jnp.sin(ec), jnp.cos(ec)], axis=-1)


def _pack_ffn(w_gate, w_up, w_down):
    return w_gate.astype(BF16), w_up.astype(BF16), w_down.astype(BF16)


def _pack_w_in(w_in):
    o = np.cumsum((0, QK_GLA, QK_GLA, V_GLA, V_GLA, GLA_RANK, GLA_RANK, 2 * W_MLSTM, W_MLSTM, W_MLSTM,
                   H_MLSTM, H_MLSTM, H_MLSTM, H_MLSTM, D_MODEL, D_MODEL))
    o = [int(v) for v in o]
    big = jnp.concatenate([w_in[:, :, o[0]:o[4]], w_in[:, :, o[6]:o[9]], w_in[:, :, o[13]:o[15]]],
                          axis=-1).astype(BF16)
    small = jnp.concatenate([w_in[:, :, o[4]:o[6]], w_in[:, :, o[9]:o[13]]], axis=-1)
    small_p = jnp.pad(small, ((0, 0), (0, 0), (0, Z_SMALL - small.shape[-1]))).astype(BF16)
    return big, small_p


def kernel(x_prompt, x_sample, c, state_gla_s, state_mlstm_c, state_mlstm_n, state_mlstm_m, c_ctx,
           w_ada, b_ada, ffn1_w_gate, ffn1_w_up, ffn1_w_down, w_in, w_decay, b_decay, w_conv, b_conv,
           f_bias, gla_norm_g, mlstm_norm_g, w_br_gla, w_br_mlstm, w_out,
           ffn2_w_gate, ffn2_w_up, ffn2_w_down, ln_g, ln_b):
    bp, tp, _ = x_prompt.shape
    bs, ts, _ = x_sample.shape
    assert bs + 1 <= 16 and tp % FIN_ROWS == 0 and ts % FIN_ROWS == 0

    cond16 = jnp.zeros((16, D_MODEL), F32).at[0].set(c_ctx).at[1:1 + bs].set(c)
    ada4 = _ada_call(cond16, w_ada, b_ada)

    ffn1 = _pack_ffn(ffn1_w_gate, ffn1_w_up, ffn1_w_down)
    ffn2 = _pack_ffn(ffn2_w_gate, ffn2_w_up, ffn2_w_down)
    w_big, w_small = _pack_w_in(w_in)
    wdec = jnp.zeros((DEPTH, N_DIR, Z_SMALL, QK_GLA), F32)
    wdec = wdec.at[:, 0, 0:GLA_RANK].set(w_decay[:, 0]).at[:, 1, GLA_RANK:2 * GLA_RANK].set(w_decay[:, 1])
    bdec = b_decay.reshape(DEPTH, N_DIR, 1, QK_GLA)
    bcol = jnp.zeros((DEPTH, 1, Z_SMALL), F32)
    brow = jnp.zeros((DEPTH, 16, 1), F32)
    for d in range(N_DIR):
        bcol = bcol.at[:, 0, COL_F[d]:COL_F[d] + H_MLSTM].set(f_bias[:, d])
        brow = brow.at[:, ROW_F[d]:ROW_F[d] + H_MLSTM, 0].set(f_bias[:, d])
    bconv = b_conv.reshape(DEPTH, 1, 2 * W_MLSTM)
    gn_gla = gla_norm_g.reshape(DEPTH, 1, DV_GLA)
    gn_ml = mlstm_norm_g.reshape(DEPTH, 1, DH_MLSTM)
    wbg, wbm, wo = w_br_gla.astype(BF16), w_br_mlstm.astype(BF16), w_out.astype(BF16)
    lng = ln_g.reshape(DEPTH * 3, 1, D_MODEL)
    lnb = ln_b.reshape(DEPTH * 3, 1, D_MODEL)
    pos = _grid_pos_embed(ts)

    m_ctx, m_dec = bp * tp, bs * ts
    ctx = dict(cond_row0=0, rows_per_cond=m_ctx)
    dec = dict(cond_row0=1, rows_per_cond=ts)
    dec_states = (state_gla_s, state_mlstm_c, state_mlstm_n, state_mlstm_m)
    fuse = (m_dec == bp * FUSE_TM and ts % FUSE_TM == 0
            and tp // GLA_CHUNK <= MAX_UNROLLED_CHUNKS and tp // MLSTM_CHUNK <= MAX_UNROLLED_CHUNKS)

    def ffn(x, pos_embed, weights, layer, sub, where, **kw):
        return _ffn_call(x, pos_embed, ada4, *weights, lng, lnb, layer=layer, sub=sub, **where, **kw)

    def proj(x, layer, where, **kw):
        return _proj_call(x, ada4, w_big, w_small, layer=layer, **where, **kw)

    def gla(z, layer, state, carried, n_seq, seq_len):
        return _gla_call(z[0], z[1], wdec, bdec, state, carried, layer=layer, n_seq=n_seq, seq_len=seq_len,
                         emit_state=state is None)

    def mlstm(z, layer, states, carried, n_seq, seq_len, reduce_on_mxu):
        return _mlstm_call(z[0], z[1], z[2], w_conv, bconv, bcol, brow, states, carried, layer=layer,
                           n_seq=n_seq, seq_len=seq_len, emit_state=states is None,
                           reduce_on_mxu=reduce_on_mxu)

    def mix(x, og, hm, z, layer, where):
        return _run(_mix_call(x, og, hm, z[0], ada4, gn_gla, gn_ml, wbg, wbm, wo, lng, lnb,
                              layer=layer, **where))[0]

    xc = x_prompt.reshape(m_ctx, D_MODEL)
    xd = x_sample.reshape(m_dec, D_MODEL)
    carried = None
    for l in range(DEPTH):
        xc = _run(ffn(xc, None, ffn1, l, 0, ctx))[0]
        zc = _run(proj(xc, l, ctx))
        ml_c = mlstm(zc, l, None, None if carried is None else carried[1:], bp, tp, reduce_on_mxu=not fuse)
        gl_c = gla(zc, l, None, None if carried is None else carried[0], bp, tp)
        pos_l = pos if l == 0 else None
        if fuse:
            ml_out, (xd,) = _run_fused(ml_c, ffn(xd, pos_l, ffn1, l, 0, dec, tm=FUSE_TM), "mlstm_scan_ffn")
            gl_out, zd = _run_fused(gl_c, proj(xd, l, dec, tm=FUSE_TM), "gla_scan_in_proj")
        else:
            ml_out, gl_out = _run(ml_c), _run(gl_c)
            xd = _run(ffn(xd, pos_l, ffn1, l, 0, dec))[0]
            zd = _run(proj(xd, l, dec))
        carried = (gl_out[1], ml_out[1], ml_out[2], ml_out[3])
        xc = mix(xc, gl_out[0], ml_out[0], zc, l, ctx)
        xc = _run(ffn(xc, None, ffn2, l, 2, ctx))[0]

        og_d = _run(gla(zd, l, dec_states[0], None, bs, ts))[0]
        hm_d = _run(mlstm(zd, l, dec_states[1:], None, bs, ts, reduce_on_mxu=True))[0]
        xd = mix(xd, og_d, hm_d, zd, l, dec)
        xd = _run(ffn(xd, None, ffn2, l, 2, dec))[0]

    return (xc.reshape(bp, tp, D_MODEL), xd.reshape(bs, ts, D_MODEL)) + tuple(carried)
```

```python
import functools

import jax
import jax.numpy as jnp
import numpy as np
from jax import lax
from jax.experimental import pallas as pl
from jax.experimental.pallas import tpu as pltpu

F32 = jnp.float32
BF16 = jnp.bfloat16

D_MODEL = 1024
DEPTH = 2
GRID_W = 64
N_DIR = 2
H_GLA = 4
DK_GLA = 128
DV_GLA = 256
GLA_RANK = 16
GLA_TAU = 16.0
H_MLSTM = 4
DH_MLSTM = 256
D_FF = 2816
N_MOD = 9
LN_EPS = 1e-5
NORM_EPS = 1e-6
QK_GLA = H_GLA * DK_GLA
V_GLA = H_GLA * DV_GLA
W_MLSTM = H_MLSTM * DH_MLSTM
ALPHA = (2.0 * DEPTH) ** 0.25

Z_BIG = 9 * 1024
Z_SMALL = 128
SLAB_V_GLA, SLAB_R_GLA, SLAB_Q_MLSTM, SLAB_K_MLSTM, SLAB_V_MLSTM = 1, 2, 3, 4, 5
SLAB_O_MLSTM, SLAB_G_GLA, SLAB_G_MLSTM = 6, 7, 8
W_SPLIT = (3, 7)
COL_I = (32, 40)
COL_F = (36, 44)
ROW_I = (0, 8)
ROW_F = (4, 12)

LANE = 128
SUBLANE = 8
VMEM_LIMIT = 56 * 1024 * 1024

FFN_TM = 1024
FFN_SUB = 512
FFN_CHUNKS = ((0, 1536), (1536, 1280))
PROJ_TM = 512
PROJ_TN = 1024
MIX_TM = 512
MIX_SUB = 256
FUSE_TM = 256
GLA_CHUNK = 64
GLA_SUB = 16
MLSTM_CHUNK = 128
FIN_ROWS = 256
MAX_UNROLLED_CHUNKS = 4
LOOP_CHUNKS = 4


def _dot(a, b):
    return jnp.dot(a, b, preferred_element_type=F32)


def _dot_nt(a, b):
    return lax.dot_general(a, b, (((1,), (1,)), ((), ())), preferred_element_type=F32)


def _dot_tn(a, b):
    return lax.dot_general(a, b, (((0,), (0,)), ((), ())), preferred_element_type=F32)


def _split3(x):
    hi = x.astype(BF16)
    r = x - hi.astype(F32)
    mid = r.astype(BF16)
    lo = (r - mid.astype(F32)).astype(BF16)
    return hi, mid, lo


def _split2(x):
    hi = x.astype(BF16)
    return hi, (x - hi.astype(F32)).astype(BF16)


def _dot_acc(x, w):
    xh, xm, _ = _split3(x)
    wh, wl = _split2(w)
    return _dot(xh, wh) + _dot(xh, wl) + _dot(xm, wh)


def _silu(x):
    return x * jax.nn.sigmoid(x)


def _log_sigmoid(x):
    return jnp.minimum(x, 0.0) - jnp.log(1.0 + jnp.exp(-jnp.abs(x)))


def _layer_norm(y, g, b):
    mu = jnp.mean(y, axis=-1, keepdims=True)
    d = y - mu
    var = jnp.mean(d * d, axis=-1, keepdims=True)
    return d * lax.rsqrt(var + LN_EPS) * g + b


def _aligned(x, m):
    return x if isinstance(x, int) else pl.multiple_of(x, m)


def _cparams(n_axes):
    return pltpu.CompilerParams(
        dimension_semantics=("arbitrary",) * n_axes, vmem_limit_bytes=VMEM_LIMIT)


def _cond_row(i, cond_row0, tiles_per_cond):
    return cond_row0 + lax.div(i, jnp.int32(tiles_per_cond))


def _run(call):
    return pl.pallas_call(
        call["kernel"],
        grid=call["grid"],
        in_specs=call["in_specs"],
        out_specs=call["out_specs"],
        out_shape=call["out_shape"],
        scratch_shapes=call["scratch"],
        input_output_aliases=call["aliases"],
        compiler_params=_cparams(len(call["grid"])),
        name=call["name"],
    )(*call["args"])


def _run_fused(scan, mm, name):
    assert scan["grid"] == mm["grid"] and not mm["aliases"]
    counts = [len(scan["args"]), len(mm["args"]), len(scan["out_shape"]), len(mm["out_shape"]),
              len(scan["scratch"]), len(mm["scratch"])]

    def kern(*refs):
        parts, i = [], 0
        for n in counts:
            parts.append(tuple(refs[i:i + n]))
            i += n
        in_a, in_b, out_a, out_b, scr_a, scr_b = parts
        scan["kernel"](*(in_a + out_a + scr_a), hooks=mm["pieces"](in_b + out_b + scr_b))

    outs = pl.pallas_call(
        kern,
        grid=scan["grid"],
        in_specs=scan["in_specs"] + mm["in_specs"],
        out_specs=scan["out_specs"] + mm["out_specs"],
        out_shape=scan["out_shape"] + mm["out_shape"],
        scratch_shapes=scan["scratch"] + mm["scratch"],
        input_output_aliases=scan["aliases"],
        compiler_params=_cparams(len(scan["grid"])),
        name=name,
    )(*scan["args"], *mm["args"])
    n_a = counts[2]
    return outs[:n_a], outs[n_a:]


def _hook_runner(hooks):
    pending = list(hooks)

    def hook(n=1):
        for _ in range(n):
            if pending:
                pending.pop(0)()

    def flush():
        while pending:
            pending.pop(0)()

    return hook, flush


def _ada_kernel(c_ref, w_ref, b_ref, o_ref):
    o_ref[...] = _dot_acc(_silu(c_ref[...]), w_ref[...]) + b_ref[...]


def _ada_call(cond16, w_ada, b_ada):
    return pl.pallas_call(
        _ada_kernel,
        grid=(DEPTH, N_MOD),
        in_specs=[
            pl.BlockSpec((16, D_MODEL), lambda l, j: (0, 0)),
            pl.BlockSpec((None, D_MODEL, D_MODEL), lambda l, j: (l, 0, j)),
            pl.BlockSpec((None, None, 1, D_MODEL), lambda l, j: (l, j, 0, 0)),
        ],
        out_specs=pl.BlockSpec((None, None, 16, D_MODEL), lambda l, j: (l, j, 0, 0)),
        out_shape=jax.ShapeDtypeStruct((DEPTH, N_MOD, 16, D_MODEL), F32),
        compiler_params=_cparams(2),
        name="ada_mod",
    )(cond16, w_ada, b_ada.reshape(DEPTH, N_MOD, 1, D_MODEL))


def _ffn_pieces(refs, *, has_pos, cond_row0, tiles_per_cond, tm, sub):
    if has_pos:
        x_ref, pos_ref, mod_ref, wg_ref, wu_ref, wd_ref, lg_ref, lb_ref, o_ref = refs
    else:
        x_ref, mod_ref, wg_ref, wu_ref, wd_ref, lg_ref, lb_ref, o_ref = refs
        pos_ref = None
    st = {}
    pieces = []

    def start(s):
        rows = pl.ds(s * sub, sub)
        if "sh" not in st:
            r = _cond_row(pl.program_id(0), cond_row0, tiles_per_cond)
            st["sh"] = mod_ref[0, pl.ds(r, 1), :]
            st["sc"] = mod_ref[1, pl.ds(r, 1), :]
            st["gate"] = mod_ref[2, pl.ds(r, 1), :]
        x = x_ref[rows, :]
        if has_pos:
            x = x + pos_ref[rows, :]
        st["x", s] = x
        st["hm", s] = (x * (1.0 + st["sc"]) + st["sh"]).astype(BF16)
        st["acc", s] = None

    def gate_mm(s, c0, cw):
        st["g", s] = _dot(st["hm", s], wg_ref[:, c0:c0 + cw])

    def up_mm(s, c0, cw):
        st["u", s] = _dot(st["hm", s], wu_ref[:, c0:c0 + cw])

    def down_mm(s, c0, cw):
        part = _dot((_silu(st["g", s]) * st["u", s]).astype(BF16), wd_ref[c0:c0 + cw, :])
        st["acc", s] = part if st["acc", s] is None else st["acc", s] + part

    def finish(s):
        y = ALPHA * st["x", s] + (0.5 * st["gate"]) * st["acc", s]
        o_ref[pl.ds(s * sub, sub), :] = _layer_norm(y, lg_ref[...], lb_ref[...])

    per_tile = []
    for s in range(tm // sub):
        steps = [functools.partial(start, s)]
        for c0, cw in FFN_CHUNKS:
            steps += [functools.partial(gate_mm, s, c0, cw), functools.partial(up_mm, s, c0, cw),
                      functools.partial(down_mm, s, c0, cw)]
        per_tile.append(steps + [functools.partial(finish, s)])
    for k in range(len(per_tile[0]) + len(per_tile) - 1):
        for s, steps in enumerate(per_tile):
            if 0 <= k - s < len(steps):
                pieces.append(steps[k - s])
    return pieces


def _ffn_call(x, pos, ada4, wg, wu, wd, lng, lnb, *, layer, sub, cond_row0, rows_per_cond, tm=FFN_TM):
    m = x.shape[0]
    has_pos = pos is not None
    static = dict(has_pos=has_pos, cond_row0=cond_row0, tiles_per_cond=rows_per_cond // tm,
                  tm=tm, sub=min(tm, FFN_SUB))
    pieces = functools.partial(_ffn_pieces, **static)

    def kern(*refs):
        for p in pieces(refs):
            p()

    once = pl.Buffered(1)
    in_specs = [pl.BlockSpec((tm, D_MODEL), lambda i: (i, 0))]
    args = [x]
    if has_pos:
        n_pos = pos.shape[0] // tm
        if n_pos == 1:
            in_specs.append(pl.BlockSpec((tm, D_MODEL), lambda i: (0, 0), pipeline_mode=once))
        else:
            in_specs.append(pl.BlockSpec((tm, D_MODEL), lambda i: (lax.rem(i, jnp.int32(n_pos)), 0)))
        args.append(pos)
    in_specs += [
        pl.BlockSpec((None, 3, 16, D_MODEL), lambda i: (layer, sub, 0, 0)),
        pl.BlockSpec((None, D_MODEL, D_FF), lambda i: (layer, 0, 0), pipeline_mode=once),
        pl.BlockSpec((None, D_MODEL, D_FF), lambda i: (layer, 0, 0), pipeline_mode=once),
        pl.BlockSpec((None, D_FF, D_MODEL), lambda i: (layer, 0, 0), pipeline_mode=once),
        pl.BlockSpec((None, 1, D_MODEL), lambda i: (3 * layer + sub, 0, 0)),
        pl.BlockSpec((None, 1, D_MODEL), lambda i: (3 * layer + sub, 0, 0)),
    ]
    args += [ada4, wg, wu, wd, lng, lnb]
    return dict(kernel=kern, pieces=pieces, grid=(m // tm,), in_specs=in_specs, args=args,
                out_specs=[pl.BlockSpec((tm, D_MODEL), lambda i: (i, 0))],
                out_shape=[jax.ShapeDtypeStruct((m, D_MODEL), F32)],
                scratch=[], aliases={}, name="ffn_ln")


def _proj_pieces(refs, *, cond_row0, tiles_per_cond):
    x_ref, mod_ref, wa_ref, wb_ref, wc_ref, ws_ref, zb_ref, zs_ref, zst_ref = refs
    st = {}

    def weight(n):
        for ref, first in ((wa_ref, 0), (wb_ref, W_SPLIT[0]), (wc_ref, W_SPLIT[1])):
            local = n - first
            if 0 <= local * PROJ_TN < ref.shape[1]:
                return ref[:, local * PROJ_TN:(local + 1) * PROJ_TN]
        raise ValueError(n)

    def start():
        r = _cond_row(pl.program_id(0), cond_row0, tiles_per_cond)
        sh = mod_ref[0, pl.ds(r, 1), :]
        sc = mod_ref[1, pl.ds(r, 1), :]
        st["hm"] = (x_ref[...] * (1.0 + sc) + sh).astype(BF16)
        zs = _dot(st["hm"], ws_ref[...])
        zs_ref[...] = zs
        zst_ref[...] = zs.T[COL_I[0]:COL_I[0] + 16, :]

    def slab(n):
        cols = slice(n * PROJ_TN, (n + 1) * PROJ_TN)
        z = _dot(st["hm"], weight(n))
        if n == SLAB_R_GLA:
            z = _silu(z)
        elif n in (SLAB_O_MLSTM, SLAB_G_GLA, SLAB_G_MLSTM):
            z = jax.nn.sigmoid(z)
        zb_ref[:, cols] = z.astype(BF16)

    return [start] + [functools.partial(slab, n) for n in range(Z_BIG // PROJ_TN)]


def _proj_call(x, ada4, w_big, w_small, *, layer, cond_row0, rows_per_cond, tm=PROJ_TM):
    m = x.shape[0]
    w_a, w_b, w_c = w_big
    assert (W_SPLIT[0] * PROJ_TN + w_b.shape[-1] + w_c.shape[-1] == Z_BIG
            and w_b.shape[-1] == (W_SPLIT[1] - W_SPLIT[0]) * PROJ_TN)
    pieces = functools.partial(_proj_pieces, cond_row0=cond_row0, tiles_per_cond=rows_per_cond // tm)

    def kern(*refs):
        for p in pieces(refs):
            p()

    once = pl.Buffered(1)
    return dict(
        kernel=kern, pieces=pieces, grid=(m // tm,),
        in_specs=[
            pl.BlockSpec((tm, D_MODEL), lambda i: (i, 0)),
            pl.BlockSpec((None, 3, 16, D_MODEL), lambda i: (layer, 1, 0, 0)),
            pl.BlockSpec((None, D_MODEL, W_SPLIT[0] * PROJ_TN), lambda i: (layer, 0, 0), pipeline_mode=once),
            pl.BlockSpec((None, D_MODEL, w_b.shape[-1]), lambda i: (layer, 0, 0), pipeline_mode=once),
            pl.BlockSpec((None, D_MODEL, w_c.shape[-1]), lambda i: (layer, 0, 0), pipeline_mode=once),
            pl.BlockSpec((None, D_MODEL, Z_SMALL), lambda i: (layer, 0, 0), pipeline_mode=once),
        ],
        args=[x, ada4, w_a, w_b, w_c, w_small],
        out_specs=[
            pl.BlockSpec((tm, Z_BIG), lambda i: (i, 0)),
            pl.BlockSpec((tm, Z_SMALL), lambda i: (i, 0)),
            pl.BlockSpec((16, tm), lambda i: (0, i)),
        ],
        out_shape=[
            jax.ShapeDtypeStruct((m, Z_BIG), BF16),
            jax.ShapeDtypeStruct((m, Z_SMALL), F32),
            jax.ShapeDtypeStruct((16, m), F32),
        ],
        scratch=[], aliases={}, name="mixer_in_proj")


def _gla_kernel(*refs, seq_len, has_state, emit_state, n_carried, hooks=()):
    it = iter(refs)
    q_ref, k_ref, v_ref, zs_ref, wdec_ref, bdec_ref = (next(it) for _ in range(6))
    s0_ref = next(it) if has_state else None
    if n_carried:
        next(it)
    oacc_s = next(it)
    sout_ref = next(it) if emit_state else None
    la_s, st_s, cum_s = (next(it) for _ in range(3))
    t_len = seq_len
    c_len = GLA_CHUNK
    n_chunks = t_len // c_len
    unrolled = n_chunks <= MAX_UNROLLED_CHUNKS
    assert unrolled or not hooks
    hook, flush = _hook_runner(hooks)
    hook(4)

    for rb in range(t_len // FIN_ROWS):
        rows = pl.ds(rb * FIN_ROWS, FIN_ROWS)
        zh, zm, _ = _split3(zs_ref[rows, :])
        for d in range(N_DIR):
            wh, wl = _split2(wdec_ref[d])
            x = _dot(zh, wh) + _dot(zh, wl) + _dot(zm, wh) + bdec_ref[d]
            la_s[d, rows, :] = _log_sigmoid(x) * (1.0 / GLA_TAU)

    for d in range(N_DIR):
        for h in range(H_GLA):
            if has_state:
                st_s[d, h] = s0_ref[d, h].T
            else:
                st_s[d, h] = jnp.zeros((DV_GLA, DK_GLA), F32)
    oacc_s[...] = jnp.zeros_like(oacc_s)

    ri = lax.broadcasted_iota(jnp.int32, (c_len, c_len), 0)
    ci = lax.broadcasted_iota(jnp.int32, (c_len, c_len), 1)
    rr = lax.broadcasted_iota(jnp.int32, (c_len, DK_GLA), 0)
    odd_rows = (lax.shift_right_logical(rr, 4) & 1) == 1
    upper_rows = rr >= 2 * GLA_SUB
    same_sub = lax.shift_right_logical(ri, 4) == lax.shift_right_logical(ci, 4)
    same_half = lax.shift_right_logical(ri, 5) == lax.shift_right_logical(ci, 5)
    causal = (ri >= ci, ri <= ci)
    tri = tuple(jnp.where(c, 1.0, 0.0).astype(BF16) for c in causal)
    pm0 = tuple(same_sub & c for c in causal)
    q1rows = (odd_rows, jnp.logical_not(odd_rows))
    q2rows = (upper_rows, jnp.logical_not(upper_rows))

    def bcast(x, n):
        return jnp.broadcast_to(x, (n, DK_GLA))

    for d in range(N_DIR):
        for cb in range(n_chunks):
            rows = pl.ds(cb * c_len, c_len)
            l_hi, l_lo = _split2(la_s[d, rows, :])
            cum_s[d, rows, :] = _dot(tri[d], l_hi) + _dot(tri[d], l_lo)
    hook()

    def scores(d, h, row0):
        rows = pl.ds(row0, c_len)
        ls = slice(DK_GLA * h, DK_GLA * (h + 1))
        vs = slice(DV_GLA * h, DV_GLA * (h + 1))

        def row(i):
            g = i - i % SUBLANE
            return cum_s[d, pl.ds(_aligned(row0 + g, SUBLANE), SUBLANE), ls][i - g:i - g + 1, :]

        c = cum_s[d, rows, ls]
        zero = jnp.zeros((GLA_SUB, DK_GLA), F32)
        if d == 0:
            ref0 = jnp.concatenate([zero, bcast(row(15), 16), bcast(row(31), 16), bcast(row(47), 16)], axis=0)
            ref1 = jnp.concatenate([bcast(row(15), 32), bcast(row(47), 32)], axis=0)
            ref2, cend = row(31), row(63)
        else:
            ref0 = jnp.concatenate([bcast(row(16), 16), bcast(row(32), 16), bcast(row(48), 16), zero], axis=0)
            ref1 = jnp.concatenate([bcast(row(16), 32), bcast(row(48), 32)], axis=0)
            ref2, cend = row(32), row(0)
        q = q_ref[rows, ls].astype(F32)
        k = k_ref[rows, ls].astype(F32)
        e0 = c - ref0
        p0 = _dot_nt((q * jnp.exp(e0)).astype(BF16), (k * jnp.exp(-e0)).astype(BF16))
        x1 = jnp.exp(-jnp.abs(c - ref1))
        p1 = _dot_nt(jnp.where(q1rows[d], q * x1, 0.0).astype(BF16),
                     jnp.where(q1rows[d], 0.0, k * x1).astype(BF16))
        x2 = jnp.exp(-jnp.abs(c - ref2))
        p2 = _dot_nt(jnp.where(q2rows[d], q * x2, 0.0).astype(BF16),
                     jnp.where(q2rows[d], 0.0, k * x2).astype(BF16))
        qi = (q * jnp.exp(c)).astype(BF16)
        kst = (k * jnp.exp(cend - c)).astype(BF16)
        return dict(d=d, h=h, rows=rows, vs=vs, p0=p0, p1=p1, p2=p2, qi=qi, kst=kst, dec=jnp.exp(cend))

    def chunk_scores(j):
        row0 = (_aligned(j * c_len, c_len), _aligned((n_chunks - 1 - j) * c_len, c_len))
        return [scores(d, h, row0[d]) for d in range(N_DIR) for h in range(H_GLA)]

    def chunk_update(work):
        for w in work:
            d, h = w["d"], w["h"]
            p = jnp.where(pm0[d], w["p0"], jnp.where(same_half, w["p1"], w["p2"]))
            o = _dot(p.astype(BF16), v_ref[w["rows"], w["vs"]]) + _dot_nt(w["qi"], st_s[d, h].astype(BF16))
            oacc_s[w["rows"], w["vs"]] += o
        hook()
        for w in work:
            d, h = w["d"], w["h"]
            st_s[d, h] = st_s[d, h] * w["dec"] + _dot_tn(v_ref[w["rows"], w["vs"]], w["kst"])

    def body(j, carry):
        work = chunk_scores(j)
        hook()
        chunk_update(work)
        return carry

    def body_multi(jj, carry):
        work = chunk_scores(LOOP_CHUNKS * jj)
        for k in range(1, LOOP_CHUNKS):
            ahead = chunk_scores(LOOP_CHUNKS * jj + k)
            chunk_update(work)
            work = ahead
        chunk_update(work)
        return carry

    if unrolled:
        for j in range(n_chunks):
            body(j, 0)
    else:
        lax.fori_loop(0, n_chunks // LOOP_CHUNKS, body_multi, 0)
    flush()

    if emit_state:
        for d in range(N_DIR):
            for h in range(H_GLA):
                sout_ref[d, h] = st_s[d, h].T


def _gla_call(zb, zs, wdec, bdec, state, carried, *, layer, n_seq, seq_len, emit_state):
    has_state = state is not None
    t = seq_len
    n_carried = 0 if carried is None else 1
    kern = functools.partial(_gla_kernel, seq_len=t, has_state=has_state, emit_state=emit_state,
                             n_carried=n_carried)
    in_specs = [
        pl.BlockSpec((t, QK_GLA), lambda b: (b, 0)),
        pl.BlockSpec((t, QK_GLA), lambda b: (b, 1)),
        pl.BlockSpec((t, V_GLA), lambda b: (b, SLAB_V_GLA)),
        pl.BlockSpec((t, Z_SMALL), lambda b: (b, 0)),
        pl.BlockSpec((None, N_DIR, Z_SMALL, QK_GLA), lambda b: (layer, 0, 0, 0)),
        pl.BlockSpec((None, N_DIR, 1, QK_GLA), lambda b: (layer, 0, 0, 0)),
    ]
    args = [zb, zb, zb, zs, wdec, bdec]
    if has_state:
        in_specs.append(pl.BlockSpec((None, None, N_DIR, H_GLA, DK_GLA, DV_GLA),
                                     lambda b: (b, layer, 0, 0, 0, 0)))
        args.append(state)
    aliases = {}
    if n_carried:
        aliases[len(args)] = 1
        in_specs.append(pl.BlockSpec(memory_space=pl.ANY))
        args.append(carried)
    out_specs = [pl.BlockSpec((t, V_GLA), lambda b: (b, 0))]
    out_shape = [jax.ShapeDtypeStruct((n_seq * t, V_GLA), F32)]
    if emit_state:
        out_specs.append(pl.BlockSpec((None, None, N_DIR, H_GLA, DK_GLA, DV_GLA),
                                      lambda b: (b, layer, 0, 0, 0, 0)))
        out_shape.append(jax.ShapeDtypeStruct((n_seq, DEPTH, N_DIR, H_GLA, DK_GLA, DV_GLA), F32))
    scratch = [
        pltpu.VMEM((N_DIR, t, QK_GLA), F32),
        pltpu.VMEM((N_DIR, H_GLA, DV_GLA, DK_GLA), F32),
        pltpu.VMEM((N_DIR, t, QK_GLA), F32),
    ]
    return dict(kernel=kern, grid=(n_seq,), in_specs=in_specs, args=args, out_specs=out_specs,
                out_shape=out_shape, scratch=scratch, aliases=aliases, name="gla_scan")


def _mlstm_kernel(*refs, seq_len, has_state, emit_state, n_carried, reduce_on_mxu, hooks=()):
    it = iter(refs)
    (q_ref, k_ref, v_ref, zs_ref, zst_ref, wconv_ref, bconv_ref,
     bcol_ref, brow_ref) = (next(it) for _ in range(9))
    if has_state:
        c0_ref, n0_ref, m0_ref = (next(it) for _ in range(3))
    for _ in range(n_carried):
        next(it)
    hacc_s = next(it)
    if emit_state:
        cout_ref, nout_ref, mout_ref = (next(it) for _ in range(3))
    (qc_s, kc_s, xp_s, lf_s, lft_s, c_s, n_s, m_s, cumc_s, cumr_s) = (next(it) for _ in range(10))
    t_len = seq_len
    c_len = MLSTM_CHUNK
    n_chunks = t_len // c_len
    dh = DH_MLSTM
    unrolled = n_chunks <= MAX_UNROLLED_CHUNKS
    assert unrolled or not hooks
    hook, flush = _hook_runner(hooks)
    hook(6)

    xp_s[pl.ds(0, 8), :] = jnp.zeros((8, dh), F32)
    xp_s[pl.ds(8 + t_len, 8), :] = jnp.zeros((8, dh), F32)
    for which, src in enumerate((q_ref, k_ref)):
        for h in range(H_MLSTM):
            hs = slice(dh * h, dh * (h + 1))
            ws = slice(which * W_MLSTM + dh * h, which * W_MLSTM + dh * (h + 1))
            xp_s[pl.ds(8, t_len), :] = src[:, hs].astype(F32)
            w0 = wconv_ref[pl.ds(0, 1), ws]
            w1 = wconv_ref[pl.ds(1, 1), ws]
            w2 = wconv_ref[pl.ds(2, 1), ws]
            bias = bconv_ref[:, ws]
            for rb in range(t_len // FIN_ROWS):
                r0 = rb * FIN_ROWS
                y = (w0 * xp_s[pl.ds(r0 + 7, FIN_ROWS), :] + w1 * xp_s[pl.ds(r0 + 8, FIN_ROWS), :]
                     + w2 * xp_s[pl.ds(r0 + 9, FIN_ROWS), :] + bias)
                a = _silu(y)
                if which == 0:
                    qc_s[pl.ds(r0, FIN_ROWS), hs] = (a * (dh ** -0.5)).astype(BF16)
                else:
                    kc_s[pl.ds(r0, FIN_ROWS), hs] = a

    lf_s[...] = _log_sigmoid(zs_ref[...] + bcol_ref[...])
    zt = zst_ref[...]
    trow = lax.broadcasted_iota(jnp.int32, zt.shape, 0)
    is_f = (lax.shift_right_logical(trow, 2) & 1) == 1
    lft_s[...] = jnp.where(is_f, _log_sigmoid(zt + brow_ref[...]), zt)

    for d in range(N_DIR):
        for h in range(H_MLSTM):
            if has_state:
                c_s[d, h] = c0_ref[d, h]
                n_s[d, h] = n0_ref[d, pl.ds(h, 1), :]
                m_s[d, h] = jnp.broadcast_to(m0_ref[pl.ds(d, 1), pl.ds(h, 1)], (1, LANE))
            else:
                c_s[d, h] = jnp.zeros((dh, dh), F32)
                n_s[d, h] = jnp.zeros((1, dh), F32)
                m_s[d, h] = jnp.zeros((1, LANE), F32)
    hacc_s[...] = jnp.zeros_like(hacc_s)

    ri = lax.broadcasted_iota(jnp.int32, (c_len, c_len), 0)
    ci = lax.broadcasted_iota(jnp.int32, (c_len, c_len), 1)
    causal = (ri >= ci, ri <= ci)
    tri_c = tuple(jnp.where(c, 1.0, 0.0).astype(BF16) for c in causal)
    tri_r = (tri_c[1], tri_c[0])

    for d in range(N_DIR):
        for cb in range(n_chunks):
            rows = pl.ds(cb * c_len, c_len)
            l3 = _split3(lf_s[rows, :])
            cumc_s[d, rows, :] = _dot(tri_c[d], l3[0]) + _dot(tri_c[d], l3[1]) + _dot(tri_c[d], l3[2])
            r3 = _split3(lft_s[:, rows])
            cumr_s[d, :, rows] = _dot(r3[0], tri_r[d]) + _dot(r3[1], tri_r[d]) + _dot(r3[2], tri_r[d])

    ones_c = jnp.ones((c_len, LANE), BF16)

    def lanes2(x):
        return jnp.concatenate([x, x], axis=1)

    def matmuls(d, h, row0):
        rows = pl.ds(row0, c_len)
        hs = slice(dh * h, dh * (h + 1))
        q = qc_s[rows, hs]
        s_raw = _dot_nt(q, kc_s[rows, hs].astype(BF16))
        qc = _dot(q, c_s[d, h].astype(BF16))
        if reduce_on_mxu:
            n_rep = jnp.broadcast_to(n_s[d, h], (LANE, dh)).astype(BF16)
            qn = _dot_nt(q, n_rep)
        else:
            qn = jnp.sum(q.astype(F32) * n_s[d, h], axis=1, keepdims=True)
        return dict(d=d, h=h, rows=rows, row0=row0, hs=hs, s_raw=s_raw, qc=qc, qn=qn)

    def gates(w):
        d, h, rows, row0 = w["d"], w["h"], w["rows"], w["row0"]
        cf = COL_F[d] + h
        end = c_len - 1 if d == 0 else 0
        cum_col = cumc_s[d, rows, pl.ds(cf, 1)]
        cum_c = jnp.broadcast_to(cum_col, (c_len, LANE))
        b_c = jnp.broadcast_to(zs_ref[rows, pl.ds(COL_I[d] + h, 1)] - cum_col, (c_len, LANE))
        g_end = end - end % SUBLANE
        cum_end = cumc_s[d, pl.ds(_aligned(row0 + g_end, SUBLANE), SUBLANE), pl.ds(cf, 1)][
            end - g_end:end - g_end + 1, :]
        b_r = lft_s[pl.ds(ROW_I[d] + h, 1), rows] - cumr_s[d, pl.ds(ROW_F[d] + h, 1), rows]
        m_prev = m_s[d, h]
        b_m = jnp.where(causal[d], b_r, -jnp.inf)
        m_rel = jnp.maximum(m_prev, jnp.broadcast_to(jnp.max(b_m, axis=1, keepdims=True), (c_len, LANE)))
        w["dmat"] = jnp.exp(b_m - m_rel)
        w["inter"] = jnp.exp(m_prev - m_rel)
        w["floor"] = jnp.exp(-(cum_c + m_rel))
        log_w = cum_end + b_c
        m_new = jnp.maximum(cum_end + m_prev, jnp.max(log_w, axis=0, keepdims=True))
        w["wgt"] = jnp.exp(log_w - m_new)
        w["decay"] = jnp.exp(cum_end + m_prev - m_new)
        w["m_new"] = m_new

    def body(j, carry):
        row0 = (_aligned(j * c_len, c_len), _aligned((n_chunks - 1 - j) * c_len, c_len))
        work = [matmuls(d, h, row0[d]) for d in range(N_DIR) for h in range(H_MLSTM)]
        hook()
        for w in work:
            gates(w)
        for w in work:
            s = w["s_raw"] * w["dmat"]
            s_hi = s.astype(BF16)
            w["sv"] = _dot(s_hi, v_ref[w["rows"], w["hs"]])
            if reduce_on_mxu:
                s_lo = (s - s_hi.astype(F32)).astype(BF16)
                w["rsum"] = _dot(s_hi, ones_c) + _dot(s_lo, ones_c)
            else:
                w["rsum"] = jnp.sum(s, axis=1, keepdims=True)
        hook()
        for w in work:
            den = w["rsum"] + w["inter"] * w["qn"]
            rn = 1.0 / jnp.maximum(jnp.abs(den), w["floor"])
            hacc_s[w["rows"], w["hs"]] += w["sv"] * lanes2(rn) + w["qc"] * lanes2(w["inter"] * rn)
        for w in work:
            d, h = w["d"], w["h"]
            kw = kc_s[w["rows"], w["hs"]] * lanes2(w["wgt"])
            decay = w["decay"]
            c_s[d, h] = decay[:, :1] * c_s[d, h] + _dot_tn(kw.astype(BF16), v_ref[w["rows"], w["hs"]])
            n_s[d, h] = lanes2(decay) * n_s[d, h] + jnp.sum(kw, axis=0, keepdims=True)
            m_s[d, h] = w["m_new"]
        return carry

    if unrolled:
        for j in range(n_chunks):
            body(j, 0)
    else:
        lax.fori_loop(0, n_chunks, body, 0)
    flush()

    if emit_state:
        for d in range(N_DIR):
            for h in range(H_MLSTM):
                cout_ref[d, h] = c_s[d, h]
                nout_ref[d, pl.ds(h, 1), :] = n_s[d, h]
                mout_ref[pl.ds(d, 1), pl.ds(h, 1)] = m_s[d, h][:, :1]


def _mlstm_call(zb, zs, zst, wconv, bconv, bcol, brow, states, carried, *, layer, n_seq, seq_len,
                emit_state, reduce_on_mxu):
    has_state = states is not None
    t = seq_len
    dh = DH_MLSTM
    n_carried = 0 if carried is None else 3
    kern = functools.partial(_mlstm_kernel, seq_len=t, has_state=has_state, emit_state=emit_state,
                             n_carried=n_carried, reduce_on_mxu=reduce_on_mxu)
    in_specs = [
        pl.BlockSpec((t, W_MLSTM), lambda b: (b, SLAB_Q_MLSTM)),
        pl.BlockSpec((t, W_MLSTM), lambda b: (b, SLAB_K_MLSTM)),
        pl.BlockSpec((t, W_MLSTM), lambda b: (b, SLAB_V_MLSTM)),
        pl.BlockSpec((t, Z_SMALL), lambda b: (b, 0)),
        pl.BlockSpec((16, t), lambda b: (0, b)),
        pl.BlockSpec((None, 3, 2 * W_MLSTM), lambda b: (layer, 0, 0)),
        pl.BlockSpec((None, 1, 2 * W_MLSTM), lambda b: (layer, 0, 0)),
        pl.BlockSpec((None, 1, Z_SMALL), lambda b: (layer, 0, 0)),
        pl.BlockSpec((None, 16, 1), lambda b: (layer, 0, 0)),
    ]
    args = [zb, zb, zb, zs, zst, wconv, bconv, bcol, brow]
    if has_state:
        c0, n0, m0 = states
        in_specs += [
            pl.BlockSpec((None, None, N_DIR, H_MLSTM, dh, dh), lambda b: (b, layer, 0, 0, 0, 0)),
            pl.BlockSpec((None, None, N_DIR, H_MLSTM, dh), lambda b: (b, layer, 0, 0, 0)),
            pl.BlockSpec((None, None, N_DIR, H_MLSTM), lambda b: (b, layer, 0, 0)),
        ]
        args += [c0, n0, m0]
    aliases = {}
    for k in range(n_carried):
        aliases[len(args)] = 1 + k
        in_specs.append(pl.BlockSpec(memory_space=pl.ANY))
        args.append(carried[k])
    out_specs = [pl.BlockSpec((t, W_MLSTM), lambda b: (b, 0))]
    out_shape = [jax.ShapeDtypeStruct((n_seq * t, W_MLSTM), F32)]
    if emit_state:
        out_specs += [
            pl.BlockSpec((None, None, N_DIR, H_MLSTM, dh, dh), lambda b: (b, layer, 0, 0, 0, 0)),
            pl.BlockSpec((None, None, N_DIR, H_MLSTM, dh), lambda b: (b, layer, 0, 0, 0)),
            pl.BlockSpec((None, None, N_DIR, H_MLSTM), lambda b: (b, layer, 0, 0)),
        ]
        out_shape += [
            jax.ShapeDtypeStruct((n_seq, DEPTH, N_DIR, H_MLSTM, dh, dh), F32),
            jax.ShapeDtypeStruct((n_seq, DEPTH, N_DIR, H_MLSTM, dh), F32),
            jax.ShapeDtypeStruct((n_seq, DEPTH, N_DIR, H_MLSTM), F32),
        ]
    scratch = [
        pltpu.VMEM((t, W_MLSTM), BF16),
        pltpu.VMEM((t, W_MLSTM), F32),
        pltpu.VMEM((t + 16, dh), F32),
        pltpu.VMEM((t, Z_SMALL), F32),
        pltpu.VMEM((16, t), F32),
        pltpu.VMEM((N_DIR, H_MLSTM, dh, dh), F32),
        pltpu.VMEM((N_DIR, H_MLSTM, 1, dh), F32),
        pltpu.VMEM((N_DIR, H_MLSTM, 1, LANE), F32),
        pltpu.VMEM((N_DIR, t, Z_SMALL), F32),
        pltpu.VMEM((N_DIR, 16, t), F32),
    ]
    return dict(kernel=kern, grid=(n_seq,), in_specs=in_specs, args=args, out_specs=out_specs,
                out_shape=out_shape, scratch=scratch, aliases=aliases, name="mlstm_scan")


def _head_rms(o, g, n_heads, width):
    out = []
    for h in range(n_heads):
        oh = o[:, width * h:width * (h + 1)]
        out.append(oh * lax.rsqrt(jnp.mean(oh * oh, axis=-1, keepdims=True) + NORM_EPS) * g)
    return out


def _mix_kernel(x_ref, og_ref, hm_ref, rg_ref, om_ref, gg_ref, gm_ref, mod_ref, gng_ref, gnm_ref,
                wbg_ref, wbm_ref, wo_ref, lg_ref, lb_ref, o_ref, *, cond_row0, tiles_per_cond):
    r = _cond_row(pl.program_id(0), cond_row0, tiles_per_cond)
    gate = mod_ref[2, pl.ds(r, 1), :]
    halves = [pl.ds(s * MIX_SUB, MIX_SUB) for s in range(MIX_TM // MIX_SUB)]
    branch = []
    for rows in halves:
        o_g = jnp.concatenate(_head_rms(og_ref[rows, :] * (DK_GLA ** -0.5), gng_ref[...], H_GLA, DV_GLA), axis=1)
        a_g = (o_g * rg_ref[rows, :].astype(F32)).astype(BF16)
        h_m = jnp.concatenate(_head_rms(hm_ref[rows, :], gnm_ref[...], H_MLSTM, DH_MLSTM), axis=1)
        a_m = (om_ref[rows, :].astype(F32) * h_m).astype(BF16)
        branch.append((_dot(a_g, wbg_ref[...]), _dot(a_m, wbm_ref[...])))
    for rows, (y_g, y_m) in zip(halves, branch):
        mix = gg_ref[rows, :].astype(F32) * y_g + gm_ref[rows, :].astype(F32) * y_m
        y = _dot(mix.astype(BF16), wo_ref[...])
        o_ref[rows, :] = _layer_norm(ALPHA * x_ref[rows, :] + gate * y, lg_ref[...], lb_ref[...])


def _mix_call(x, og, hm, zb, ada4, gn_gla, gn_ml, wbg, wbm, wo, lng, lnb, *, layer, cond_row0, rows_per_cond):
    m = x.shape[0]
    tm = MIX_TM
    kern = functools.partial(_mix_kernel, cond_row0=cond_row0, tiles_per_cond=rows_per_cond // tm)
    wspec = pl.BlockSpec((None, D_MODEL, D_MODEL), lambda i: (layer, 0, 0), pipeline_mode=pl.Buffered(1))
    return dict(
        kernel=kern, grid=(m // tm,),
        in_specs=[
            pl.BlockSpec((tm, D_MODEL), lambda i: (i, 0)),
            pl.BlockSpec((tm, V_GLA), lambda i: (i, 0)),
            pl.BlockSpec((tm, W_MLSTM), lambda i: (i, 0)),
            pl.BlockSpec((tm, V_GLA), lambda i: (i, SLAB_R_GLA)),
            pl.BlockSpec((tm, W_MLSTM), lambda i: (i, SLAB_O_MLSTM)),
            pl.BlockSpec((tm, D_MODEL), lambda i: (i, SLAB_G_GLA)),
            pl.BlockSpec((tm, D_MODEL), lambda i: (i, SLAB_G_MLSTM)),
            pl.BlockSpec((None, 3, 16, D_MODEL), lambda i: (layer, 1, 0, 0)),
            pl.BlockSpec((None, 1, DV_GLA), lambda i: (layer, 0, 0)),
            pl.BlockSpec((None, 1, DH_MLSTM), lambda i: (layer, 0, 0)),
            wspec, wspec, wspec,
            pl.BlockSpec((None, 1, D_MODEL), lambda i: (3 * layer + 1, 0, 0)),
            pl.BlockSpec((None, 1, D_MODEL), lambda i: (3 * layer + 1, 0, 0)),
        ],
        args=[x, og, hm, zb, zb, zb, zb, ada4, gn_gla, gn_ml, wbg, wbm, wo, lng, lnb],
        out_specs=[pl.BlockSpec((tm, D_MODEL), lambda i: (i, 0))],
        out_shape=[jax.ShapeDtypeStruct((m, D_MODEL), F32)],
        scratch=[], aliases={}, name="mixer_out_ln")


def _grid_pos_embed(t_len):
    rows = t_len // GRID_W
    r = jnp.repeat(jnp.arange(rows), GRID_W).astype(F32)
    col = jnp.tile(jnp.arange(GRID_W), rows).astype(F32)
    nf = D_MODEL // 4
    omega = 1.0 / (10000.0 ** (jnp.arange(nf, dtype=F32) / nf))
    er = r[:, None] * omega
    ec = col[:, None] * omega
    return jnp.concatenate([jnp.sin(er), jnp.cos(er), jnp.sin(ec), jnp.cos(ec)], axis=-1)


def _pack_ffn(w_gate, w_up, w_down):
    return w_gate.astype(BF16), w_up.astype(BF16), w_down.astype(BF16)


def _pack_w_in(w_in):
    o = np.cumsum((0, QK_GLA, QK_GLA, V_GLA, V_GLA, GLA_RANK, GLA_RANK, 2 * W_MLSTM, W_MLSTM, W_MLSTM,
                   H_MLSTM, H_MLSTM, H_MLSTM, H_MLSTM, D_MODEL, D_MODEL))
    o = [int(v) for v in o]
    assert o[4] == W_SPLIT[0] * PROJ_TN
    w16 = w_in.astype(BF16)
    big = (w16, w16[:, :, o[6]:o[9]], w16[:, :, o[13]:o[15]])
    small = jnp.concatenate([w16[:, :, o[4]:o[6]], w16[:, :, o[9]:o[13]]], axis=-1)
    small_p = jnp.pad(small, ((0, 0), (0, 0), (0, Z_SMALL - small.shape[-1])))
    return big, small_p


def kernel(x_prompt, x_sample, c, state_gla_s, state_mlstm_c, state_mlstm_n, state_mlstm_m, c_ctx,
           w_ada, b_ada, ffn1_w_gate, ffn1_w_up, ffn1_w_down, w_in, w_decay, b_decay, w_conv, b_conv,
           f_bias, gla_norm_g, mlstm_norm_g, w_br_gla, w_br_mlstm, w_out,
           ffn2_w_gate, ffn2_w_up, ffn2_w_down, ln_g, ln_b):
    bp, tp, _ = x_prompt.shape
    bs, ts, _ = x_sample.shape
    assert bs + 1 <= 16 and tp % FIN_ROWS == 0 and ts % FIN_ROWS == 0

    cond16 = jnp.zeros((16, D_MODEL), F32).at[0].set(c_ctx).at[1:1 + bs].set(c)
    ada4 = _ada_call(cond16, w_ada, b_ada)

    ffn1 = _pack_ffn(ffn1_w_gate, ffn1_w_up, ffn1_w_down)
    ffn2 = _pack_ffn(ffn2_w_gate, ffn2_w_up, ffn2_w_down)
    w_big, w_small = _pack_w_in(w_in)
    wdec = jnp.zeros((DEPTH, N_DIR, Z_SMALL, QK_GLA), F32)
    wdec = wdec.at[:, 0, 0:GLA_RANK].set(w_decay[:, 0]).at[:, 1, GLA_RANK:2 * GLA_RANK].set(w_decay[:, 1])
    bdec = b_decay.reshape(DEPTH, N_DIR, 1, QK_GLA)
    bcol = jnp.zeros((DEPTH, 1, Z_SMALL), F32)
    brow = jnp.zeros((DEPTH, 16, 1), F32)
    for d in range(N_DIR):
        bcol = bcol.at[:, 0, COL_F[d]:COL_F[d] + H_MLSTM].set(f_bias[:, d])
        brow = brow.at[:, ROW_F[d]:ROW_F[d] + H_MLSTM, 0].set(f_bias[:, d])
    bconv = b_conv.reshape(DEPTH, 1, 2 * W_MLSTM)
    gn_gla = gla_norm_g.reshape(DEPTH, 1, DV_GLA)
    gn_ml = mlstm_norm_g.reshape(DEPTH, 1, DH_MLSTM)
    wbg, wbm, wo = w_br_gla.astype(BF16), w_br_mlstm.astype(BF16), w_out.astype(BF16)
    lng = ln_g.reshape(DEPTH * 3, 1, D_MODEL)
    lnb = ln_b.reshape(DEPTH * 3, 1, D_MODEL)
    pos = _grid_pos_embed(ts)

    m_ctx, m_dec = bp * tp, bs * ts
    ctx = dict(cond_row0=0, rows_per_cond=m_ctx)
    dec = dict(cond_row0=1, rows_per_cond=ts)
    dec_states = (state_gla_s, state_mlstm_c, state_mlstm_n, state_mlstm_m)
    fuse = (m_dec == bp * FUSE_TM and ts % FUSE_TM == 0
            and tp // GLA_CHUNK <= MAX_UNROLLED_CHUNKS and tp // MLSTM_CHUNK <= MAX_UNROLLED_CHUNKS)

    def ffn(x, pos_embed, weights, layer, sub, where, **kw):
        return _ffn_call(x, pos_embed, ada4, *weights, lng, lnb, layer=layer, sub=sub, **where, **kw)

    def proj(x, layer, where, **kw):
        return _proj_call(x, ada4, w_big, w_small, layer=layer, **where, **kw)

    def gla(z, layer, state, carried, n_seq, seq_len):
        return _gla_call(z[0], z[1], wdec, bdec, state, carried, layer=layer, n_seq=n_seq, seq_len=seq_len,
                         emit_state=state is None)

    def mlstm(z, layer, states, carried, n_seq, seq_len, reduce_on_mxu):
        return _mlstm_call(z[0], z[1], z[2], w_conv, bconv, bcol, brow, states, carried, layer=layer,
                           n_seq=n_seq, seq_len=seq_len, emit_state=states is None,
                           reduce_on_mxu=reduce_on_mxu)

    def mix(x, og, hm, z, layer, where):
        return _run(_mix_call(x, og, hm, z[0], ada4, gn_gla, gn_ml, wbg, wbm, wo, lng, lnb,
                              layer=layer, **where))[0]

    xc = x_prompt.reshape(m_ctx, D_MODEL)
    xd = x_sample.reshape(m_dec, D_MODEL)
    carried = None
    for l in range(DEPTH):
        xc = _run(ffn(xc, None, ffn1, l, 0, ctx))[0]
        zc = _run(proj(xc, l, ctx))
        ml_c = mlstm(zc, l, None, None if carried is None else carried[1:], bp, tp, reduce_on_mxu=not fuse)
        gl_c = gla(zc, l, None, None if carried is None else carried[0], bp, tp)
        pos_l = pos if l == 0 else None
        if fuse:
            ml_out, (xd,) = _run_fused(ml_c, ffn(xd, pos_l, ffn1, l, 0, dec, tm=FUSE_TM), "mlstm_scan_ffn")
            gl_out, zd = _run_fused(gl_c, proj(xd, l, dec, tm=FUSE_TM), "gla_scan_in_proj")
        else:
            ml_out, gl_out = _run(ml_c), _run(gl_c)
            xd = _run(ffn(xd, pos_l, ffn1, l, 0, dec))[0]
            zd = _run(proj(xd, l, dec))
        carried = (gl_out[1], ml_out[1], ml_out[2], ml_out[3])
        xc = mix(xc, gl_out[0], ml_out[0], zc, l, ctx)
        xc = _run(ffn(xc, None, ffn2, l, 2, ctx))[0]

        og_d = _run(gla(zd, l, dec_states[0], None, bs, ts))[0]
        hm_d = _run(mlstm(zd, l, dec_states[1:], None, bs, ts, reduce_on_mxu=True))[0]
        xd = mix(xd, og_d, hm_d, zd, l, dec)
        xd = _run(ffn(xd, None, ffn2, l, 2, dec))[0]

    return (xc.reshape(bp, tp, D_MODEL), xd.reshape(bs, ts, D_MODEL)) + tuple(carried)
```
